```python
import math
import jax
import jax.numpy as jnp
from jax import lax
import numpy as np

D_MODEL = 1024
BATCH = 16
SEQ = 2048
DEPTH = 1
DEC_BATCH = 128
DEC_SEQ = 4
PAST_LEN = 8192
PAGE_SIZE = 128

HEAD_DIM = 64
N_HEADS_NSA = 8
N_KV_NSA = 2
GROUP = N_HEADS_NSA // N_KV_NSA
N_HEADS_RWKV = 8
NSA_WIDTH = N_HEADS_NSA * HEAD_DIM
RWKV_WIDTH = N_HEADS_RWKV * HEAD_DIM
MIX_WIDTH = NSA_WIDTH + RWKV_WIDTH
KV_WIDTH = N_KV_NSA * HEAD_DIM
CMP_BLOCK = 64
CMP_HIDDEN = 64
SLC_BLOCK = 64
TOP_K_BLOCKS = 16
WINDOW = 512
BAND_BLOCK = 128
SLC_QUERY_CHUNK = 32
N_BUCKETS = 32
MAX_DISTANCE = 128
DECAY_LORA = 64
AAA_LORA = 64
GATE_LORA = 128
D_FF = 2816
NORM_EPS = 1e-6
GN_EPS = 64e-5
ATTN_SCALE = HEAD_DIM ** -0.5
FORCE_SCORE = 1e4
NEG_INF = -1e30
NSA_SPLITS = (NSA_WIDTH,) + (KV_WIDTH,) * 6 + (3 * N_HEADS_NSA,)
RWKV_SPLITS = (RWKV_WIDTH,) * 3 + (DECAY_LORA, AAA_LORA, GATE_LORA)
NSA_COLS = sum(NSA_SPLITS)
RWKV_COLS = sum(RWKV_SPLITS)
IN_COLS = NSA_COLS + RWKV_COLS

kernel_name = 'hymba_nsa_rwkv7_macaron_step'


def split_cols(x, sizes):
    cuts = [int(c) for c in np.cumsum(sizes)[:-1]]
    return jnp.split(x, cuts, axis=-1)


def rms_norm(x, g):
    xf = x.astype(jnp.float32)
    y = xf * lax.rsqrt(jnp.mean(xf * xf, axis=-1, keepdims=True) + NORM_EPS)
    return (y * g.astype(jnp.float32)).astype(x.dtype)


def ffn_half(x, g, wg, wu, wd):
    h = rms_norm(x, g)
    return x + 0.5 * ((jax.nn.silu(h @ wg) * (h @ wu)) @ wd)


def t5_bucket(dist):
    n = jnp.maximum(dist, 0)
    max_exact = N_BUCKETS // 2
    nf = jnp.maximum(n, 1).astype(jnp.float32)
    large = max_exact + (jnp.log(nf / max_exact) / math.log(MAX_DISTANCE / max_exact)
                         * (N_BUCKETS - max_exact)).astype(jnp.int32)
    return jnp.where(n < max_exact, n, jnp.minimum(large, N_BUCKETS - 1))


def rel_bias(dist, table):
    b = table.astype(jnp.float32)[t5_bucket(dist)]
    return jnp.moveaxis(b.reshape(dist.shape + (N_KV_NSA, GROUP)), (-2, -1), (0, 1))


def masked_softmax(logits, mask, axes):
    logits = jnp.where(mask, logits, NEG_INF)
    m = jnp.max(logits, axis=axes, keepdims=True)
    e = jnp.where(mask, jnp.exp(logits - m), 0.0)
    return e / jnp.maximum(jnp.sum(e, axis=axes, keepdims=True), 1e-30)


def attend(q, k, v, dist, mask, table):
    logits = jnp.einsum('ntgrd,nlgd->ngrtl', q, k).astype(jnp.float32) * ATTN_SCALE + rel_bias(dist, table)
    p = masked_softmax(logits, mask, (-1,))
    return jnp.einsum('ngrtl,nlgd->ntgrd', p.astype(v.dtype), v), p


def compress(rows, pe, w1, w2):
    n, length = rows.shape[:2]
    blk = rows.reshape(n, length // CMP_BLOCK, CMP_BLOCK, N_KV_NSA, HEAD_DIM) + pe[:, None, :]
    hid = jnp.einsum('nbcgd,cde->nbge', blk, w1)
    return jax.nn.gelu(hid) @ w2


def cmp_branch(q, q_pos, kcc, vcc, table):
    blk_end = (jnp.arange(kcc.shape[1], dtype=jnp.int32) + 1) * CMP_BLOCK - 1
    dist = q_pos[:, None] - blk_end[None, :]
    return attend(q, kcc, vcc, dist, dist >= 0, table)


def select_blocks(importance, q_pos, n_blocks):
    n_done = importance.shape[-1]
    imp = jnp.pad(importance, ((0, 0), (0, 0), (0, 0), (0, n_blocks - n_done)))
    blk = jnp.arange(n_blocks, dtype=jnp.int32)
    cur = (q_pos // SLC_BLOCK)[:, None]
    forced = (blk == 0) | (blk == cur) | (blk == cur - 1)
    score = jnp.where(blk <= cur, jnp.where(forced, FORCE_SCORE, imp), -FORCE_SCORE)
    _, idx = lax.top_k(score, min(TOP_K_BLOCKS, n_blocks))
    return idx, idx <= cur


def slc_attend(q, q_pos, k_sel, v_sel, idx, valid, table):
    key_pos = idx[..., None] * SLC_BLOCK + jnp.arange(SLC_BLOCK, dtype=jnp.int32)
    dist = q_pos[:, None, None] - key_pos
    mask = valid[..., None] & (dist >= 0)
    tb = table.astype(jnp.float32).reshape(N_BUCKETS, N_KV_NSA, GROUP)
    g_ix = jnp.arange(N_KV_NSA)[None, :, None, None, None]
    bias = jnp.moveaxis(tb[t5_bucket(dist), g_ix], -1, 2)
    logits = jnp.einsum('ntgrd,ngtkld->ngrtkl', q, k_sel).astype(jnp.float32) * ATTN_SCALE + bias
    p = masked_softmax(logits, mask[:, :, None], (-2, -1))
    return jnp.einsum('ngrtkl,ngtkld->ntgrd', p.astype(v_sel.dtype), v_sel)


def select_prompt(q, ks, vs, idx, valid, pos, table):
    n, s = q.shape[:2]
    nsb = s // SLC_BLOCK
    nc = s // SLC_QUERY_CHUNK
    kb = ks.reshape(n, nsb, SLC_BLOCK, N_KV_NSA, HEAD_DIM)
    vb = vs.reshape(n, nsb, SLC_BLOCK, N_KV_NSA, HEAD_DIM)
    n_ix = jnp.arange(n)[:, None, None, None]
    g_ix = jnp.arange(N_KV_NSA)[None, :, None, None]

    def chunk(args):
        qc, ic, vc_, pc = args
        k_sel = kb[n_ix, ic, :, g_ix]
        v_sel = vb[n_ix, ic, :, g_ix]
        return slc_attend(qc, pc, k_sel, v_sel, ic, vc_, table)

    kk = idx.shape[-1]
    xs = (jnp.moveaxis(q.reshape(n, nc, SLC_QUERY_CHUNK, N_KV_NSA, GROUP, HEAD_DIM), 1, 0),
          jnp.moveaxis(idx.reshape(n, N_KV_NSA, nc, SLC_QUERY_CHUNK, kk), 2, 0),
          jnp.moveaxis(valid.reshape(n, N_KV_NSA, nc, SLC_QUERY_CHUNK, kk), 2, 0),
          pos.reshape(nc, SLC_QUERY_CHUNK))
    o = lax.map(chunk, xs)
    return jnp.moveaxis(o, 0, 1).reshape(n, s, N_KV_NSA, GROUP, HEAD_DIM)


def window_prompt(q, kw, vw, table):
    n, s = q.shape[:2]
    nq = s // BAND_BLOCK
    nprev = WINDOW // BAND_BLOCK
    kb_len = (nprev + 1) * BAND_BLOCK

    def band(u):
        up = jnp.pad(u, ((0, 0), (WINDOW, 0), (0, 0), (0, 0))).reshape(n, nprev + nq, BAND_BLOCK, N_KV_NSA, HEAD_DIM)
        ub = jnp.stack([up[:, j:j + nq] for j in range(nprev + 1)], axis=2)
        return jnp.moveaxis(ub.reshape(n, nq, kb_len, N_KV_NSA, HEAD_DIM), 1, 0)

    qb = jnp.moveaxis(q.reshape(n, nq, BAND_BLOCK, N_KV_NSA, GROUP, HEAD_DIM), 1, 0)
    q_pos = jnp.arange(s, dtype=jnp.int32).reshape(nq, BAND_BLOCK)
    k_pos = (jnp.arange(nq, dtype=jnp.int32) * BAND_BLOCK - WINDOW)[:, None] + jnp.arange(kb_len, dtype=jnp.int32)

    def block(args):
        qc, kc, vc, qp, kp = args
        dist = qp[:, None] - kp[None, :]
        mask = (dist >= 0) & (dist <= WINDOW) & (kp >= 0)[None, :]
        return attend(qc, kc, vc, dist, mask, table)[0]

    o = lax.map(block, (qb, band(kw), band(vw), q_pos, k_pos))
    return jnp.moveaxis(o, 0, 1).reshape(n, s, N_KV_NSA, GROUP, HEAD_DIM)


def gather_sample_blocks(pool, new_rows, idx, page_table):
    n, t = new_rows.shape[:2]
    bpp = PAGE_SIZE // SLC_BLOCK
    n_past = PAST_LEN // SLC_BLOCK
    n_new = -(-t // SLC_BLOCK)
    s_ix = jnp.arange(n)[:, None, None, None]
    g_ix = jnp.arange(N_KV_NSA)[None, :, None, None]
    pb = jnp.minimum(idx, n_past - 1)
    phys = page_table[s_ix, pb // bpp]
    past = pool.reshape(pool.shape[0], bpp, SLC_BLOCK, N_KV_NSA, HEAD_DIM)[phys, pb % bpp, :, g_ix]
    new = jnp.pad(new_rows, ((0, 0), (0, n_new * SLC_BLOCK - t), (0, 0), (0, 0)))
    new = new.reshape(n, n_new, SLC_BLOCK, N_KV_NSA, HEAD_DIM)[s_ix, jnp.clip(idx - n_past, 0, n_new - 1), :, g_ix]
    return jnp.where((idx < n_past)[..., None, None], past, new)


def nsa_heads(z):
    n, t = z.shape[:2]
    q, kc, vc, ks, vs, kw, vw, gate = split_cols(z, NSA_SPLITS)
    q = q.reshape(n, t, N_KV_NSA, GROUP, HEAD_DIM)
    kv = tuple(u.reshape(n, t, N_KV_NSA, HEAD_DIM) for u in (kc, vc, ks, vs, kw, vw))
    gate = jax.nn.sigmoid(gate.reshape(n, t, 3, N_KV_NSA, GROUP))[..., None]
    return q, kv, gate


def nsa_combine(gate, o_cmp, o_slc, o_win):
    o = gate[:, :, 0] * o_cmp + gate[:, :, 1] * o_slc + gate[:, :, 2] * o_win
    return o.reshape(o.shape[0], o.shape[1], NSA_WIDTH)


def nsa_prompt(q, kv, gate, cmp_w, table):
    kc, vc, ks, vs, kw, vw = kv
    pe_k, w1_k, w2_k, pe_v, w1_v, w2_v = cmp_w
    s = q.shape[1]
    pos = jnp.arange(s, dtype=jnp.int32)
    kcc = compress(kc, pe_k, w1_k, w2_k)
    vcc = compress(vc, pe_v, w1_v, w2_v)
    o_cmp, p_cmp = cmp_branch(q, pos, kcc, vcc, table)
    idx, valid = select_blocks(p_cmp.sum(axis=2), pos, s // SLC_BLOCK)
    o_slc = select_prompt(q, ks, vs, idx, valid, pos, table)
    o_win = window_prompt(q, kw, vw, table)
    wb = min(WINDOW, s)
    return nsa_combine(gate, o_cmp, o_slc, o_win), kw[:, s - wb:], vw[:, s - wb:]


def nsa_sample(q, kv, gate, cache_cmp_k, cache_cmp_v, cache_slc_k, cache_slc_v,
               cache_win_k, cache_win_v, page_table, cmp_w, table):
    kc, vc, ks, vs, kw, vw = kv
    pe_k, w1_k, w2_k, pe_v, w1_v, w2_v = cmp_w
    n, t = q.shape[:2]
    pos = PAST_LEN + jnp.arange(t, dtype=jnp.int32)
    n_new_done = (t // CMP_BLOCK) * CMP_BLOCK

    def cmp_rows(pool, new, pe, w1, w2):
        past = pool[page_table].reshape(n, PAST_LEN, N_KV_NSA, HEAD_DIM)
        return jnp.concatenate([compress(past, pe, w1, w2), compress(new[:, :n_new_done], pe, w1, w2)], axis=1)

    kcc = cmp_rows(cache_cmp_k, kc, pe_k, w1_k, w2_k)
    vcc = cmp_rows(cache_cmp_v, vc, pe_v, w1_v, w2_v)
    o_cmp, p_cmp = cmp_branch(q, pos, kcc, vcc, table)
    idx, valid = select_blocks(p_cmp.sum(axis=2), pos, -(-(PAST_LEN + t) // SLC_BLOCK))
    k_sel = gather_sample_blocks(cache_slc_k, ks, idx, page_table)
    v_sel = gather_sample_blocks(cache_slc_v, vs, idx, page_table)
    o_slc = slc_attend(q, pos, k_sel, v_sel, idx, valid, table)
    wb = cache_win_k.shape[1]
    kw_all = jnp.concatenate([cache_win_k, kw], axis=1)
    vw_all = jnp.concatenate([cache_win_v, vw], axis=1)
    k_pos = PAST_LEN - wb + jnp.arange(wb + t, dtype=jnp.int32)
    dist = pos[:, None] - k_pos[None, :]
    o_win, _ = attend(q, kw_all, vw_all, dist, (dist >= 0) & (dist <= WINDOW), table)
    return nsa_combine(gate, o_cmp, o_slc, o_win), kw_all[:, -wb:], vw_all[:, -wb:]


def rwkv_mix(p, p_prev, wkv0, mu, w0, w2, a0, a2, g2, k_k, k_a, r_k, ln_w, ln_b):
    n, t = p.shape[:2]
    prev = jnp.concatenate([p_prev[:, None, :].astype(p.dtype), p[:, :-1]], axis=1)
    xs = p + (prev - p) * mu
    r, k, v, xw, xa, xg = split_cols(xs, RWKV_SPLITS)
    w = -jax.nn.softplus(-(w0 + jnp.tanh(xw) @ w2)) - 0.5
    a = jax.nn.sigmoid(a0 + xa @ a2)
    g = jax.nn.sigmoid(xg) @ g2
    heads = lambda u: u.astype(jnp.float32).reshape(n, t, N_HEADS_RWKV, HEAD_DIM)
    kk = heads(k * k_k)
    kk = kk / jnp.maximum(jnp.linalg.norm(kk, axis=-1, keepdims=True), 1e-12)
    k = k * (1 + (a - 1) * k_a)
    r, k, v, a = heads(r), heads(k), heads(v), heads(a)
    decay = jnp.exp(-jnp.exp(heads(w)))

    def step(S, inp):
        r_t, k_t, v_t, kk_t, a_t, d_t = inp
        s_kk = jnp.einsum('nhvk,nhk->nhv', S, -kk_t)
        S = S * d_t[:, :, None, :] + s_kk[..., None] * (kk_t * a_t)[:, :, None, :] + v_t[..., None] * k_t[:, :, None, :]
        return S, jnp.einsum('nhvk,nhk->nhv', S, r_t)

    seq = tuple(jnp.moveaxis(u, 1, 0) for u in (r, k, v, kk, a, decay))
    S_last, y = lax.scan(step, wkv0.astype(jnp.float32), seq)
    y = jnp.moveaxis(y, 0, 1)
    mean = jnp.mean(y, axis=-1, keepdims=True)
    var = jnp.mean(jnp.square(y - mean), axis=-1, keepdims=True)
    y = ((y - mean) * lax.rsqrt(var + GN_EPS)).reshape(n, t, RWKV_WIDTH) * ln_w + ln_b
    bonus = (jnp.sum(r * k * r_k, axis=-1, keepdims=True) * v).reshape(n, t, RWKV_WIDTH)
    out = ((y + bonus) * g.astype(jnp.float32)).astype(p.dtype)
    return out, S_last, p[:, -1]


def setup_inputs(seed: int = 0) -> dict:
    key = jax.random.key(seed)
    keys = iter(jax.random.split(key, 48))

    def nrm(shape, scale=1.0):
        return jax.random.normal(next(keys), shape, jnp.float32) * scale

    def unif(shape, lo, hi):
        return jax.random.uniform(next(keys), shape, jnp.float32, lo, hi)

    L = DEPTH
    n_pages = PAST_LEN // PAGE_SIZE
    n_phys = (DEC_BATCH * n_pages * 5 + 3) // 4
    win_buf = min(WINDOW, PAST_LEN)
    pool = (L, n_phys, PAGE_SIZE, N_KV_NSA, HEAD_DIM)
    x_prompt = nrm((BATCH, SEQ, D_MODEL))
    x_sample = nrm((DEC_BATCH, DEC_SEQ, D_MODEL))
    cache_cmp_k = nrm(pool)
    cache_cmp_v = nrm(pool)
    cache_slc_k = nrm(pool)
    cache_slc_v = nrm(pool)
    cache_win_k = nrm((L, DEC_BATCH, win_buf, N_KV_NSA, HEAD_DIM))
    cache_win_v = nrm((L, DEC_BATCH, win_buf, N_KV_NSA, HEAD_DIM))
    state_wkv = nrm((L, DEC_BATCH, N_HEADS_RWKV, HEAD_DIM, HEAD_DIM), 0.5)
    state_shift = nrm((L, DEC_BATCH, RWKV_COLS))
    page_table = jax.random.permutation(next(keys), n_phys)[: DEC_BATCH * n_pages].reshape(DEC_BATCH, n_pages).astype(jnp.int32)
    return {
        'x_prompt': x_prompt,
        'x_sample': x_sample,
        'cache_cmp_k': cache_cmp_k,
        'cache_cmp_v': cache_cmp_v,
        'cache_slc_k': cache_slc_k,
        'cache_slc_v': cache_slc_v,
        'cache_win_k': cache_win_k,
        'cache_win_v': cache_win_v,
        'state_wkv': state_wkv,
        'state_shift': state_shift,
        'page_table': page_table,
        'rel_bias_table': nrm((N_BUCKETS, N_HEADS_NSA), 0.5),
        'ffn1_norm': 1.0 + nrm((L, D_MODEL), 0.02),
        'ffn1_wg': nrm((L, D_MODEL, D_FF), D_MODEL ** -0.5),
        'ffn1_wu': nrm((L, D_MODEL, D_FF), D_MODEL ** -0.5),
        'ffn1_wd': nrm((L, D_FF, D_MODEL), D_FF ** -0.5),
        'mix_norm': 1.0 + nrm((L, D_MODEL), 0.02),
        'w_in': nrm((L, D_MODEL, IN_COLS), D_MODEL ** -0.5),
        'cmp_pe_k': nrm((L, CMP_BLOCK, HEAD_DIM), 0.1),
        'cmp_w1_k': nrm((L, CMP_BLOCK, HEAD_DIM, CMP_HIDDEN), (CMP_BLOCK * HEAD_DIM) ** -0.5),
        'cmp_w2_k': nrm((L, CMP_HIDDEN, HEAD_DIM), CMP_HIDDEN ** -0.5),
        'cmp_pe_v': nrm((L, CMP_BLOCK, HEAD_DIM), 0.1),
        'cmp_w1_v': nrm((L, CMP_BLOCK, HEAD_DIM, CMP_HIDDEN), (CMP_BLOCK * HEAD_DIM) ** -0.5),
        'cmp_w2_v': nrm((L, CMP_HIDDEN, HEAD_DIM), CMP_HIDDEN ** -0.5),
        'shift_mu': unif((L, RWKV_COLS), 0.0, 1.0),
        'decay_w0': unif((L, RWKV_WIDTH), -6.0, 0.0),
        'decay_w2': nrm((L, DECAY_LORA, RWKV_WIDTH), DECAY_LORA ** -0.5),
        'aaa_a0': nrm((L, RWKV_WIDTH), 0.1),
        'aaa_a2': nrm((L, AAA_LORA, RWKV_WIDTH), AAA_LORA ** -0.5),
        'gate_g2': nrm((L, GATE_LORA, RWKV_WIDTH), GATE_LORA ** -0.5),
        'k_k': 0.85 + nrm((L, RWKV_WIDTH), 0.05),
        'k_a': 1.0 + nrm((L, RWKV_WIDTH), 0.05),
        'r_k': nrm((L, N_HEADS_RWKV, HEAD_DIM), 0.1),
        'ln_x_w': 1.0 + nrm((L, RWKV_WIDTH), 0.02),
        'ln_x_b': nrm((L, RWKV_WIDTH), 0.02),
        'w_out': nrm((L, MIX_WIDTH, D_MODEL), MIX_WIDTH ** -0.5),
        'ffn2_norm': 1.0 + nrm((L, D_MODEL), 0.02),
        'ffn2_wg': nrm((L, D_MODEL, D_FF), D_MODEL ** -0.5),
        'ffn2_wu': nrm((L, D_MODEL, D_FF), D_MODEL ** -0.5),
        'ffn2_wd': nrm((L, D_FF, D_MODEL), D_FF ** -0.5),
        'final_norm': 1.0 + nrm((D_MODEL,), 0.02),
    }


def reference(x_prompt, x_sample, cache_cmp_k, cache_cmp_v, cache_slc_k, cache_slc_v,
              cache_win_k, cache_win_v, state_wkv, state_shift, page_table, rel_bias_table,
              ffn1_norm, ffn1_wg, ffn1_wu, ffn1_wd, mix_norm, w_in,
              cmp_pe_k, cmp_w1_k, cmp_w2_k, cmp_pe_v, cmp_w1_v, cmp_w2_v,
              shift_mu, decay_w0, decay_w2, aaa_a0, aaa_a2, gate_g2, k_k, k_a, r_k, ln_x_w, ln_x_b,
              w_out, ffn2_norm, ffn2_wg, ffn2_wu, ffn2_wd, final_norm):
    xp, xs = x_prompt, x_sample
    bp = xp.shape[0]
    layer_states = []
    for l in range(DEPTH):
        cmp_w = (cmp_pe_k[l], cmp_w1_k[l], cmp_w2_k[l], cmp_pe_v[l], cmp_w1_v[l], cmp_w2_v[l])
        rw = (shift_mu[l], decay_w0[l], decay_w2[l], aaa_a0[l], aaa_a2[l], gate_g2[l],
              k_k[l], k_a[l], r_k[l], ln_x_w[l], ln_x_b[l])
        xp = ffn_half(xp, ffn1_norm[l], ffn1_wg[l], ffn1_wu[l], ffn1_wd[l])
        xs = ffn_half(xs, ffn1_norm[l], ffn1_wg[l], ffn1_wu[l], ffn1_wd[l])
        zp = rms_norm(xp, mix_norm[l]) @ w_in[l]
        zs = rms_norm(xs, mix_norm[l]) @ w_in[l]
        qp, kvp, gp = nsa_heads(zp[..., :NSA_COLS])
        o_nsa_p, wk_p, wv_p = nsa_prompt(qp, kvp, gp, cmp_w, rel_bias_table)
        o_rw_p, wkv_p, sh_p = rwkv_mix(zp[..., NSA_COLS:], jnp.zeros((bp, RWKV_COLS), zp.dtype),
                                       jnp.zeros((bp, N_HEADS_RWKV, HEAD_DIM, HEAD_DIM), jnp.float32), *rw)
        qs, kvs, gs = nsa_heads(zs[..., :NSA_COLS])
        o_nsa_s, wk_s, wv_s = nsa_sample(qs, kvs, gs, cache_cmp_k[l], cache_cmp_v[l], cache_slc_k[l], cache_slc_v[l],
                                         cache_win_k[l], cache_win_v[l], page_table, cmp_w, rel_bias_table)
        o_rw_s, wkv_s, sh_s = rwkv_mix(zs[..., NSA_COLS:], state_shift[l], state_wkv[l], *rw)
        xp = xp + jnp.concatenate([o_nsa_p, o_rw_p], axis=-1) @ w_out[l]
        xs = xs + jnp.concatenate([o_nsa_s, o_rw_s], axis=-1) @ w_out[l]
        xp = ffn_half(xp, ffn2_norm[l], ffn2_wg[l], ffn2_wu[l], ffn2_wd[l])
        xs = ffn_half(xs, ffn2_norm[l], ffn2_wg[l], ffn2_wu[l], ffn2_wd[l])
        layer_states.append((kvp[0], kvp[1], kvp[2], kvp[3], wk_p, wv_p, wkv_p, sh_p,
                             kvs[0], kvs[1], kvs[2], kvs[3], wk_s, wv_s, wkv_s, sh_s))
    y_prompt = rms_norm(xp, final_norm)
    y_sample = rms_norm(xs, final_norm)
    (p_cmp_k, p_cmp_v, p_slc_k, p_slc_v, p_win_k, p_win_v, p_wkv, p_shift,
     s_cmp_k, s_cmp_v, s_slc_k, s_slc_v, s_win_k, s_win_v, s_wkv, s_shift) = [jnp.stack(z) for z in zip(*layer_states)]
    return (y_prompt, y_sample, p_cmp_k, p_cmp_v, p_slc_k, p_slc_v, p_win_k, p_win_v, p_wkv, p_shift,
            s_cmp_k, s_cmp_v, s_slc_k, s_slc_v, s_win_k, s_win_v, s_wkv, s_shift)
```

```python
import functools
import math

import jax
import jax.numpy as jnp
import numpy as np
from jax import lax
from jax.experimental import pallas as pl
from jax.experimental.pallas import tpu as pltpu

F32 = jnp.float32
BF16 = jnp.bfloat16

HEAD_DIM = 64
N_HEADS_NSA = 8
N_KV_NSA = 2
GROUP = N_HEADS_NSA // N_KV_NSA
N_HEADS_RWKV = 8
NSA_WIDTH = N_HEADS_NSA * HEAD_DIM
RWKV_WIDTH = N_HEADS_RWKV * HEAD_DIM
KV_WIDTH = N_KV_NSA * HEAD_DIM
CMP_BLOCK = 64
SLC_BLOCK = 64
TOP_K_BLOCKS = 16
WINDOW = 512
PAGE_SIZE = 128
N_BUCKETS = 32
MAX_DISTANCE = 128
DECAY_LORA = 64
AAA_LORA = 64
GATE_LORA = 128
NORM_EPS = 1e-6
GN_EPS = 64e-5
ATTN_SCALE = HEAD_DIM ** -0.5
FORCE_SCORE = 1e4
NEG_INF = -1e30
RWKV_COLS = 3 * RWKV_WIDTH + DECAY_LORA + AAA_LORA + GATE_LORA
N_GATES = 3 * N_HEADS_NSA

LANES = 128
SUBLANES = 8
VMEM_LIMIT = 56 * 1024 * 1024

QT = 128
TC = 128


def _bucket_thresholds():
    max_exact = N_BUCKETS // 2
    thr = list(range(max_exact + 1))
    for b in range(max_exact + 1, N_BUCKETS):
        x = max_exact * (MAX_DISTANCE / max_exact) ** ((b - max_exact) / (N_BUCKETS - max_exact))
        thr.append(int(math.ceil(x - 1e-9)))
    return thr


BUCKET_THR = _bucket_thresholds()


def _cparams(sem):
    return pltpu.CompilerParams(dimension_semantics=sem, vmem_limit_bytes=VMEM_LIMIT)


def _dot(a, b):
    return jnp.dot(a, b, preferred_element_type=F32)


def _dot_nt(a, b):
    return lax.dot_general(a, b, (((1,), (1,)), ((), ())), preferred_element_type=F32)


def _rms(x, g):
    return x * lax.rsqrt(jnp.mean(x * x, axis=-1, keepdims=True) + NORM_EPS) * g


def _ffn_kernel(x_ref, g_ref, wg_ref, wu_ref, wd_ref, gf_ref, o_ref, h_ref, acc_ref, *, final):
    j = pl.program_id(1)

    @pl.when(j == 0)
    def _():
        h_ref[...] = _rms(x_ref[...], g_ref[...]).astype(BF16)
        acc_ref[...] = jnp.zeros_like(acc_ref)

    h = h_ref[...]
    a = _dot(h, wg_ref[...])
    b = _dot(h, wu_ref[...])
    act = (a * jax.nn.sigmoid(a) * b).astype(BF16)
    acc_ref[...] += _dot(act, wd_ref[...])

    @pl.when(j == pl.num_programs(1) - 1)
    def _():
        y = x_ref[...] + 0.5 * acc_ref[...]
        if final:
            y = _rms(y, gf_ref[...])
        o_ref[...] = y


def _ffn_half(x, g, wg, wu, wd, gf, *, final):
    m, d = x.shape
    ff = wg.shape[1]
    tm = min(m, 1024)
    tf = 256 if ff % 256 == 0 else ff
    assert m % tm == 0 and ff % tf == 0
    return pl.pallas_call(
        functools.partial(_ffn_kernel, final=final),
        out_shape=jax.ShapeDtypeStruct((m, d), F32),
        grid=(m // tm, ff // tf),
        in_specs=[
            pl.BlockSpec((tm, d), lambda i, j: (i, 0)),
            pl.BlockSpec((1, d), lambda i, j: (0, 0)),
            pl.BlockSpec((d, tf), lambda i, j: (0, j)),
            pl.BlockSpec((d, tf), lambda i, j: (0, j)),
            pl.BlockSpec((tf, d), lambda i, j: (j, 0)),
            pl.BlockSpec((1, d), lambda i, j: (0, 0)),
        ],
        out_specs=pl.BlockSpec((tm, d), lambda i, j: (i, 0)),
        scratch_shapes=[pltpu.VMEM((tm, d), BF16), pltpu.VMEM((tm, d), F32)],
        compiler_params=_cparams(("parallel", "arbitrary")),
        name="ffn_half",
    )(x, g.reshape(1, d), wg, wu, wd, gf.reshape(1, d))


def _proj_kernel(x_ref, g_ref, w_ref, q_ref, kc_ref, vc_ref, ks_ref, vs_ref, kw_ref, vw_ref, gate_ref, zr_ref):
    h = _rms(x_ref[...], g_ref[...]).astype(BF16)
    z = _dot(h, w_ref[...])
    c = 0
    for ref in (q_ref, kc_ref, vc_ref, ks_ref, vs_ref, kw_ref, vw_ref, gate_ref, zr_ref):
        w = ref.shape[1]
        ref[...] = z[:, c:c + w]
        c += w


def _proj(x, g, w_pad):
    m, d = x.shape
    tm = min(m, 512)
    widths = (NSA_WIDTH,) + (KV_WIDTH,) * 6 + (LANES, RWKV_COLS)
    assert sum(widths) == w_pad.shape[1] and m % tm == 0
    return pl.pallas_call(
        _proj_kernel,
        out_shape=[jax.ShapeDtypeStruct((m, w), F32) for w in widths],
        grid=(m // tm,),
        in_specs=[
            pl.BlockSpec((tm, d), lambda i: (i, 0)),
            pl.BlockSpec((1, d), lambda i: (0, 0)),
            pl.BlockSpec(w_pad.shape, lambda i: (0, 0)),
        ],
        out_specs=[pl.BlockSpec((tm, w), lambda i: (i, 0)) for w in widths],
        compiler_params=_cparams(("parallel",)),
        name="in_proj",
    )(x, g.reshape(1, d), w_pad)


def _out_kernel(x_ref, a_ref, b_ref, wa_ref, wb_ref, o_ref):
    o_ref[...] = (x_ref[...] + _dot(a_ref[...].astype(BF16), wa_ref[...])
                  + _dot(b_ref[...].astype(BF16), wb_ref[...]))


def _out_proj(x, a, b, wa, wb):
    m, d = x.shape
    tm = min(m, 512)
    return pl.pallas_call(
        _out_kernel,
        out_shape=jax.ShapeDtypeStruct((m, d), F32),
        grid=(m // tm,),
        in_specs=[
            pl.BlockSpec((tm, d), lambda i: (i, 0)),
            pl.BlockSpec((tm, a.shape[1]), lambda i: (i, 0)),
            pl.BlockSpec((tm, b.shape[1]), lambda i: (i, 0)),
            pl.BlockSpec(wa.shape, lambda i: (0, 0)),
            pl.BlockSpec(wb.shape, lambda i: (0, 0)),
        ],
        out_specs=pl.BlockSpec((tm, d), lambda i: (i, 0)),
        compiler_params=_cparams(("parallel",)),
        name="out_proj",
    )(x, a, b, wa, wb)


def _bias_lookup(dist, table_ref, heads):
    vals = [jnp.full(dist.shape, table_ref[0, h], F32) for h in heads]
    for b in range(1, N_BUCKETS):
        ge = dist >= BUCKET_THR[b]
        vals = [jnp.where(ge, table_ref[b, h], v) for h, v in zip(heads, vals)]
    return vals


def _bias_tiles_kernel(table_ref, o_ref):
    i = lax.broadcasted_iota(jnp.int32, (QT, QT), 0)
    j = lax.broadcasted_iota(jnp.int32, (QT, QT), 1)
    for off in range(3):
        vals = _bias_lookup(off * QT + i - j, table_ref, range(N_HEADS_NSA))
        for h, v in enumerate(vals):
            g, r = divmod(h, GROUP)
            o_ref[g, off, r * QT:(r + 1) * QT, :] = v


def _bias_tiles(table):
    return pl.pallas_call(
        _bias_tiles_kernel,
        out_shape=jax.ShapeDtypeStruct((N_KV_NSA, 3, GROUP * QT, QT), F32),
        in_specs=[pl.BlockSpec(memory_space=pltpu.SMEM)],
        name="bias_tiles",
    )(table)


def _bias_rows_kernel(table_ref, o_ref, *, q0, key_first, key_stride):
    _, _, rc, L = o_ref.shape
    i = lax.broadcasted_iota(jnp.int32, (rc, L), 0) + pl.program_id(0) * rc
    j = lax.broadcasted_iota(jnp.int32, (rc, L), 1)
    dist = (q0 + i) - (key_first + key_stride * j)
    vals = _bias_lookup(dist, table_ref, range(N_HEADS_NSA))
    for h, v in enumerate(vals):
        g, r = divmod(h, GROUP)
        o_ref[g, r] = v


def _bias_rows(table, *, q0, rows, n_keys, key_first, key_stride):
    rc = min(rows, 256)
    assert rows % rc == 0
    return pl.pallas_call(
        functools.partial(_bias_rows_kernel, q0=q0, key_first=key_first, key_stride=key_stride),
        out_shape=jax.ShapeDtypeStruct((N_KV_NSA, GROUP, rows, n_keys), F32),
        grid=(rows // rc,),
        in_specs=[pl.BlockSpec(memory_space=pltpu.SMEM)],
        out_specs=pl.BlockSpec((N_KV_NSA, GROUP, rc, n_keys), lambda i: (0, 0, i, 0)),
        compiler_params=_cparams(("parallel",)),
        name="bias_rows",
    )(table)


def _gelu_tanh(x):
    return 0.5 * x * (1.0 + jnp.tanh(math.sqrt(2.0 / math.pi) * (x + 0.044715 * (x * x * x))))


def _compress_blocks(load_rows, nblk, pe_ref, w1_ref, w2_ref):
    acc = jnp.zeros((nblk, KV_WIDTH), F32)
    for c in range(CMP_BLOCK):
        rows = load_rows(c) + pe_ref[c:c + 1, :]
        acc = acc + _dot(rows.astype(BF16), w1_ref[c])
    return _dot(_gelu_tanh(acc).astype(BF16), w2_ref[...])


def _compress_kernel(x_ref, pe_ref, w1_ref, w2_ref, o_ref):
    nblk = o_ref.shape[0]
    o_ref[...] = _compress_blocks(lambda c: x_ref[pl.ds(c, nblk, stride=CMP_BLOCK), :],
                                  nblk, pe_ref, w1_ref, w2_ref)


def _compress(x, pe2, w1bd, w2bd):
    rows = x.shape[0]
    nb = rows // CMP_BLOCK
    nblk = min(nb, 128)
    assert nb % nblk == 0
    return pl.pallas_call(
        _compress_kernel,
        out_shape=jax.ShapeDtypeStruct((nb, KV_WIDTH), F32),
        grid=(nb // nblk,),
        in_specs=[
            pl.BlockSpec((nblk * CMP_BLOCK, KV_WIDTH), lambda i: (i, 0)),
            pl.BlockSpec(pe2.shape, lambda i: (0, 0)),
            pl.BlockSpec(w1bd.shape, lambda i: (0, 0, 0)),
            pl.BlockSpec(w2bd.shape, lambda i: (0, 0)),
        ],
        out_specs=pl.BlockSpec((nblk, KV_WIDTH), lambda i: (i, 0)),
        compiler_params=_cparams(("parallel",)),
        name="compress",
    )(x, pe2, w1bd, w2bd)


def _page_copies(pt_ref, n, pool_ref, buf_ref, sem_ref, slot, n_pages):
    return [pltpu.make_async_copy(pool_ref.at[pt_ref[n, p]],
                                  buf_ref.at[slot, pl.ds(p * PAGE_SIZE, PAGE_SIZE)],
                                  sem_ref.at[slot]) for p in range(n_pages)]


def _paged_prefetch(pt_ref, pools, bufs, sems, n_pages):
    n = pl.program_id(0)
    slot = n % 2

    def start(seq, slot_):
        for pool, buf, sem in zip(pools, bufs, sems):
            for cp in _page_copies(pt_ref, seq, pool, buf, sem, slot_, n_pages):
                cp.start()

    @pl.when(n == 0)
    def _():
        start(0, 0)

    @pl.when(n + 1 < pl.num_programs(0))
    def _():
        start(n + 1, 1 - slot)

    for pool, buf, sem in zip(pools, bufs, sems):
        for cp in _page_copies(pt_ref, n, pool, buf, sem, slot, n_pages):
            cp.wait()
    return slot


def _compress_paged_kernel(pt_ref, pk_ref, pv_ref, pek_ref, w1k_ref, w2k_ref, pev_ref, w1v_ref, w2v_ref,
                           ok_ref, ov_ref, bk_ref, bv_ref, sk_ref, sv_ref, *, n_pages):
    slot = _paged_prefetch(pt_ref, (pk_ref, pv_ref), (bk_ref, bv_ref), (sk_ref, sv_ref), n_pages)
    nblk = ok_ref.shape[1]
    ok_ref[0] = _compress_blocks(lambda c: bk_ref[slot, pl.ds(c, nblk, stride=CMP_BLOCK), :],
                                 nblk, pek_ref, w1k_ref, w2k_ref)
    ov_ref[0] = _compress_blocks(lambda c: bv_ref[slot, pl.ds(c, nblk, stride=CMP_BLOCK), :],
                                 nblk, pev_ref, w1v_ref, w2v_ref)


def _compress_paged(page_table, pool_k, pool_v, cw_k, cw_v):
    n, n_pages = page_table.shape
    past = n_pages * PAGE_SIZE
    nblk = past // CMP_BLOCK
    wspec = lambda a: pl.BlockSpec(a.shape, lambda i, pt: (0,) * a.ndim)
    grid_spec = pltpu.PrefetchScalarGridSpec(
        num_scalar_prefetch=1,
        grid=(n,),
        in_specs=[pl.BlockSpec(memory_space=pl.ANY), pl.BlockSpec(memory_space=pl.ANY)]
                 + [wspec(a) for a in cw_k + cw_v],
        out_specs=[pl.BlockSpec((1, nblk, KV_WIDTH), lambda i, pt: (i, 0, 0))] * 2,
        scratch_shapes=[pltpu.VMEM((2, past, KV_WIDTH), F32), pltpu.VMEM((2, past, KV_WIDTH), F32),
                        pltpu.SemaphoreType.DMA((2,)), pltpu.SemaphoreType.DMA((2,))],
    )
    return pl.pallas_call(
        functools.partial(_compress_paged_kernel, n_pages=n_pages),
        out_shape=[jax.ShapeDtypeStruct((n, nblk, KV_WIDTH), F32)] * 2,
        grid_spec=grid_spec,
        compiler_params=_cparams(("arbitrary",)),
        name="compress_paged",
    )(page_table, pool_k, pool_v, *cw_k, *cw_v)


def _stack_heads(x, g):
    return jnp.concatenate(
        [x[:, (g * GROUP + r) * HEAD_DIM:(g * GROUP + r + 1) * HEAD_DIM] for r in range(GROUP)], axis=0)


def _stack_cols(x, cols):
    return jnp.concatenate([x[:, c:c + 1] for c in cols], axis=0)


def _softmax_step(carry, s, mask, v):
    m, l, acc = carry
    s = jnp.where(mask, s, NEG_INF)
    m_new = jnp.maximum(m, jnp.max(s, axis=1, keepdims=True))
    p = jnp.where(mask, jnp.exp(s - m_new), 0.0)
    alpha = jnp.exp(m - m_new)
    l = alpha * l + jnp.sum(p, axis=1, keepdims=True)
    acc = alpha * acc + _dot(p.astype(BF16), v)
    return m_new, l, acc


def _softmax_init(rows):
    return (jnp.full((rows, 1), NEG_INF, F32), jnp.zeros((rows, 1), F32), jnp.zeros((rows, HEAD_DIM), F32))


def _softmax_done(carry):
    _, l, acc = carry
    return acc / jnp.maximum(l, 1e-30)


def _topk_mask(imp, cur, n_blocks):
    blk = lax.broadcasted_iota(jnp.int32, imp.shape, 1)
    forced = (blk == 0) | (blk == cur) | (blk == cur - 1)
    score = jnp.where(blk <= cur, jnp.where(forced, FORCE_SCORE, imp), -FORCE_SCORE)
    rank = jnp.zeros(imp.shape, jnp.int32)
    for b in range(n_blocks):
        col = score[:, b:b + 1]
        ahead = (col > score) | ((col == score) & (blk > b))
        rank = rank + jnp.where(ahead, 1, 0)
    return (rank < TOP_K_BLOCKS) & (blk <= cur)


def _nsa_prompt_kernel(q_ref, gate_ref, ks_ref, vs_ref, kw_ref, vw_ref, kcc_ref, vcc_ref, bt_ref, bc_ref,
                       o_ref, selx_ref):
    qi = pl.program_id(1)
    seq = ks_ref.shape[1]
    nb = seq // SLC_BLOCK
    rows = GROUP * QT
    q = q_ref[...]
    gates = jax.nn.sigmoid(gate_ref[...])
    row = lax.broadcasted_iota(jnp.int32, (rows, 1), 0)
    tpos = qi * QT + (row & (QT - 1))
    tpos1 = qi * QT + lax.broadcasted_iota(jnp.int32, (QT, 1), 0)
    lane = lax.broadcasted_iota(jnp.int32, (1, QT), 1)
    expand = (lax.broadcasted_iota(jnp.int32, (nb, seq), 1) // SLC_BLOCK
              == lax.broadcasted_iota(jnp.int32, (nb, seq), 0)).astype(BF16)
    win_lo = jnp.maximum(qi - WINDOW // QT, 0)

    for g in range(N_KV_NSA):
        cols = slice(g * HEAD_DIM, (g + 1) * HEAD_DIM)
        qg = _stack_heads(q, g).astype(BF16)

        blk_end = (lax.broadcasted_iota(jnp.int32, (1, nb), 1) + 1) * CMP_BLOCK - 1
        s = _dot_nt(qg, kcc_ref[0][:, cols].astype(BF16)) * ATTN_SCALE + bc_ref[g, 0]
        mask = blk_end <= tpos
        s = jnp.where(mask, s, NEG_INF)
        e = jnp.where(mask, jnp.exp(s - jnp.max(s, axis=1, keepdims=True)), 0.0)
        p = e / jnp.maximum(jnp.sum(e, axis=1, keepdims=True), 1e-30)
        o_cmp = _dot(p.astype(BF16), vcc_ref[0][:, cols].astype(BF16))

        imp = p[0:QT]
        for r in range(1, GROUP):
            imp = imp + p[r * QT:(r + 1) * QT]
        sel = _topk_mask(imp, tpos1 // SLC_BLOCK, nb)
        selx_ref[...] = _dot(jnp.where(sel, 1.0, 0.0).astype(BF16), expand)

        def bias(kt):
            return bt_ref[g, jnp.minimum(qi - kt, 2)]

        def slc_step(kt, carry):
            off = pl.multiple_of(kt * QT, QT)
            k = ks_ref[0, pl.ds(off, QT), cols].astype(BF16)
            v = vs_ref[0, pl.ds(off, QT), cols].astype(BF16)
            s = _dot_nt(qg, k) * ATTN_SCALE + bias(kt)
            chosen = jnp.concatenate([selx_ref[:, pl.ds(off, QT)]] * GROUP, axis=0) > 0.5
            mask = chosen & (tpos - (off + lane) >= 0)
            return _softmax_step(carry, s, mask, v)

        o_slc = _softmax_done(lax.fori_loop(0, qi + 1, slc_step, _softmax_init(rows)))

        def win_step(kt, carry):
            off = pl.multiple_of(kt * QT, QT)
            k = kw_ref[0, pl.ds(off, QT), cols].astype(BF16)
            v = vw_ref[0, pl.ds(off, QT), cols].astype(BF16)
            s = _dot_nt(qg, k) * ATTN_SCALE + bias(kt)
            dist = tpos - (off + lane)
            mask = (dist >= 0) & (dist <= WINDOW)
            return _softmax_step(carry, s, mask, v)

        o_win = _softmax_done(lax.fori_loop(win_lo, qi + 1, win_step, _softmax_init(rows)))

        heads = [g * GROUP + r for r in range(GROUP)]
        o = (_stack_cols(gates, heads) * o_cmp
             + _stack_cols(gates, [N_HEADS_NSA + h for h in heads]) * o_slc
             + _stack_cols(gates, [2 * N_HEADS_NSA + h for h in heads]) * o_win)
        for r, h in enumerate(heads):
            o_ref[:, h * HEAD_DIM:(h + 1) * HEAD_DIM] = o[r * QT:(r + 1) * QT]


def _nsa_prompt(q, gate, ks, vs, kw, vw, kcc, vcc, bias_tiles, bias_cmp, n, seq):
    nq = seq // QT
    nb = seq // SLC_BLOCK
    seq_spec = pl.BlockSpec((1, seq, KV_WIDTH), lambda b, i: (b, 0, 0))
    cmp_spec = pl.BlockSpec((1, nb, KV_WIDTH), lambda b, i: (b, 0, 0))
    return pl.pallas_call(
        _nsa_prompt_kernel,
        out_shape=jax.ShapeDtypeStruct((n * seq, NSA_WIDTH), F32),
        grid=(n, nq),
        in_specs=[
            pl.BlockSpec((QT, NSA_WIDTH), lambda b, i: (b * nq + i, 0)),
            pl.BlockSpec((QT, LANES), lambda b, i: (b * nq + i, 0)),
            seq_spec, seq_spec, seq_spec, seq_spec, cmp_spec, cmp_spec,
            pl.BlockSpec(bias_tiles.shape, lambda b, i: (0, 0, 0, 0)),
            pl.BlockSpec((N_KV_NSA, 1, GROUP * QT, nb), lambda b, i: (0, i, 0, 0)),
        ],
        out_specs=pl.BlockSpec((QT, NSA_WIDTH), lambda b, i: (b * nq + i, 0)),
        scratch_shapes=[pltpu.VMEM((QT, seq), F32)],
        compiler_params=_cparams(("parallel", "arbitrary")),
        name="nsa_prompt",
    )(q, gate, ks.reshape(n, seq, KV_WIDTH), vs.reshape(n, seq, KV_WIDTH), kw.reshape(n, seq, KV_WIDTH),
      vw.reshape(n, seq, KV_WIDTH), kcc.reshape(n, nb, KV_WIDTH), vcc.reshape(n, nb, KV_WIDTH),
      bias_tiles, bias_cmp)


NEW_PAD = SUBLANES


def _row_token(t):
    return jnp.concatenate([lax.broadcasted_iota(jnp.int32, (t, 1), 0)] * GROUP, axis=0)


def _nsa_sample_cmp_kernel(q_ref, kcc_ref, vcc_ref, bc_ref, o_ref, sel_ref, *, past):
    t = q_ref.shape[1]
    nbp = kcc_ref.shape[1]
    width = sel_ref.shape[3]
    q = q_ref[0]
    pos = past + _row_token(t)
    blk_end = (lax.broadcasted_iota(jnp.int32, (1, nbp), 1) + 1) * CMP_BLOCK - 1
    cur = (past + lax.broadcasted_iota(jnp.int32, (t, 1), 0)) // SLC_BLOCK
    for g in range(N_KV_NSA):
        cols = slice(g * HEAD_DIM, (g + 1) * HEAD_DIM)
        qg = _stack_heads(q, g).astype(BF16)
        s = _dot_nt(qg, kcc_ref[0][:, cols].astype(BF16)) * ATTN_SCALE + bc_ref[g]
        mask = blk_end <= pos
        s = jnp.where(mask, s, NEG_INF)
        e = jnp.where(mask, jnp.exp(s - jnp.max(s, axis=1, keepdims=True)), 0.0)
        p = e / jnp.maximum(jnp.sum(e, axis=1, keepdims=True), 1e-30)
        o_ref[0, g] = _dot(p.astype(BF16), vcc_ref[0][:, cols].astype(BF16))
        imp = p[0:t]
        for r in range(1, GROUP):
            imp = imp + p[r * t:(r + 1) * t]
        imp = jnp.concatenate([imp, jnp.zeros((t, width - nbp), F32)], axis=1)
        sel = _topk_mask(imp, cur, nbp + 1)
        sel_ref[0, g] = jnp.where(sel, 1.0, 0.0)


def _nsa_sample_cmp(q3, kcc, vcc, bias_cmp, past):
    n, t, _ = q3.shape
    nbp = kcc.shape[1]
    width = -(-(nbp + 1) // LANES) * LANES
    return pl.pallas_call(
        functools.partial(_nsa_sample_cmp_kernel, past=past),
        out_shape=[jax.ShapeDtypeStruct((n, N_KV_NSA, GROUP * t, HEAD_DIM), F32),
                   jax.ShapeDtypeStruct((n, N_KV_NSA, t, width), F32)],
        grid=(n,),
        in_specs=[
            pl.BlockSpec((1, t, NSA_WIDTH), lambda i: (i, 0, 0)),
            pl.BlockSpec((1, nbp, KV_WIDTH), lambda i: (i, 0, 0)),
            pl.BlockSpec((1, nbp, KV_WIDTH), lambda i: (i, 0, 0)),
            pl.BlockSpec(bias_cmp.shape, lambda i: (0, 0, 0)),
        ],
        out_specs=[pl.BlockSpec((1, N_KV_NSA, GROUP * t, HEAD_DIM), lambda i: (i, 0, 0, 0)),
                   pl.BlockSpec((1, N_KV_NSA, t, width), lambda i: (i, 0, 0, 0))],
        compiler_params=_cparams(("parallel",)),
        name="nsa_sample_cmp",
    )(q3, kcc, vcc, bias_cmp)


KEY_CHUNK = 1024


def _nsa_sample_attn_kernel(pt_ref, q_ref, gate_ref, ocmp_ref, sel_ref, ksn_ref, vsn_ref, kwn_ref, vwn_ref,
                            cwk_ref, cwv_ref, pk_ref, pv_ref, bs_ref, bn_ref, bw_ref, ex_ref,
                            o_ref, bk_ref, bv_ref, sk_ref, sv_ref, selx_ref, *, n_pages):
    slot = _paged_prefetch(pt_ref, (pk_ref, pv_ref), (bk_ref, bv_ref), (sk_ref, sv_ref), n_pages)
    t = q_ref.shape[1]
    rows = GROUP * t
    past = n_pages * PAGE_SIZE
    nbp = past // SLC_BLOCK
    wb = cwk_ref.shape[1]
    q = q_ref[0]
    gates = jax.nn.sigmoid(gate_ref[0])
    tok = _row_token(t)
    jn = lax.broadcasted_iota(jnp.int32, (1, NEW_PAD), 1)
    new_mask = jn <= tok
    jw = lax.broadcasted_iota(jnp.int32, (1, wb), 1)
    wdist = wb + tok - jw
    win_mask = (wdist >= 0) & (wdist <= WINDOW)
    pad = jnp.zeros((NEW_PAD - t, HEAD_DIM), F32)

    for g in range(N_KV_NSA):
        cols = slice(g * HEAD_DIM, (g + 1) * HEAD_DIM)
        qg = _stack_heads(q, g).astype(BF16)
        sel4 = jnp.concatenate([sel_ref[0, g]] * GROUP, axis=0)
        selx_ref[...] = _dot(sel4[:, :nbp].astype(BF16), ex_ref[...])
        bias_new = bn_ref[g][:, :NEW_PAD]

        def new_rows(ref):
            return jnp.concatenate([ref[0][:, cols], pad], axis=0).astype(BF16)

        def slc_step(c, carry):
            off = pl.multiple_of(c * KEY_CHUNK, KEY_CHUNK)
            k = bk_ref[slot, pl.ds(off, KEY_CHUNK), cols].astype(BF16)
            v = bv_ref[slot, pl.ds(off, KEY_CHUNK), cols].astype(BF16)
            s = _dot_nt(qg, k) * ATTN_SCALE + bs_ref[g, :, pl.ds(off, KEY_CHUNK)]
            return _softmax_step(carry, s, selx_ref[:, pl.ds(off, KEY_CHUNK)] > 0.5, v)

        carry = lax.fori_loop(0, past // KEY_CHUNK, slc_step, _softmax_init(rows))
        s = _dot_nt(qg, new_rows(ksn_ref)) * ATTN_SCALE + bias_new
        carry = _softmax_step(carry, s, new_mask & (sel4[:, nbp:nbp + 1] > 0.5), new_rows(vsn_ref))
        o_slc = _softmax_done(carry)

        s = _dot_nt(qg, cwk_ref[0][:, cols].astype(BF16)) * ATTN_SCALE + bw_ref[g]
        carry = _softmax_step(_softmax_init(rows), s, win_mask, cwv_ref[0][:, cols].astype(BF16))
        s = _dot_nt(qg, new_rows(kwn_ref)) * ATTN_SCALE + bias_new
        carry = _softmax_step(carry, s, new_mask, new_rows(vwn_ref))
        o_win = _softmax_done(carry)

        heads = [g * GROUP + r for r in range(GROUP)]
        o = (_stack_cols(gates, heads) * ocmp_ref[0, g]
             + _stack_cols(gates, [N_HEADS_NSA + h for h in heads]) * o_slc
             + _stack_cols(gates, [2 * N_HEADS_NSA + h for h in heads]) * o_win)
        for r, h in enumerate(heads):
            o_ref[0, :, h * HEAD_DIM:(h + 1) * HEAD_DIM] = o[r * t:(r + 1) * t]


def _nsa_sample_attn(page_table, q3, gate3, o_cmp, sel, ksn, vsn, kwn, vwn, cwk, cwv, pool_k, pool_v,
                     bias_slc, bias_new, bias_win, expand):
    n, n_pages = page_table.shape
    t = q3.shape[1]
    past = n_pages * PAGE_SIZE
    wb = cwk.shape[1]
    width = sel.shape[3]
    assert past % KEY_CHUNK == 0 and t <= NEW_PAD
    per_seq = lambda shape: pl.BlockSpec((1,) + shape, lambda i, pt: (i,) + (0,) * len(shape))
    full = lambda a: pl.BlockSpec(a.shape, lambda i, pt: (0,) * a.ndim)
    anyspec = pl.BlockSpec(memory_space=pl.ANY)
    grid_spec = pltpu.PrefetchScalarGridSpec(
        num_scalar_prefetch=1,
        grid=(n,),
        in_specs=[per_seq((t, NSA_WIDTH)), per_seq((t, LANES)), per_seq((N_KV_NSA, GROUP * t, HEAD_DIM)),
                  per_seq((N_KV_NSA, t, width)),
                  per_seq((t, KV_WIDTH)), per_seq((t, KV_WIDTH)), per_seq((t, KV_WIDTH)), per_seq((t, KV_WIDTH)),
                  per_seq((wb, KV_WIDTH)), per_seq((wb, KV_WIDTH)), anyspec, anyspec,
                  full(bias_slc), full(bias_new), full(bias_win), full(expand)],
        out_specs=per_seq((t, NSA_WIDTH)),
        scratch_shapes=[pltpu.VMEM((2, past, KV_WIDTH), F32), pltpu.VMEM((2, past, KV_WIDTH), F32),
                        pltpu.SemaphoreType.DMA((2,)), pltpu.SemaphoreType.DMA((2,)),
                        pltpu.VMEM((GROUP * t, past), F32)],
    )
    return pl.pallas_call(
        functools.partial(_nsa_sample_attn_kernel, n_pages=n_pages),
        out_shape=jax.ShapeDtypeStruct((n, t, NSA_WIDTH), F32),
        grid_spec=grid_spec,
        compiler_params=_cparams(("arbitrary",)),
        name="nsa_sample_attn",
    )(page_table, q3, gate3, o_cmp, sel, ksn, vsn, kwn, vwn, cwk, cwv, pool_k, pool_v,
      bias_slc, bias_new, bias_win, expand)


def _segsum(x, ones_ref):
    hi = x.astype(BF16)
    lo = (x - hi.astype(F32)).astype(BF16)
    return _dot(hi, ones_ref[...]) + _dot(lo, ones_ref[...])


def _softplus(x):
    return jnp.maximum(x, 0.0) + jnp.log(1.0 + jnp.exp(-jnp.abs(x)))


def _rwkv_pre_kernel(p_ref, prev_ref, mu_ref, w0_ref, w2_ref, a0_ref, a2_ref, g2_ref, kk_ref, ka_ref, ones_ref,
                     r_ref, k_ref, v_ref, kkn_ref, kka_ref, d_ref, g_ref, last_ref, buf_ref):
    ti = pl.program_id(1)
    tt = p_ref.shape[1]
    p = p_ref[0]
    first = jnp.where(ti == 0, prev_ref[0], last_ref[...])
    last_ref[...] = p[tt - 1:tt]
    buf_ref[SUBLANES - 1:SUBLANES, :] = first
    buf_ref[SUBLANES:SUBLANES + tt, :] = p
    prev = buf_ref[SUBLANES - 1:SUBLANES - 1 + tt, :]
    xs = p + (prev - p) * mu_ref[...]
    W = RWKV_WIDTH
    r, k, v = xs[:, 0:W], xs[:, W:2 * W], xs[:, 2 * W:3 * W]
    c = 3 * W
    xw = xs[:, c:c + DECAY_LORA]
    xa = xs[:, c + DECAY_LORA:c + DECAY_LORA + AAA_LORA]
    xg = xs[:, c + DECAY_LORA + AAA_LORA:]
    w = -_softplus(-(w0_ref[...] + _dot(jnp.tanh(xw).astype(BF16), w2_ref[...]))) - 0.5
    a = jax.nn.sigmoid(a0_ref[...] + _dot(xa.astype(BF16), a2_ref[...]))
    kk = k * kk_ref[...]
    kk = kk / jnp.maximum(jnp.sqrt(_segsum(kk * kk, ones_ref)), 1e-12)
    r_ref[0] = r
    k_ref[0] = k * (1.0 + (a - 1.0) * ka_ref[...])
    v_ref[0] = v
    kkn_ref[0] = kk
    kka_ref[0] = kk * a
    d_ref[0] = jnp.exp(-jnp.exp(w))
    g_ref[0] = _dot(jax.nn.sigmoid(xg).astype(BF16), g2_ref[...])


def _rwkv_pre(p3, prev, mu, w0, w2, a0, a2, g2, k_k, k_a, ones_bd):
    n, t, c = p3.shape
    tt = min(t, 256)
    assert t % tt == 0
    row = lambda a: a.reshape(1, -1)
    full = lambda a: pl.BlockSpec(a.shape, lambda b, i: (0,) * a.ndim)
    params = [row(mu), row(w0), w2, row(a0), a2, g2, row(k_k), row(k_a), ones_bd]
    out_spec = pl.BlockSpec((1, tt, RWKV_WIDTH), lambda b, i: (b, i, 0))
    return pl.pallas_call(
        _rwkv_pre_kernel,
        out_shape=[jax.ShapeDtypeStruct((n, t, RWKV_WIDTH), F32)] * 7,
        grid=(n, t // tt),
        in_specs=[pl.BlockSpec((1, tt, c), lambda b, i: (b, i, 0)),
                  pl.BlockSpec((1, 1, c), lambda b, i: (b, 0, 0))] + [full(a) for a in params],
        out_specs=[out_spec] * 7,
        scratch_shapes=[pltpu.VMEM((1, c), F32), pltpu.VMEM((SUBLANES + tt, c), F32)],
        compiler_params=_cparams(("parallel", "arbitrary")),
        name="rwkv_pre",
    )(p3, prev.reshape(n, 1, c), *params)


def _rwkv_scan_kernel(r_ref, k_ref, v_ref, kk_ref, kka_ref, d_ref, s0_ref, y_ref, sout_ref, s_ref):
    ti = pl.program_id(1)
    tc = r_ref.shape[1]

    @pl.when(ti == 0)
    def _():
        s_ref[...] = s0_ref[0]

    diag = (lax.broadcasted_iota(jnp.int32, (HEAD_DIM, HEAD_DIM), 0)
            == lax.broadcasted_iota(jnp.int32, (HEAD_DIM, HEAD_DIM), 1))

    def step(t, _):
        r_t, k_t, v_t, kk_t, kka_t, d_t = (ref[0, pl.ds(t, 1), :]
                                           for ref in (r_ref, k_ref, v_ref, kk_ref, kka_ref, d_ref))
        y_rows = []
        for h in range(N_HEADS_RWKV):
            cols = slice(h * HEAD_DIM, (h + 1) * HEAD_DIM)
            s = s_ref[h]
            s_kk = -jnp.sum(s * kk_t[:, cols], axis=1, keepdims=True)
            v_col = jnp.sum(jnp.where(diag, v_t[:, cols], 0.0), axis=1, keepdims=True)
            s = s * d_t[:, cols] + s_kk * kka_t[:, cols] + v_col * k_t[:, cols]
            s_ref[h] = s
            y_col = jnp.sum(s * r_t[:, cols], axis=1, keepdims=True)
            y_rows.append(jnp.sum(jnp.where(diag, y_col, 0.0), axis=0, keepdims=True))
        y_ref[0, pl.ds(t, 1), :] = jnp.concatenate(y_rows, axis=1)
        return 0

    lax.fori_loop(0, tc, step, 0)

    @pl.when(ti == pl.num_programs(1) - 1)
    def _():
        sout_ref[0] = s_ref[...]


def _rwkv_scan(r, k, v, kk, kka, d, s0):
    n, t, w = r.shape
    tc = min(t, TC)
    assert t % tc == 0
    seq_spec = pl.BlockSpec((1, tc, w), lambda b, i: (b, i, 0))
    st_spec = pl.BlockSpec((1, N_HEADS_RWKV, HEAD_DIM, HEAD_DIM), lambda b, i: (b, 0, 0, 0))
    return pl.pallas_call(
        _rwkv_scan_kernel,
        out_shape=[jax.ShapeDtypeStruct((n, t, w), F32),
                   jax.ShapeDtypeStruct((n, N_HEADS_RWKV, HEAD_DIM, HEAD_DIM), F32)],
        grid=(n, t // tc),
        in_specs=[seq_spec] * 6 + [st_spec],
        out_specs=[seq_spec, st_spec],
        scratch_shapes=[pltpu.VMEM((N_HEADS_RWKV, HEAD_DIM, HEAD_DIM), F32)],
        compiler_params=_cparams(("parallel", "arbitrary")),
        name="rwkv_scan",
    )(r, k, v, kk, kka, d, s0)


def _mix_out_kernel(x_ref, nsa_ref, y_ref, r_ref, k_ref, v_ref, g_ref, rk_ref, lnw_ref, lnb_ref, ones_ref,
                    wa_ref, wb_ref, o_ref):
    y = y_ref[...]
    mean = _segsum(y, ones_ref) * (1.0 / HEAD_DIM)
    yc = y - mean
    var = _segsum(yc * yc, ones_ref) * (1.0 / HEAD_DIM)
    yn = yc * lax.rsqrt(var + GN_EPS) * lnw_ref[...] + lnb_ref[...]
    bonus = _segsum(r_ref[...] * k_ref[...] * rk_ref[...], ones_ref) * v_ref[...]
    rw = ((yn + bonus) * g_ref[...]).astype(BF16)
    o_ref[...] = x_ref[...] + _dot(nsa_ref[...].astype(BF16), wa_ref[...]) + _dot(rw, wb_ref[...])


def _mix_out(x, o_nsa, y, r, k, v, g, r_k, ln_w, ln_b, ones_bd, wa, wb):
    m, d = x.shape
    tm = min(m, 512)
    assert m % tm == 0
    row = lambda a: a.reshape(1, -1)
    tile = lambda w: pl.BlockSpec((tm, w), lambda i: (i, 0))
    full = lambda a: pl.BlockSpec(a.shape, lambda i: (0,) * a.ndim)
    params = [row(r_k), row(ln_w), row(ln_b), ones_bd, wa, wb]
    return pl.pallas_call(
        _mix_out_kernel,
        out_shape=jax.ShapeDtypeStruct((m, d), F32),
        grid=(m // tm,),
        in_specs=[tile(d), tile(NSA_WIDTH)] + [tile(RWKV_WIDTH)] * 5 + [full(a) for a in params],
        out_specs=tile(d),
        compiler_params=_cparams(("parallel",)),
        name="mix_out",
    )(x, o_nsa, y, r, k, v, g, *params)


def _block_diag2(w):
    z = jnp.zeros_like(w)
    return jnp.concatenate([jnp.concatenate([w, z], axis=-1), jnp.concatenate([z, w], axis=-1)], axis=-2)


def _compress_weights(pe, w1, w2):
    return (jnp.concatenate([pe] * N_KV_NSA, axis=1), _block_diag2(w1).astype(BF16), _block_diag2(w2).astype(BF16))


def kernel(x_prompt, x_sample, cache_cmp_k, cache_cmp_v, cache_slc_k, cache_slc_v, cache_win_k, cache_win_v,
           state_wkv, state_shift, page_table, rel_bias_table, ffn1_norm, ffn1_wg, ffn1_wu, ffn1_wd, mix_norm,
           w_in, cmp_pe_k, cmp_w1_k, cmp_w2_k, cmp_pe_v, cmp_w1_v, cmp_w2_v, shift_mu, decay_w0, decay_w2,
           aaa_a0, aaa_a2, gate_g2, k_k, k_a, r_k, ln_x_w, ln_x_b, w_out, ffn2_norm, ffn2_wg, ffn2_wu, ffn2_wd,
           final_norm):
    bp, seq, d = x_prompt.shape
    bs, ts, _ = x_sample.shape
    depth = w_in.shape[0]
    n_pages = page_table.shape[1]
    past = n_pages * PAGE_SIZE
    n_phys = cache_cmp_k.shape[1]
    wbuf = cache_win_k.shape[2]
    assert seq % QT == 0 and seq >= WINDOW and ts < CMP_BLOCK and wbuf == WINDOW
    nq = seq // QT
    nb = seq // SLC_BLOCK
    nbp = past // SLC_BLOCK
    nsa_main = NSA_WIDTH + 6 * KV_WIDTH
    nsa_cols = nsa_main + N_GATES

    bias_tiles = _bias_tiles(rel_bias_table)
    bias_cmp_p = _bias_rows(rel_bias_table, q0=0, rows=seq, n_keys=nb, key_first=CMP_BLOCK - 1, key_stride=CMP_BLOCK)
    bias_cmp_p = bias_cmp_p.reshape(N_KV_NSA, GROUP, nq, QT, nb).transpose(0, 2, 1, 3, 4).reshape(
        N_KV_NSA, nq, GROUP * QT, nb)
    stacked = lambda b: b.reshape(N_KV_NSA, GROUP * ts, b.shape[-1])
    bias_cmp_s = stacked(_bias_rows(rel_bias_table, q0=past, rows=ts, n_keys=nbp,
                                    key_first=CMP_BLOCK - 1, key_stride=CMP_BLOCK))
    bias_slc_s = stacked(_bias_rows(rel_bias_table, q0=past, rows=ts, n_keys=past, key_first=0, key_stride=1))
    bias_new_s = stacked(_bias_rows(rel_bias_table, q0=past, rows=ts, n_keys=LANES, key_first=past, key_stride=1))
    bias_win_s = stacked(_bias_rows(rel_bias_table, q0=past, rows=ts, n_keys=wbuf,
                                    key_first=past - wbuf, key_stride=1))
    expand_s = (jnp.arange(past, dtype=jnp.int32)[None, :] // SLC_BLOCK
                == jnp.arange(nbp, dtype=jnp.int32)[:, None]).astype(BF16)
    head_of = jnp.arange(RWKV_WIDTH, dtype=jnp.int32) // HEAD_DIM
    ones_bd = (head_of[:, None] == head_of[None, :]).astype(BF16)

    xp = x_prompt.reshape(bp * seq, d)
    xs = x_sample.reshape(bs * ts, d)
    states = []
    for l in range(depth):
        bf = lambda a: a[l].astype(BF16)
        w_in_l = w_in[l]
        w_pad = jnp.concatenate(
            [w_in_l[:, :nsa_main], w_in_l[:, nsa_main:nsa_cols], jnp.zeros((d, LANES - N_GATES), F32),
             w_in_l[:, nsa_cols:]], axis=1).astype(BF16)
        cw_k = _compress_weights(cmp_pe_k[l], cmp_w1_k[l], cmp_w2_k[l])
        cw_v = _compress_weights(cmp_pe_v[l], cmp_w1_v[l], cmp_w2_v[l])
        wa, wb = w_out[l][:NSA_WIDTH].astype(BF16), w_out[l][NSA_WIDTH:].astype(BF16)
        rw_pre = (shift_mu[l], decay_w0[l], bf(decay_w2), aaa_a0[l], bf(aaa_a2), bf(gate_g2), k_k[l], k_a[l], ones_bd)
        ffn1 = (ffn1_norm[l], bf(ffn1_wg), bf(ffn1_wu), bf(ffn1_wd), final_norm)
        ffn2 = (ffn2_norm[l], bf(ffn2_wg), bf(ffn2_wu), bf(ffn2_wd), final_norm)
        last = l == depth - 1

        xp = _ffn_half(xp, *ffn1, final=False)
        xs = _ffn_half(xs, *ffn1, final=False)

        q, kc, vc, ks, vs, kw, vw, gate, zr = _proj(xp, mix_norm[l], w_pad)
        kcc = _compress(kc, *cw_k)
        vcc = _compress(vc, *cw_v)
        o_nsa = _nsa_prompt(q, gate, ks, vs, kw, vw, kcc, vcc, bias_tiles, bias_cmp_p, bp, seq)
        zr3 = zr.reshape(bp, seq, RWKV_COLS)
        r, k2, v, kk, kka, dec, gg = _rwkv_pre(zr3, jnp.zeros((bp, RWKV_COLS), F32), *rw_pre)
        y, wkv_p = _rwkv_scan(r, k2, v, kk, kka, dec,
                              jnp.zeros((bp, N_HEADS_RWKV, HEAD_DIM, HEAD_DIM), F32))
        flat = lambda a: a.reshape(-1, RWKV_WIDTH)
        xp = _mix_out(xp, o_nsa, flat(y), flat(r), flat(k2), flat(v), flat(gg), r_k[l], ln_x_w[l], ln_x_b[l],
                      ones_bd, wa, wb)
        kv5 = lambda a, n_, t_: a.reshape(n_, t_, N_KV_NSA, HEAD_DIM)
        p_state = (kv5(kc, bp, seq), kv5(vc, bp, seq), kv5(ks, bp, seq), kv5(vs, bp, seq),
                   kv5(kw, bp, seq)[:, seq - WINDOW:], kv5(vw, bp, seq)[:, seq - WINDOW:],
                   wkv_p, zr3[:, seq - 1])

        q, kc, vc, ks, vs, kw, vw, gate, zr = _proj(xs, mix_norm[l], w_pad)
        pool = lambda c: c[l].reshape(n_phys, PAGE_SIZE, KV_WIDTH)
        kcc, vcc = _compress_paged(page_table, pool(cache_cmp_k), pool(cache_cmp_v), cw_k, cw_v)
        seq3 = lambda a: a.reshape(bs, ts, a.shape[-1])
        o_cmp, sel = _nsa_sample_cmp(seq3(q), kcc, vcc, bias_cmp_s, past)
        cwk = cache_win_k[l].reshape(bs, wbuf, KV_WIDTH)
        cwv = cache_win_v[l].reshape(bs, wbuf, KV_WIDTH)
        o_nsa = _nsa_sample_attn(page_table, seq3(q), seq3(gate), o_cmp, sel, seq3(ks), seq3(vs), seq3(kw),
                                 seq3(vw), cwk, cwv, pool(cache_slc_k), pool(cache_slc_v),
                                 bias_slc_s, bias_new_s, bias_win_s, expand_s)
        zr3 = seq3(zr)
        r, k2, v, kk, kka, dec, gg = _rwkv_pre(zr3, state_shift[l], *rw_pre)
        y, wkv_s = _rwkv_scan(r, k2, v, kk, kka, dec, state_wkv[l])
        xs = _mix_out(xs, o_nsa.reshape(bs * ts, NSA_WIDTH), flat(y), flat(r), flat(k2), flat(v), flat(gg),
                      r_k[l], ln_x_w[l], ln_x_b[l], ones_bd, wa, wb)
        win = lambda cache, new: jnp.concatenate([cache[l], kv5(new, bs, ts)], axis=1)[:, ts:]
        s_state = (kv5(kc, bs, ts), kv5(vc, bs, ts), kv5(ks, bs, ts), kv5(vs, bs, ts),
                   win(cache_win_k, kw), win(cache_win_v, vw), wkv_s, zr3[:, ts - 1])

        xp = _ffn_half(xp, *ffn2, final=last)
        xs = _ffn_half(xs, *ffn2, final=last)
        states.append(p_state + s_state)

    stacked_states = [jnp.stack(z) for z in zip(*states)]
    return (xp.reshape(bp, seq, d), xs.reshape(bs, ts, d), *stacked_states)
```

```python
import functools
import math

import jax
import jax.numpy as jnp
import numpy as np
from jax import lax
from jax.experimental import pallas as pl
from jax.experimental.pallas import tpu as pltpu

F32 = jnp.float32
BF16 = jnp.bfloat16

HEAD_DIM = 64
N_HEADS_NSA = 8
N_KV_NSA = 2
GROUP = N_HEADS_NSA // N_KV_NSA
N_HEADS_RWKV = 8
NSA_WIDTH = N_HEADS_NSA * HEAD_DIM
RWKV_WIDTH = N_HEADS_RWKV * HEAD_DIM
KV_WIDTH = N_KV_NSA * HEAD_DIM
CMP_BLOCK = 64
SLC_BLOCK = 64
TOP_K_BLOCKS = 16
WINDOW = 512
PAGE_SIZE = 128
N_BUCKETS = 32
MAX_DISTANCE = 128
DECAY_LORA = 64
AAA_LORA = 64
GATE_LORA = 128
NORM_EPS = 1e-6
GN_EPS = 64e-5
ATTN_SCALE = HEAD_DIM ** -0.5
FORCE_SCORE = 1e4
NEG_INF = -1e30
RWKV_COLS = 3 * RWKV_WIDTH + DECAY_LORA + AAA_LORA + GATE_LORA
N_GATES = 3 * N_HEADS_NSA

LANES = 128
SUBLANES = 8
VMEM_LIMIT = 56 * 1024 * 1024

QT = 128
TC = 64


def _bucket_thresholds():
    max_exact = N_BUCKETS // 2
    thr = list(range(max_exact + 1))
    for b in range(max_exact + 1, N_BUCKETS):
        x = max_exact * (MAX_DISTANCE / max_exact) ** ((b - max_exact) / (N_BUCKETS - max_exact))
        thr.append(int(math.ceil(x - 1e-9)))
    return thr


BUCKET_THR = _bucket_thresholds()


def _cparams(sem):
    return pltpu.CompilerParams(dimension_semantics=sem, vmem_limit_bytes=VMEM_LIMIT)


def _dot(a, b):
    return jnp.dot(a, b, preferred_element_type=F32)


def _dot_nt(a, b):
    return lax.dot_general(a, b, (((1,), (1,)), ((), ())), preferred_element_type=F32)


def _rms(x, g):
    return x * lax.rsqrt(jnp.mean(x * x, axis=-1, keepdims=True) + NORM_EPS) * g


def _ffn_kernel(x_ref, g_ref, wg_ref, wu_ref, wd_ref, gf_ref, o_ref, h_ref, acc_ref, *, final):
    j = pl.program_id(1)

    @pl.when(j == 0)
    def _():
        h_ref[...] = _rms(x_ref[...], g_ref[...]).astype(BF16)
        acc_ref[...] = jnp.zeros_like(acc_ref)

    h = h_ref[...]
    a = _dot(h, wg_ref[...])
    b = _dot(h, wu_ref[...])
    act = (a * jax.nn.sigmoid(a) * b).astype(BF16)
    acc_ref[...] += _dot(act, wd_ref[...])

    @pl.when(j == pl.num_programs(1) - 1)
    def _():
        y = x_ref[...] + 0.5 * acc_ref[...]
        if final:
            y = _rms(y, gf_ref[...])
        o_ref[...] = y


def _ffn_half(x, g, wg, wu, wd, gf, *, final):
    m, d = x.shape
    ff = wg.shape[1]
    tm = min(m, 1024)
    tf = 256 if ff % 256 == 0 else ff
    assert m % tm == 0 and ff % tf == 0
    return pl.pallas_call(
        functools.partial(_ffn_kernel, final=final),
        out_shape=jax.ShapeDtypeStruct((m, d), F32),
        grid=(m // tm, ff // tf),
        in_specs=[
            pl.BlockSpec((tm, d), lambda i, j: (i, 0)),
            pl.BlockSpec((1, d), lambda i, j: (0, 0)),
            pl.BlockSpec((d, tf), lambda i, j: (0, j)),
            pl.BlockSpec((d, tf), lambda i, j: (0, j)),
            pl.BlockSpec((tf, d), lambda i, j: (j, 0)),
            pl.BlockSpec((1, d), lambda i, j: (0, 0)),
        ],
        out_specs=pl.BlockSpec((tm, d), lambda i, j: (i, 0)),
        scratch_shapes=[pltpu.VMEM((tm, d), BF16), pltpu.VMEM((tm, d), F32)],
        compiler_params=_cparams(("parallel", "arbitrary")),
        name="ffn_half",
    )(x, g.reshape(1, d), wg, wu, wd, gf.reshape(1, d))


def _proj_kernel(x_ref, g_ref, w_ref, q_ref, kc_ref, vc_ref, ks_ref, vs_ref, kw_ref, vw_ref, gate_ref, zr_ref):
    h = _rms(x_ref[...], g_ref[...]).astype(BF16)
    z = _dot(h, w_ref[...])
    c = 0
    for ref in (q_ref, kc_ref, vc_ref, ks_ref, vs_ref, kw_ref, vw_ref, gate_ref, zr_ref):
        w = ref.shape[1]
        ref[...] = z[:, c:c + w]
        c += w


def _proj(x, g, w_pad):
    m, d = x.shape
    tm = min(m, 512)
    widths = (NSA_WIDTH,) + (KV_WIDTH,) * 6 + (LANES, RWKV_COLS)
    assert sum(widths) == w_pad.shape[1] and m % tm == 0
    return pl.pallas_call(
        _proj_kernel,
        out_shape=[jax.ShapeDtypeStruct((m, w), F32) for w in widths],
        grid=(m // tm,),
        in_specs=[
            pl.BlockSpec((tm, d), lambda i: (i, 0)),
            pl.BlockSpec((1, d), lambda i: (0, 0)),
            pl.BlockSpec(w_pad.shape, lambda i: (0, 0)),
        ],
        out_specs=[pl.BlockSpec((tm, w), lambda i: (i, 0)) for w in widths],
        compiler_params=_cparams(("parallel",)),
        name="in_proj",
    )(x, g.reshape(1, d), w_pad)


def _out_kernel(x_ref, a_ref, b_ref, wa_ref, wb_ref, o_ref):
    o_ref[...] = (x_ref[...] + _dot(a_ref[...].astype(BF16), wa_ref[...])
                  + _dot(b_ref[...].astype(BF16), wb_ref[...]))


def _out_proj(x, a, b, wa, wb):
    m, d = x.shape
    tm = min(m, 512)
    return pl.pallas_call(
        _out_kernel,
        out_shape=jax.ShapeDtypeStruct((m, d), F32),
        grid=(m // tm,),
        in_specs=[
            pl.BlockSpec((tm, d), lambda i: (i, 0)),
            pl.BlockSpec((tm, a.shape[1]), lambda i: (i, 0)),
            pl.BlockSpec((tm, b.shape[1]), lambda i: (i, 0)),
            pl.BlockSpec(wa.shape, lambda i: (0, 0)),
            pl.BlockSpec(wb.shape, lambda i: (0, 0)),
        ],
        out_specs=pl.BlockSpec((tm, d), lambda i: (i, 0)),
        compiler_params=_cparams(("parallel",)),
        name="out_proj",
    )(x, a, b, wa, wb)


def _bias_lookup(dist, table_ref, heads):
    vals = [jnp.full(dist.shape, table_ref[0, h], F32) for h in heads]
    for b in range(1, N_BUCKETS):
        ge = dist >= BUCKET_THR[b]
        vals = [jnp.where(ge, table_ref[b, h], v) for h, v in zip(heads, vals)]
    return vals


def _bias_tiles_kernel(table_ref, o_ref):
    i = lax.broadcasted_iota(jnp.int32, (QT, QT), 0)
    j = lax.broadcasted_iota(jnp.int32, (QT, QT), 1)
    for off in range(3):
        vals = _bias_lookup(off * QT + i - j, table_ref, range(N_HEADS_NSA))
        for h, v in enumerate(vals):
            g, r = divmod(h, GROUP)
            o_ref[g, off, r * QT:(r + 1) * QT, :] = v


def _bias_tiles(table):
    return pl.pallas_call(
        _bias_tiles_kernel,
        out_shape=jax.ShapeDtypeStruct((N_KV_NSA, 3, GROUP * QT, QT), F32),
        in_specs=[pl.BlockSpec(memory_space=pltpu.SMEM)],
        name="bias_tiles",
    )(table)


def _bias_rows_kernel(table_ref, o_ref, *, q0, key_first, key_stride):
    _, _, rc, L = o_ref.shape
    i = lax.broadcasted_iota(jnp.int32, (rc, L), 0) + pl.program_id(0) * rc
    j = lax.broadcasted_iota(jnp.int32, (rc, L), 1)
    dist = (q0 + i) - (key_first + key_stride * j)
    vals = _bias_lookup(dist, table_ref, range(N_HEADS_NSA))
    for h, v in enumerate(vals):
        g, r = divmod(h, GROUP)
        o_ref[g, r] = v


def _bias_rows(table, *, q0, rows, n_keys, key_first, key_stride):
    rc = min(rows, 256)
    assert rows % rc == 0
    return pl.pallas_call(
        functools.partial(_bias_rows_kernel, q0=q0, key_first=key_first, key_stride=key_stride),
        out_shape=jax.ShapeDtypeStruct((N_KV_NSA, GROUP, rows, n_keys), F32),
        grid=(rows // rc,),
        in_specs=[pl.BlockSpec(memory_space=pltpu.SMEM)],
        out_specs=pl.BlockSpec((N_KV_NSA, GROUP, rc, n_keys), lambda i: (0, 0, i, 0)),
        compiler_params=_cparams(("parallel",)),
        name="bias_rows",
    )(table)


def _gelu_tanh(x):
    return 0.5 * x * (1.0 + jnp.tanh(math.sqrt(2.0 / math.pi) * (x + 0.044715 * (x * x * x))))


def _compress_blocks(load_rows, nblk, pe_ref, w1_ref, w2_ref):
    acc = jnp.zeros((nblk, KV_WIDTH), F32)
    for c in range(CMP_BLOCK):
        rows = load_rows(c) + pe_ref[c:c + 1, :]
        acc = acc + _dot(rows.astype(BF16), w1_ref[c])
    return _dot(_gelu_tanh(acc).astype(BF16), w2_ref[...])


def _compress_kernel(x_ref, pe_ref, w1_ref, w2_ref, o_ref):
    nblk = o_ref.shape[0]
    o_ref[...] = _compress_blocks(lambda c: x_ref[pl.ds(c, nblk, stride=CMP_BLOCK), :],
                                  nblk, pe_ref, w1_ref, w2_ref)


def _compress(x, pe2, w1bd, w2bd):
    rows = x.shape[0]
    nb = rows // CMP_BLOCK
    nblk = min(nb, 128)
    assert nb % nblk == 0
    return pl.pallas_call(
        _compress_kernel,
        out_shape=jax.ShapeDtypeStruct((nb, KV_WIDTH), F32),
        grid=(nb // nblk,),
        in_specs=[
            pl.BlockSpec((nblk * CMP_BLOCK, KV_WIDTH), lambda i: (i, 0)),
            pl.BlockSpec(pe2.shape, lambda i: (0, 0)),
            pl.BlockSpec(w1bd.shape, lambda i: (0, 0, 0)),
            pl.BlockSpec(w2bd.shape, lambda i: (0, 0)),
        ],
        out_specs=pl.BlockSpec((nblk, KV_WIDTH), lambda i: (i, 0)),
        compiler_params=_cparams(("parallel",)),
        name="compress",
    )(x, pe2, w1bd, w2bd)


def _page_copies(pt_ref, n, pool_ref, buf_ref, sem_ref, slot, n_pages):
    return [pltpu.make_async_copy(pool_ref.at[pt_ref[n, p]],
                                  buf_ref.at[slot, pl.ds(p * PAGE_SIZE, PAGE_SIZE)],
                                  sem_ref.at[slot]) for p in range(n_pages)]


def _paged_prefetch(pt_ref, pools, bufs, sems, n_pages):
    n = pl.program_id(0)
    slot = n % 2

    def start(seq, slot_):
        for pool, buf, sem in zip(pools, bufs, sems):
            for cp in _page_copies(pt_ref, seq, pool, buf, sem, slot_, n_pages):
                cp.start()

    @pl.when(n == 0)
    def _():
        start(0, 0)

    @pl.when(n + 1 < pl.num_programs(0))
    def _():
        start(n + 1, 1 - slot)

    for pool, buf, sem in zip(pools, bufs, sems):
        for cp in _page_copies(pt_ref, n, pool, buf, sem, slot, n_pages):
            cp.wait()
    return slot


def _compress_paged_kernel(pt_ref, pk_ref, pv_ref, pek_ref, w1k_ref, w2k_ref, pev_ref, w1v_ref, w2v_ref,
                           ok_ref, ov_ref, bk_ref, bv_ref, sk_ref, sv_ref, *, n_pages):
    slot = _paged_prefetch(pt_ref, (pk_ref, pv_ref), (bk_ref, bv_ref), (sk_ref, sv_ref), n_pages)
    nblk = ok_ref.shape[1]
    ok_ref[0] = _compress_blocks(lambda c: bk_ref[slot, pl.ds(c, nblk, stride=CMP_BLOCK), :],
                                 nblk, pek_ref, w1k_ref, w2k_ref)
    ov_ref[0] = _compress_blocks(lambda c: bv_ref[slot, pl.ds(c, nblk, stride=CMP_BLOCK), :],
                                 nblk, pev_ref, w1v_ref, w2v_ref)


def _compress_paged(page_table, pool_k, pool_v, cw_k, cw_v):
    n, n_pages = page_table.shape
    past = n_pages * PAGE_SIZE
    nblk = past // CMP_BLOCK
    wspec = lambda a: pl.BlockSpec(a.shape, lambda i, pt: (0,) * a.ndim)
    grid_spec = pltpu.PrefetchScalarGridSpec(
        num_scalar_prefetch=1,
        grid=(n,),
        in_specs=[pl.BlockSpec(memory_space=pl.ANY), pl.BlockSpec(memory_space=pl.ANY)]
                 + [wspec(a) for a in cw_k + cw_v],
        out_specs=[pl.BlockSpec((1, nblk, KV_WIDTH), lambda i, pt: (i, 0, 0))] * 2,
        scratch_shapes=[pltpu.VMEM((2, past, KV_WIDTH), F32), pltpu.VMEM((2, past, KV_WIDTH), F32),
                        pltpu.SemaphoreType.DMA((2,)), pltpu.SemaphoreType.DMA((2,))],
    )
    return pl.pallas_call(
        functools.partial(_compress_paged_kernel, n_pages=n_pages),
        out_shape=[jax.ShapeDtypeStruct((n, nblk, KV_WIDTH), F32)] * 2,
        grid_spec=grid_spec,
        compiler_params=_cparams(("arbitrary",)),
        name="compress_paged",
    )(page_table, pool_k, pool_v, *cw_k, *cw_v)


def _stack_heads(x, g):
    return jnp.concatenate(
        [x[:, (g * GROUP + r) * HEAD_DIM:(g * GROUP + r + 1) * HEAD_DIM] for r in range(GROUP)], axis=0)


def _stack_cols(x, cols):
    return jnp.concatenate([x[:, c:c + 1] for c in cols], axis=0)


def _softmax_step(carry, s, mask, v):
    m, l, acc = carry
    s = jnp.where(mask, s, NEG_INF)
    m_new = jnp.maximum(m, jnp.max(s, axis=1, keepdims=True))
    p = jnp.where(mask, jnp.exp(s - m_new), 0.0)
    alpha = jnp.exp(m - m_new)
    l = alpha * l + jnp.sum(p, axis=1, keepdims=True)
    acc = alpha * acc + _dot(p.astype(BF16), v)
    return m_new, l, acc


def _softmax_init(rows):
    return (jnp.full((rows, 1), NEG_INF, F32), jnp.zeros((rows, 1), F32), jnp.zeros((rows, HEAD_DIM), F32))


def _softmax_done(carry):
    _, l, acc = carry
    return acc / jnp.maximum(l, 1e-30)


def _topk_mask(imp, cur, n_blocks):
    blk = lax.broadcasted_iota(jnp.int32, imp.shape, 1)
    forced = (blk == 0) | (blk == cur) | (blk == cur - 1)
    score = jnp.where(blk <= cur, jnp.where(forced, FORCE_SCORE, imp), -FORCE_SCORE)
    rank = jnp.zeros(imp.shape, jnp.int32)
    for b in range(n_blocks):
        col = score[:, b:b + 1]
        ahead = (col > score) | ((col == score) & (blk > b))
        rank = rank + jnp.where(ahead, 1, 0)
    return (rank < TOP_K_BLOCKS) & (blk <= cur)


def _nsa_prompt_kernel(q_ref, gate_ref, ks_ref, vs_ref, kw_ref, vw_ref, kcc_ref, vcc_ref, bt_ref, bc_ref,
                       o_ref, selx_ref):
    qi = pl.program_id(1)
    seq = ks_ref.shape[1]
    nb = seq // SLC_BLOCK
    rows = GROUP * QT
    q = q_ref[...]
    gates = jax.nn.sigmoid(gate_ref[...])
    row = lax.broadcasted_iota(jnp.int32, (rows, 1), 0)
    tpos = qi * QT + (row & (QT - 1))
    tpos1 = qi * QT + lax.broadcasted_iota(jnp.int32, (QT, 1), 0)
    lane = lax.broadcasted_iota(jnp.int32, (1, QT), 1)
    expand = (lax.broadcasted_iota(jnp.int32, (nb, seq), 1) // SLC_BLOCK
              == lax.broadcasted_iota(jnp.int32, (nb, seq), 0)).astype(BF16)
    win_lo = jnp.maximum(qi - WINDOW // QT, 0)

    for g in range(N_KV_NSA):
        cols = slice(g * HEAD_DIM, (g + 1) * HEAD_DIM)
        qg = _stack_heads(q, g).astype(BF16)

        blk_end = (lax.broadcasted_iota(jnp.int32, (1, nb), 1) + 1) * CMP_BLOCK - 1
        s = _dot_nt(qg, kcc_ref[0][:, cols].astype(BF16)) * ATTN_SCALE + bc_ref[g, 0]
        mask = blk_end <= tpos
        s = jnp.where(mask, s, NEG_INF)
        e = jnp.where(mask, jnp.exp(s - jnp.max(s, axis=1, keepdims=True)), 0.0)
        p = e / jnp.maximum(jnp.sum(e, axis=1, keepdims=True), 1e-30)
        o_cmp = _dot(p.astype(BF16), vcc_ref[0][:, cols].astype(BF16))

        imp = p[0:QT]
        for r in range(1, GROUP):
            imp = imp + p[r * QT:(r + 1) * QT]
        sel = _topk_mask(imp, tpos1 // SLC_BLOCK, nb)
        selx_ref[...] = _dot(jnp.where(sel, 1.0, 0.0).astype(BF16), expand)

        def bias(kt):
            return bt_ref[g, jnp.minimum(qi - kt, 2)]

        def slc_step(kt, carry):
            off = pl.multiple_of(kt * QT, QT)
            k = ks_ref[0, pl.ds(off, QT), cols].astype(BF16)
            v = vs_ref[0, pl.ds(off, QT), cols].astype(BF16)
            s = _dot_nt(qg, k) * ATTN_SCALE + bias(kt)
            chosen = jnp.concatenate([selx_ref[:, pl.ds(off, QT)]] * GROUP, axis=0) > 0.5
            mask = chosen & (tpos - (off + lane) >= 0)
            return _softmax_step(carry, s, mask, v)

        o_slc = _softmax_done(lax.fori_loop(0, qi + 1, slc_step, _softmax_init(rows)))

        def win_step(kt, carry):
            off = pl.multiple_of(kt * QT, QT)
            k = kw_ref[0, pl.ds(off, QT), cols].astype(BF16)
            v = vw_ref[0, pl.ds(off, QT), cols].astype(BF16)
            s = _dot_nt(qg, k) * ATTN_SCALE + bias(kt)
            dist = tpos - (off + lane)
            mask = (dist >= 0) & (dist <= WINDOW)
            return _softmax_step(carry, s, mask, v)

        o_win = _softmax_done(lax.fori_loop(win_lo, qi + 1, win_step, _softmax_init(rows)))

        heads = [g * GROUP + r for r in range(GROUP)]
        o = (_stack_cols(gates, heads) * o_cmp
             + _stack_cols(gates, [N_HEADS_NSA + h for h in heads]) * o_slc
             + _stack_cols(gates, [2 * N_HEADS_NSA + h for h in heads]) * o_win)
        for r, h in enumerate(heads):
            o_ref[:, h * HEAD_DIM:(h + 1) * HEAD_DIM] = o[r * QT:(r + 1) * QT]


def _nsa_prompt(q, gate, ks, vs, kw, vw, kcc, vcc, bias_tiles, bias_cmp, n, seq):
    nq = seq // QT
    nb = seq // SLC_BLOCK
    seq_spec = pl.BlockSpec((1, seq, KV_WIDTH), lambda b, i: (b, 0, 0))
    cmp_spec = pl.BlockSpec((1, nb, KV_WIDTH), lambda b, i: (b, 0, 0))
    return pl.pallas_call(
        _nsa_prompt_kernel,
        out_shape=jax.ShapeDtypeStruct((n * seq, NSA_WIDTH), F32),
        grid=(n, nq),
        in_specs=[
            pl.BlockSpec((QT, NSA_WIDTH), lambda b, i: (b * nq + i, 0)),
            pl.BlockSpec((QT, LANES), lambda b, i: (b * nq + i, 0)),
            seq_spec, seq_spec, seq_spec, seq_spec, cmp_spec, cmp_spec,
            pl.BlockSpec(bias_tiles.shape, lambda b, i: (0, 0, 0, 0)),
            pl.BlockSpec((N_KV_NSA, 1, GROUP * QT, nb), lambda b, i: (0, i, 0, 0)),
        ],
        out_specs=pl.BlockSpec((QT, NSA_WIDTH), lambda b, i: (b * nq + i, 0)),
        scratch_shapes=[pltpu.VMEM((QT, seq), F32)],
        compiler_params=_cparams(("parallel", "arbitrary")),
        name="nsa_prompt",
    )(q, gate, ks.reshape(n, seq, KV_WIDTH), vs.reshape(n, seq, KV_WIDTH), kw.reshape(n, seq, KV_WIDTH),
      vw.reshape(n, seq, KV_WIDTH), kcc.reshape(n, nb, KV_WIDTH), vcc.reshape(n, nb, KV_WIDTH),
      bias_tiles, bias_cmp)


NEW_PAD = SUBLANES


def _row_token(t):
    return jnp.concatenate([lax.broadcasted_iota(jnp.int32, (t, 1), 0)] * GROUP, axis=0)


def _nsa_sample_cmp_kernel(q_ref, kcc_ref, vcc_ref, bc_ref, o_ref, sel_ref, *, past):
    t = q_ref.shape[1]
    nbp = kcc_ref.shape[1]
    width = sel_ref.shape[3]
    q = q_ref[0]
    pos = past + _row_token(t)
    blk_end = (lax.broadcasted_iota(jnp.int32, (1, nbp), 1) + 1) * CMP_BLOCK - 1
    cur = (past + lax.broadcasted_iota(jnp.int32, (t, 1), 0)) // SLC_BLOCK
    for g in range(N_KV_NSA):
        cols = slice(g * HEAD_DIM, (g + 1) * HEAD_DIM)
        qg = _stack_heads(q, g).astype(BF16)
        s = _dot_nt(qg, kcc_ref[0][:, cols].astype(BF16)) * ATTN_SCALE + bc_ref[g]
        mask = blk_end <= pos
        s = jnp.where(mask, s, NEG_INF)
        e = jnp.where(mask, jnp.exp(s - jnp.max(s, axis=1, keepdims=True)), 0.0)
        p = e / jnp.maximum(jnp.sum(e, axis=1, keepdims=True), 1e-30)
        o_ref[0, g] = _dot(p.astype(BF16), vcc_ref[0][:, cols].astype(BF16))
        imp = p[0:t]
        for r in range(1, GROUP):
            imp = imp + p[r * t:(r + 1) * t]
        imp = jnp.concatenate([imp, jnp.zeros((t, width - nbp), F32)], axis=1)
        sel = _topk_mask(imp, cur, nbp + 1)
        sel_ref[0, g] = jnp.where(sel, 1.0, 0.0)


def _nsa_sample_cmp(q3, kcc, vcc, bias_cmp, past):
    n, t, _ = q3.shape
    nbp = kcc.shape[1]
    width = -(-(nbp + 1) // LANES) * LANES
    return pl.pallas_call(
        functools.partial(_nsa_sample_cmp_kernel, past=past),
        out_shape=[jax.ShapeDtypeStruct((n, N_KV_NSA, GROUP * t, HEAD_DIM), F32),
                   jax.ShapeDtypeStruct((n, N_KV_NSA, t, width), F32)],
        grid=(n,),
        in_specs=[
            pl.BlockSpec((1, t, NSA_WIDTH), lambda i: (i, 0, 0)),
            pl.BlockSpec((1, nbp, KV_WIDTH), lambda i: (i, 0, 0)),
            pl.BlockSpec((1, nbp, KV_WIDTH), lambda i: (i, 0, 0)),
            pl.BlockSpec(bias_cmp.shape, lambda i: (0, 0, 0)),
        ],
        out_specs=[pl.BlockSpec((1, N_KV_NSA, GROUP * t, HEAD_DIM), lambda i: (i, 0, 0, 0)),
                   pl.BlockSpec((1, N_KV_NSA, t, width), lambda i: (i, 0, 0, 0))],
        compiler_params=_cparams(("parallel",)),
        name="nsa_sample_cmp",
    )(q3, kcc, vcc, bias_cmp)


KEY_CHUNK = 1024


def _nsa_sample_attn_kernel(pt_ref, q_ref, gate_ref, ocmp_ref, sel_ref, ksn_ref, vsn_ref, kwn_ref, vwn_ref,
                            cwk_ref, cwv_ref, pk_ref, pv_ref, bs_ref, bn_ref, bw_ref, ex_ref,
                            o_ref, bk_ref, bv_ref, sk_ref, sv_ref, selx_ref, *, n_pages):
    slot = _paged_prefetch(pt_ref, (pk_ref, pv_ref), (bk_ref, bv_ref), (sk_ref, sv_ref), n_pages)
    t = q_ref.shape[1]
    rows = GROUP * t
    past = n_pages * PAGE_SIZE
    nbp = past // SLC_BLOCK
    wb = cwk_ref.shape[1]
    q = q_ref[0]
    gates = jax.nn.sigmoid(gate_ref[0])
    tok = _row_token(t)
    jn = lax.broadcasted_iota(jnp.int32, (1, NEW_PAD), 1)
    new_mask = jn <= tok
    jw = lax.broadcasted_iota(jnp.int32, (1, wb), 1)
    wdist = wb + tok - jw
    win_mask = (wdist >= 0) & (wdist <= WINDOW)
    pad = jnp.zeros((NEW_PAD - t, HEAD_DIM), F32)

    for g in range(N_KV_NSA):
        cols = slice(g * HEAD_DIM, (g + 1) * HEAD_DIM)
        qg = _stack_heads(q, g).astype(BF16)
        sel4 = jnp.concatenate([sel_ref[0, g]] * GROUP, axis=0)
        selx_ref[...] = _dot(sel4[:, :nbp].astype(BF16), ex_ref[...])
        bias_new = bn_ref[g][:, :NEW_PAD]

        def new_rows(ref):
            return jnp.concatenate([ref[0][:, cols], pad], axis=0).astype(BF16)

        def slc_step(c, carry):
            off = pl.multiple_of(c * KEY_CHUNK, KEY_CHUNK)
            k = bk_ref[slot, pl.ds(off, KEY_CHUNK), cols].astype(BF16)
            v = bv_ref[slot, pl.ds(off, KEY_CHUNK), cols].astype(BF16)
            s = _dot_nt(qg, k) * ATTN_SCALE + bs_ref[g, :, pl.ds(off, KEY_CHUNK)]
            return _softmax_step(carry, s, selx_ref[:, pl.ds(off, KEY_CHUNK)] > 0.5, v)

        carry = lax.fori_loop(0, past // KEY_CHUNK, slc_step, _softmax_init(rows))
        s = _dot_nt(qg, new_rows(ksn_ref)) * ATTN_SCALE + bias_new
        carry = _softmax_step(carry, s, new_mask & (sel4[:, nbp:nbp + 1] > 0.5), new_rows(vsn_ref))
        o_slc = _softmax_done(carry)

        s = _dot_nt(qg, cwk_ref[0][:, cols].astype(BF16)) * ATTN_SCALE + bw_ref[g]
        carry = _softmax_step(_softmax_init(rows), s, win_mask, cwv_ref[0][:, cols].astype(BF16))
        s = _dot_nt(qg, new_rows(kwn_ref)) * ATTN_SCALE + bias_new
        carry = _softmax_step(carry, s, new_mask, new_rows(vwn_ref))
        o_win = _softmax_done(carry)

        heads = [g * GROUP + r for r in range(GROUP)]
        o = (_stack_cols(gates, heads) * ocmp_ref[0, g]
             + _stack_cols(gates, [N_HEADS_NSA + h for h in heads]) * o_slc
             + _stack_cols(gates, [2 * N_HEADS_NSA + h for h in heads]) * o_win)
        for r, h in enumerate(heads):
            o_ref[0, :, h * HEAD_DIM:(h + 1) * HEAD_DIM] = o[r * t:(r + 1) * t]


def _nsa_sample_attn(page_table, q3, gate3, o_cmp, sel, ksn, vsn, kwn, vwn, cwk, cwv, pool_k, pool_v,
                     bias_slc, bias_new, bias_win, expand):
    n, n_pages = page_table.shape
    t = q3.shape[1]
    past = n_pages * PAGE_SIZE
    wb = cwk.shape[1]
    width = sel.shape[3]
    assert past % KEY_CHUNK == 0 and t <= NEW_PAD
    per_seq = lambda shape: pl.BlockSpec((1,) + shape, lambda i, pt: (i,) + (0,) * len(shape))
    full = lambda a: pl.BlockSpec(a.shape, lambda i, pt: (0,) * a.ndim)
    anyspec = pl.BlockSpec(memory_space=pl.ANY)
    grid_spec = pltpu.PrefetchScalarGridSpec(
        num_scalar_prefetch=1,
        grid=(n,),
        in_specs=[per_seq((t, NSA_WIDTH)), per_seq((t, LANES)), per_seq((N_KV_NSA, GROUP * t, HEAD_DIM)),
                  per_seq((N_KV_NSA, t, width)),
                  per_seq((t, KV_WIDTH)), per_seq((t, KV_WIDTH)), per_seq((t, KV_WIDTH)), per_seq((t, KV_WIDTH)),
                  per_seq((wb, KV_WIDTH)), per_seq((wb, KV_WIDTH)), anyspec, anyspec,
                  full(bias_slc), full(bias_new), full(bias_win), full(expand)],
        out_specs=per_seq((t, NSA_WIDTH)),
        scratch_shapes=[pltpu.VMEM((2, past, KV_WIDTH), F32), pltpu.VMEM((2, past, KV_WIDTH), F32),
                        pltpu.SemaphoreType.DMA((2,)), pltpu.SemaphoreType.DMA((2,)),
                        pltpu.VMEM((GROUP * t, past), F32)],
    )
    return pl.pallas_call(
        functools.partial(_nsa_sample_attn_kernel, n_pages=n_pages),
        out_shape=jax.ShapeDtypeStruct((n, t, NSA_WIDTH), F32),
        grid_spec=grid_spec,
        compiler_params=_cparams(("arbitrary",)),
        name="nsa_sample_attn",
    )(page_table, q3, gate3, o_cmp, sel, ksn, vsn, kwn, vwn, cwk, cwv, pool_k, pool_v,
      bias_slc, bias_new, bias_win, expand)


def _segsum(x, ones_ref):
    hi = x.astype(BF16)
    lo = (x - hi.astype(F32)).astype(BF16)
    return _dot(hi, ones_ref[...]) + _dot(lo, ones_ref[...])


def _softplus(x):
    return jnp.maximum(x, 0.0) + jnp.log(1.0 + jnp.exp(-jnp.abs(x)))


def _rwkv_pre_kernel(p_ref, prev_ref, mu_ref, w0_ref, w2_ref, a0_ref, a2_ref, g2_ref, kk_ref, ka_ref, ones_ref,
                     r_ref, k_ref, v_ref, kkn_ref, kka_ref, d_ref, g_ref, last_ref, buf_ref):
    ti = pl.program_id(1)
    tt = p_ref.shape[1]
    p = p_ref[0]
    first = jnp.where(ti == 0, prev_ref[0], last_ref[...])
    last_ref[...] = p[tt - 1:tt]
    buf_ref[SUBLANES - 1:SUBLANES, :] = first
    buf_ref[SUBLANES:SUBLANES + tt, :] = p
    prev = buf_ref[SUBLANES - 1:SUBLANES - 1 + tt, :]
    xs = p + (prev - p) * mu_ref[...]
    W = RWKV_WIDTH
    r, k, v = xs[:, 0:W], xs[:, W:2 * W], xs[:, 2 * W:3 * W]
    c = 3 * W
    xw = xs[:, c:c + DECAY_LORA]
    xa = xs[:, c + DECAY_LORA:c + DECAY_LORA + AAA_LORA]
    xg = xs[:, c + DECAY_LORA + AAA_LORA:]
    w = -_softplus(-(w0_ref[...] + _dot(jnp.tanh(xw).astype(BF16), w2_ref[...]))) - 0.5
    a = jax.nn.sigmoid(a0_ref[...] + _dot(xa.astype(BF16), a2_ref[...]))
    kk = k * kk_ref[...]
    kk = kk / jnp.maximum(jnp.sqrt(_segsum(kk * kk, ones_ref)), 1e-12)
    r_ref[0] = r
    k_ref[0] = k * (1.0 + (a - 1.0) * ka_ref[...])
    v_ref[0] = v
    kkn_ref[0] = kk
    kka_ref[0] = kk * a
    d_ref[0] = -jnp.exp(w)
    g_ref[0] = _dot(jax.nn.sigmoid(xg).astype(BF16), g2_ref[...])


def _rwkv_pre(p3, prev, mu, w0, w2, a0, a2, g2, k_k, k_a, ones_bd):
    n, t, c = p3.shape
    tt = min(t, 256)
    assert t % tt == 0
    row = lambda a: a.reshape(1, -1)
    full = lambda a: pl.BlockSpec(a.shape, lambda b, i: (0,) * a.ndim)
    params = [row(mu), row(w0), w2, row(a0), a2, g2, row(k_k), row(k_a), ones_bd]
    out_spec = pl.BlockSpec((1, tt, RWKV_WIDTH), lambda b, i: (b, i, 0))
    return pl.pallas_call(
        _rwkv_pre_kernel,
        out_shape=[jax.ShapeDtypeStruct((n, t, RWKV_WIDTH), F32)] * 7,
        grid=(n, t // tt),
        in_specs=[pl.BlockSpec((1, tt, c), lambda b, i: (b, i, 0)),
                  pl.BlockSpec((1, 1, c), lambda b, i: (b, 0, 0))] + [full(a) for a in params],
        out_specs=[out_spec] * 7,
        scratch_shapes=[pltpu.VMEM((1, c), F32), pltpu.VMEM((SUBLANES + tt, c), F32)],
        compiler_params=_cparams(("parallel", "arbitrary")),
        name="rwkv_pre",
    )(p3, prev.reshape(n, 1, c), *params)


def _rwkv_scan_kernel(r_ref, k_ref, v_ref, kk_ref, kka_ref, ld_ref, s0_ref, y_ref, sout_ref, s_ref):
    ci = pl.program_id(1)
    t_in = r_ref.shape[1]
    C = TC

    @pl.when(ci == 0)
    def _():
        s_ref[...] = s0_ref[0]

    def rows(ref):
        x = ref[0]
        if t_in < C:
            x = jnp.concatenate([x, jnp.zeros((C - t_in, x.shape[1]), F32)], axis=0)
        return x

    ld = rows(ld_ref)
    tri = jnp.where(lax.broadcasted_iota(jnp.int32, (C, C), 1) <= lax.broadcasted_iota(jnp.int32, (C, C), 0),
                    1.0, 0.0).astype(BF16)
    hi = ld.astype(BF16)
    rem = ld - hi.astype(F32)
    mid = rem.astype(BF16)
    lo = (rem - mid.astype(F32)).astype(BF16)
    cum = _dot(tri, hi) + _dot(tri, mid) + _dot(tri, lo)
    p_inc = jnp.exp(cum)
    p_inv = jnp.exp(-cum)
    a_t = -rows(kk_ref) * jnp.exp(cum - ld)
    b_t = rows(kka_ref) * p_inv
    k_t = rows(k_ref) * p_inv
    r_t = rows(r_ref) * p_inc
    v = rows(v_ref)
    p_end = p_inc[C - 1:C]

    i2 = lax.broadcasted_iota(jnp.int32, (2 * C, 2 * C), 0)
    j2 = lax.broadcasted_iota(jnp.int32, (2 * C, 2 * C), 1)
    src = j2 & (C - 1)
    keep = src < jnp.where(i2 < C, i2, i2 - C + 1)

    heads = range(N_HEADS_RWKV)
    cols = [slice(h * HEAD_DIM, (h + 1) * HEAD_DIM) for h in heads]
    ar = [jnp.concatenate([a_t[:, c], r_t[:, c]], axis=0).astype(BF16) for c in cols]
    bk = [jnp.concatenate([b_t[:, c], k_t[:, c]], axis=0).astype(BF16) for c in cols]
    vh = [v[:, c].astype(BF16) for c in cols]
    s0 = [s_ref[h] for h in heads]
    quad = [jnp.where(keep, _dot_nt(ar[h], bk[h]), 0.0).astype(BF16) for h in heads]
    base = [_dot_nt(ar[h], s0[h].astype(BF16)) for h in heads]
    x = [base[h][:C] + _dot(quad[h][:C, C:], vh[h]) for h in heads]
    power = [quad[h][:C, :C] for h in heads]
    levels = C.bit_length() - 1
    for lvl in range(levels):
        x = [x[h] + _dot(power[h], x[h].astype(BF16)) for h in heads]
        if lvl + 1 < levels:
            power = [_dot(power[h], power[h]).astype(BF16) for h in heads]
    uv = [jnp.concatenate([x[h].astype(BF16), vh[h]], axis=0) for h in heads]
    ys = [base[h][C:] + _dot(quad[h][C:, :], uv[h]) for h in heads]
    for h in heads:
        upd = lax.dot_general(uv[h], bk[h], (((0,), (0,)), ((), ())), preferred_element_type=F32)
        s_ref[h] = (s0[h] + upd) * p_end[:, cols[h]]
    y = jnp.concatenate(ys, axis=1)
    y_ref[0] = y[:t_in]

    @pl.when(ci == pl.num_programs(1) - 1)
    def _():
        sout_ref[0] = s_ref[...]


def _rwkv_scan(r, k, v, kk, kka, d, s0):
    n, t, w = r.shape
    tc = min(t, TC)
    assert t % tc == 0 and TC & (TC - 1) == 0
    seq_spec = pl.BlockSpec((1, tc, w), lambda b, i: (b, i, 0))
    st_spec = pl.BlockSpec((1, N_HEADS_RWKV, HEAD_DIM, HEAD_DIM), lambda b, i: (b, 0, 0, 0))
    return pl.pallas_call(
        _rwkv_scan_kernel,
        out_shape=[jax.ShapeDtypeStruct((n, t, w), F32),
                   jax.ShapeDtypeStruct((n, N_HEADS_RWKV, HEAD_DIM, HEAD_DIM), F32)],
        grid=(n, t // tc),
        in_specs=[seq_spec] * 6 + [st_spec],
        out_specs=[seq_spec, st_spec],
        scratch_shapes=[pltpu.VMEM((N_HEADS_RWKV, HEAD_DIM, HEAD_DIM), F32)],
        compiler_params=_cparams(("parallel", "arbitrary")),
        name="rwkv_scan",
    )(r, k, v, kk, kka, d, s0)


def _mix_out_kernel(x_ref, nsa_ref, y_ref, r_ref, k_ref, v_ref, g_ref, rk_ref, lnw_ref, lnb_ref, ones_ref,
                    wa_ref, wb_ref, o_ref):
    y = y_ref[...]
    mean = _segsum(y, ones_ref) * (1.0 / HEAD_DIM)
    yc = y - mean
    var = _segsum(yc * yc, ones_ref) * (1.0 / HEAD_DIM)
    yn = yc * lax.rsqrt(var + GN_EPS) * lnw_ref[...] + lnb_ref[...]
    bonus = _segsum(r_ref[...] * k_ref[...] * rk_ref[...], ones_ref) * v_ref[...]
    rw = ((yn + bonus) * g_ref[...]).astype(BF16)
    o_ref[...] = x_ref[...] + _dot(nsa_ref[...].astype(BF16), wa_ref[...]) + _dot(rw, wb_ref[...])


def _mix_out(x, o_nsa, y, r, k, v, g, r_k, ln_w, ln_b, ones_bd, wa, wb):
    m, d = x.shape
    tm = min(m, 512)
    assert m % tm == 0
    row = lambda a: a.reshape(1, -1)
    tile = lambda w: pl.BlockSpec((tm, w), lambda i: (i, 0))
    full = lambda a: pl.BlockSpec(a.shape, lambda i: (0,) * a.ndim)
    params = [row(r_k), row(ln_w), row(ln_b), ones_bd, wa, wb]
    return pl.pallas_call(
        _mix_out_kernel,
        out_shape=jax.ShapeDtypeStruct((m, d), F32),
        grid=(m // tm,),
        in_specs=[tile(d), tile(NSA_WIDTH)] + [tile(RWKV_WIDTH)] * 5 + [full(a) for a in params],
        out_specs=tile(d),
        compiler_params=_cparams(("parallel",)),
        name="mix_out",
    )(x, o_nsa, y, r, k, v, g, *params)


def _block_diag2(w):
    z = jnp.zeros_like(w)
    return jnp.concatenate([jnp.concatenate([w, z], axis=-1), jnp.concatenate([z, w], axis=-1)], axis=-2)


def _compress_weights(pe, w1, w2):
    return (jnp.concatenate([pe] * N_KV_NSA, axis=1), _block_diag2(w1).astype(BF16), _block_diag2(w2).astype(BF16))


def kernel(x_prompt, x_sample, cache_cmp_k, cache_cmp_v, cache_slc_k, cache_slc_v, cache_win_k, cache_win_v,
           state_wkv, state_shift, page_table, rel_bias_table, ffn1_norm, ffn1_wg, ffn1_wu, ffn1_wd, mix_norm,
           w_in, cmp_pe_k, cmp_w1_k, cmp_w2_k, cmp_pe_v, cmp_w1_v, cmp_w2_v, shift_mu, decay_w0, decay_w2,
           aaa_a0, aaa_a2, gate_g2, k_k, k_a, r_k, ln_x_w, ln_x_b, w_out, ffn2_norm, ffn2_wg, ffn2_wu, ffn2_wd,
           final_norm):
    bp, seq, d = x_prompt.shape
    bs, ts, _ = x_sample.shape
    depth = w_in.shape[0]
    n_pages = page_table.shape[1]
    past = n_pages * PAGE_SIZE
    n_phys = cache_cmp_k.shape[1]
    wbuf = cache_win_k.shape[2]
    assert seq % QT == 0 and seq >= WINDOW and ts < CMP_BLOCK and wbuf == WINDOW
    nq = seq // QT
    nb = seq // SLC_BLOCK
    nbp = past // SLC_BLOCK
    nsa_main = NSA_WIDTH + 6 * KV_WIDTH
    nsa_cols = nsa_main + N_GATES

    bias_tiles = _bias_tiles(rel_bias_table)
    bias_cmp_p = _bias_rows(rel_bias_table, q0=0, rows=seq, n_keys=nb, key_first=CMP_BLOCK - 1, key_stride=CMP_BLOCK)
    bias_cmp_p = bias_cmp_p.reshape(N_KV_NSA, GROUP, nq, QT, nb).transpose(0, 2, 1, 3, 4).reshape(
        N_KV_NSA, nq, GROUP * QT, nb)
    stacked = lambda b: b.reshape(N_KV_NSA, GROUP * ts, b.shape[-1])
    bias_cmp_s = stacked(_bias_rows(rel_bias_table, q0=past, rows=ts, n_keys=nbp,
                                    key_first=CMP_BLOCK - 1, key_stride=CMP_BLOCK))
    bias_slc_s = stacked(_bias_rows(rel_bias_table, q0=past, rows=ts, n_keys=past, key_first=0, key_stride=1))
    bias_new_s = stacked(_bias_rows(rel_bias_table, q0=past, rows=ts, n_keys=LANES, key_first=past, key_stride=1))
    bias_win_s = stacked(_bias_rows(rel_bias_table, q0=past, rows=ts, n_keys=wbuf,
                                    key_first=past - wbuf, key_stride=1))
    expand_s = (jnp.arange(past, dtype=jnp.int32)[None, :] // SLC_BLOCK
                == jnp.arange(nbp, dtype=jnp.int32)[:, None]).astype(BF16)
    head_of = jnp.arange(RWKV_WIDTH, dtype=jnp.int32) // HEAD_DIM
    ones_bd = (head_of[:, None] == head_of[None, :]).astype(BF16)

    xp = x_prompt.reshape(bp * seq, d)
    xs = x_sample.reshape(bs * ts, d)
    states = []
    for l in range(depth):
        bf = lambda a: a[l].astype(BF16)
        w_in_l = w_in[l]
        w_pad = jnp.concatenate(
            [w_in_l[:, :nsa_main], w_in_l[:, nsa_main:nsa_cols], jnp.zeros((d, LANES - N_GATES), F32),
             w_in_l[:, nsa_cols:]], axis=1).astype(BF16)
        cw_k = _compress_weights(cmp_pe_k[l], cmp_w1_k[l], cmp_w2_k[l])
        cw_v = _compress_weights(cmp_pe_v[l], cmp_w1_v[l], cmp_w2_v[l])
        wa, wb = w_out[l][:NSA_WIDTH].astype(BF16), w_out[l][NSA_WIDTH:].astype(BF16)
        rw_pre = (shift_mu[l], decay_w0[l], bf(decay_w2), aaa_a0[l], bf(aaa_a2), bf(gate_g2), k_k[l], k_a[l], ones_bd)
        ffn1 = (ffn1_norm[l], bf(ffn1_wg), bf(ffn1_wu), bf(ffn1_wd), final_norm)
        ffn2 = (ffn2_norm[l], bf(ffn2_wg), bf(ffn2_wu), bf(ffn2_wd), final_norm)
        last = l == depth - 1

        xp = _ffn_half(xp, *ffn1, final=False)
        xs = _ffn_half(xs, *ffn1, final=False)

        q, kc, vc, ks, vs, kw, vw, gate, zr = _proj(xp, mix_norm[l], w_pad)
        kcc = _compress(kc, *cw_k)
        vcc = _compress(vc, *cw_v)
        o_nsa = _nsa_prompt(q, gate, ks, vs, kw, vw, kcc, vcc, bias_tiles, bias_cmp_p, bp, seq)
        zr3 = zr.reshape(bp, seq, RWKV_COLS)
        r, k2, v, kk, kka, dec, gg = _rwkv_pre(zr3, jnp.zeros((bp, RWKV_COLS), F32), *rw_pre)
        y, wkv_p = _rwkv_scan(r, k2, v, kk, kka, dec,
                              jnp.zeros((bp, N_HEADS_RWKV, HEAD_DIM, HEAD_DIM), F32))
        flat = lambda a: a.reshape(-1, RWKV_WIDTH)
        xp = _mix_out(xp, o_nsa, flat(y), flat(r), flat(k2), flat(v), flat(gg), r_k[l], ln_x_w[l], ln_x_b[l],
                      ones_bd, wa, wb)
        kv5 = lambda a, n_, t_: a.reshape(n_, t_, N_KV_NSA, HEAD_DIM)
        p_state = (kv5(kc, bp, seq), kv5(vc, bp, seq), kv5(ks, bp, seq), kv5(vs, bp, seq),
                   kv5(kw, bp, seq)[:, seq - WINDOW:], kv5(vw, bp, seq)[:, seq - WINDOW:],
                   wkv_p, zr3[:, seq - 1])

        q, kc, vc, ks, vs, kw, vw, gate, zr = _proj(xs, mix_norm[l], w_pad)
        pool = lambda c: c[l].reshape(n_phys, PAGE_SIZE, KV_WIDTH)
        kcc, vcc = _compress_paged(page_table, pool(cache_cmp_k), pool(cache_cmp_v), cw_k, cw_v)
        seq3 = lambda a: a.reshape(bs, ts, a.shape[-1])
        o_cmp, sel = _nsa_sample_cmp(seq3(q), kcc, vcc, bias_cmp_s, past)
        cwk = cache_win_k[l].reshape(bs, wbuf, KV_WIDTH)
        cwv = cache_win_v[l].reshape(bs, wbuf, KV_WIDTH)
        o_nsa = _nsa_sample_attn(page_table, seq3(q), seq3(gate), o_cmp, sel, seq3(ks), seq3(vs), seq3(kw),
                                 seq3(vw), cwk, cwv, pool(cache_slc_k), pool(cache_slc_v),
                                 bias_slc_s, bias_new_s, bias_win_s, expand_s)
        zr3 = seq3(zr)
        r, k2, v, kk, kka, dec, gg = _rwkv_pre(zr3, state_shift[l], *rw_pre)
        y, wkv_s = _rwkv_scan(r, k2, v, kk, kka, dec, state_wkv[l])
        xs = _mix_out(xs, o_nsa.reshape(bs * ts, NSA_WIDTH), flat(y), flat(r), flat(k2), flat(v), flat(gg),
                      r_k[l], ln_x_w[l], ln_x_b[l], ones_bd, wa, wb)
        win = lambda cache, new: jnp.concatenate([cache[l], kv5(new, bs, ts)], axis=1)[:, ts:]
        s_state = (kv5(kc, bs, ts), kv5(vc, bs, ts), kv5(ks, bs, ts), kv5(vs, bs, ts),
                   win(cache_win_k, kw), win(cache_win_v, vw), wkv_s, zr3[:, ts - 1])

        xp = _ffn_half(xp, *ffn2, final=last)
        xs = _ffn_half(xs, *ffn2, final=last)
        states.append(p_state + s_state)

    stacked_states = [jnp.stack(z) for z in zip(*states)]
    return (xp.reshape(bp, seq, d), xs.reshape(bs, ts, d), *stacked_states)
```

```python
import functools
import math

import jax
import jax.numpy as jnp
import numpy as np
from jax import lax
from jax.experimental import pallas as pl
from jax.experimental.pallas import tpu as pltpu

F32 = jnp.float32
BF16 = jnp.bfloat16

HEAD_DIM = 64
N_HEADS_NSA = 8
N_KV_NSA = 2
GROUP = N_HEADS_NSA // N_KV_NSA
N_HEADS_RWKV = 8
NSA_WIDTH = N_HEADS_NSA * HEAD_DIM
RWKV_WIDTH = N_HEADS_RWKV * HEAD_DIM
KV_WIDTH = N_KV_NSA * HEAD_DIM
CMP_BLOCK = 64
SLC_BLOCK = 64
TOP_K_BLOCKS = 16
WINDOW = 512
PAGE_SIZE = 128
N_BUCKETS = 32
MAX_DISTANCE = 128
DECAY_LORA = 64
AAA_LORA = 64
GATE_LORA = 128
NORM_EPS = 1e-6
GN_EPS = 64e-5
ATTN_SCALE = HEAD_DIM ** -0.5
FORCE_SCORE = 1e4
NEG_INF = -1e30
M_FLOOR = -1e29
RWKV_COLS = 3 * RWKV_WIDTH + DECAY_LORA + AAA_LORA + GATE_LORA
N_GATES = 3 * N_HEADS_NSA

LANES = 128
SUBLANES = 8
VMEM_LIMIT = 56 * 1024 * 1024

QT = 128
TC = 64


def _bucket_thresholds():
    max_exact = N_BUCKETS // 2
    thr = list(range(max_exact + 1))
    for b in range(max_exact + 1, N_BUCKETS):
        x = max_exact * (MAX_DISTANCE / max_exact) ** ((b - max_exact) / (N_BUCKETS - max_exact))
        thr.append(int(math.ceil(x - 1e-9)))
    return thr


BUCKET_THR = _bucket_thresholds()


def _cparams(sem):
    return pltpu.CompilerParams(dimension_semantics=sem, vmem_limit_bytes=VMEM_LIMIT)


def _dot(a, b):
    return jnp.dot(a, b, preferred_element_type=F32)


def _dot_nt(a, b):
    return lax.dot_general(a, b, (((1,), (1,)), ((), ())), preferred_element_type=F32)


def _rms(x, g):
    return x * lax.rsqrt(jnp.mean(x * x, axis=-1, keepdims=True) + NORM_EPS) * g


def _ffn_kernel(x_ref, g_ref, wg_ref, wu_ref, wd_ref, gf_ref, o_ref, h_ref, acc_ref, *, final):
    j = pl.program_id(1)

    @pl.when(j == 0)
    def _():
        h_ref[...] = _rms(x_ref[...], g_ref[...]).astype(BF16)
        acc_ref[...] = jnp.zeros_like(acc_ref)

    h = h_ref[...]
    a = _dot(h, wg_ref[...])
    b = _dot(h, wu_ref[...])
    act = (a * jax.nn.sigmoid(a) * b).astype(BF16)
    acc_ref[...] += _dot(act, wd_ref[...])

    @pl.when(j == pl.num_programs(1) - 1)
    def _():
        y = x_ref[...] + 0.5 * acc_ref[...]
        if final:
            y = _rms(y, gf_ref[...])
        o_ref[...] = y


def _ffn_half(x, g, wg, wu, wd, gf, *, final):
    m, d = x.shape
    ff = wg.shape[1]
    tm = min(m, 1024)
    tf = 256 if ff % 256 == 0 else ff
    assert m % tm == 0 and ff % tf == 0
    return pl.pallas_call(
        functools.partial(_ffn_kernel, final=final),
        out_shape=jax.ShapeDtypeStruct((m, d), F32),
        grid=(m // tm, ff // tf),
        in_specs=[
            pl.BlockSpec((tm, d), lambda i, j: (i, 0)),
            pl.BlockSpec((1, d), lambda i, j: (0, 0)),
            pl.BlockSpec((d, tf), lambda i, j: (0, j)),
            pl.BlockSpec((d, tf), lambda i, j: (0, j)),
            pl.BlockSpec((tf, d), lambda i, j: (j, 0)),
            pl.BlockSpec((1, d), lambda i, j: (0, 0)),
        ],
        out_specs=pl.BlockSpec((tm, d), lambda i, j: (i, 0)),
        scratch_shapes=[pltpu.VMEM((tm, d), BF16), pltpu.VMEM((tm, d), F32)],
        compiler_params=_cparams(("parallel", "arbitrary")),
        name="ffn_half",
    )(x, g.reshape(1, d), wg, wu, wd, gf.reshape(1, d))


def _proj_kernel(x_ref, g_ref, w_ref, q_ref, kc_ref, vc_ref, ks_ref, vs_ref, kw_ref, vw_ref, gate_ref, zr_ref):
    h = _rms(x_ref[...], g_ref[...]).astype(BF16)
    z = _dot(h, w_ref[...])
    c = 0
    for ref in (q_ref, kc_ref, vc_ref, ks_ref, vs_ref, kw_ref, vw_ref, gate_ref, zr_ref):
        w = ref.shape[1]
        ref[...] = z[:, c:c + w]
        c += w


def _proj(x, g, w_pad):
    m, d = x.shape
    tm = min(m, 512)
    widths = (NSA_WIDTH,) + (KV_WIDTH,) * 6 + (LANES, RWKV_COLS)
    assert sum(widths) == w_pad.shape[1] and m % tm == 0
    return pl.pallas_call(
        _proj_kernel,
        out_shape=[jax.ShapeDtypeStruct((m, w), F32) for w in widths],
        grid=(m // tm,),
        in_specs=[
            pl.BlockSpec((tm, d), lambda i: (i, 0)),
            pl.BlockSpec((1, d), lambda i: (0, 0)),
            pl.BlockSpec(w_pad.shape, lambda i: (0, 0)),
        ],
        out_specs=[pl.BlockSpec((tm, w), lambda i: (i, 0)) for w in widths],
        compiler_params=_cparams(("parallel",)),
        name="in_proj",
    )(x, g.reshape(1, d), w_pad)


def _out_kernel(x_ref, a_ref, b_ref, wa_ref, wb_ref, o_ref):
    o_ref[...] = (x_ref[...] + _dot(a_ref[...].astype(BF16), wa_ref[...])
                  + _dot(b_ref[...].astype(BF16), wb_ref[...]))


def _out_proj(x, a, b, wa, wb):
    m, d = x.shape
    tm = min(m, 512)
    return pl.pallas_call(
        _out_kernel,
        out_shape=jax.ShapeDtypeStruct((m, d), F32),
        grid=(m // tm,),
        in_specs=[
            pl.BlockSpec((tm, d), lambda i: (i, 0)),
            pl.BlockSpec((tm, a.shape[1]), lambda i: (i, 0)),
            pl.BlockSpec((tm, b.shape[1]), lambda i: (i, 0)),
            pl.BlockSpec(wa.shape, lambda i: (0, 0)),
            pl.BlockSpec(wb.shape, lambda i: (0, 0)),
        ],
        out_specs=pl.BlockSpec((tm, d), lambda i: (i, 0)),
        compiler_params=_cparams(("parallel",)),
        name="out_proj",
    )(x, a, b, wa, wb)


def _bias_lookup(dist, table_ref, heads):
    vals = [jnp.full(dist.shape, table_ref[0, h], F32) for h in heads]
    for b in range(1, N_BUCKETS):
        ge = dist >= BUCKET_THR[b]
        vals = [jnp.where(ge, table_ref[b, h], v) for h, v in zip(heads, vals)]
    return vals


def _bias_tiles_kernel(table_ref, o_ref):
    j = lax.broadcasted_iota(jnp.int32, (QT, QT), 0)
    i = lax.broadcasted_iota(jnp.int32, (QT, QT), 1)
    for off in range(3):
        vals = _bias_lookup(off * QT + i - j, table_ref, range(N_HEADS_NSA))
        for h, v in enumerate(vals):
            g, r = divmod(h, GROUP)
            o_ref[g, off, :, r * QT:(r + 1) * QT] = v


def _bias_tiles(table):
    return pl.pallas_call(
        _bias_tiles_kernel,
        out_shape=jax.ShapeDtypeStruct((N_KV_NSA, 3, QT, GROUP * QT), F32),
        in_specs=[pl.BlockSpec(memory_space=pltpu.SMEM)],
        name="bias_tiles",
    )(table)


def _bias_rows_kernel(table_ref, o_ref, *, q0, key_first, key_stride):
    _, _, rc, L = o_ref.shape
    i = lax.broadcasted_iota(jnp.int32, (rc, L), 0) + pl.program_id(0) * rc
    j = lax.broadcasted_iota(jnp.int32, (rc, L), 1)
    dist = (q0 + i) - (key_first + key_stride * j)
    vals = _bias_lookup(dist, table_ref, range(N_HEADS_NSA))
    for h, v in enumerate(vals):
        g, r = divmod(h, GROUP)
        o_ref[g, r] = v


def _bias_rows(table, *, q0, rows, n_keys, key_first, key_stride):
    rc = min(rows, 256)
    assert rows % rc == 0
    return pl.pallas_call(
        functools.partial(_bias_rows_kernel, q0=q0, key_first=key_first, key_stride=key_stride),
        out_shape=jax.ShapeDtypeStruct((N_KV_NSA, GROUP, rows, n_keys), F32),
        grid=(rows // rc,),
        in_specs=[pl.BlockSpec(memory_space=pltpu.SMEM)],
        out_specs=pl.BlockSpec((N_KV_NSA, GROUP, rc, n_keys), lambda i: (0, 0, i, 0)),
        compiler_params=_cparams(("parallel",)),
        name="bias_rows",
    )(table)


def _gelu_tanh(x):
    return 0.5 * x * (1.0 + jnp.tanh(math.sqrt(2.0 / math.pi) * (x + 0.044715 * (x * x * x))))


def _compress_blocks(load_rows, nblk, pe_ref, w1_ref, w2_ref):
    acc = jnp.zeros((nblk, KV_WIDTH), F32)
    for c in range(CMP_BLOCK):
        rows = load_rows(c) + pe_ref[c:c + 1, :]
        acc = acc + _dot(rows.astype(BF16), w1_ref[c])
    return _dot(_gelu_tanh(acc).astype(BF16), w2_ref[...])


def _compress_kernel(x_ref, pe_ref, w1_ref, w2_ref, o_ref):
    nblk = o_ref.shape[0]
    o_ref[...] = _compress_blocks(lambda c: x_ref[pl.ds(c, nblk, stride=CMP_BLOCK), :],
                                  nblk, pe_ref, w1_ref, w2_ref)


def _compress(x, pe2, w1bd, w2bd):
    rows = x.shape[0]
    nb = rows // CMP_BLOCK
    nblk = min(nb, 128)
    assert nb % nblk == 0
    return pl.pallas_call(
        _compress_kernel,
        out_shape=jax.ShapeDtypeStruct((nb, KV_WIDTH), F32),
        grid=(nb // nblk,),
        in_specs=[
            pl.BlockSpec((nblk * CMP_BLOCK, KV_WIDTH), lambda i: (i, 0)),
            pl.BlockSpec(pe2.shape, lambda i: (0, 0)),
            pl.BlockSpec(w1bd.shape, lambda i: (0, 0, 0)),
            pl.BlockSpec(w2bd.shape, lambda i: (0, 0)),
        ],
        out_specs=pl.BlockSpec((nblk, KV_WIDTH), lambda i: (i, 0)),
        compiler_params=_cparams(("parallel",)),
        name="compress",
    )(x, pe2, w1bd, w2bd)


def _page_copies(pt_ref, n, pool_ref, buf_ref, sem_ref, slot, n_pages):
    return [pltpu.make_async_copy(pool_ref.at[pt_ref[n, p]],
                                  buf_ref.at[slot, pl.ds(p * PAGE_SIZE, PAGE_SIZE)],
                                  sem_ref.at[slot]) for p in range(n_pages)]


def _paged_prefetch(pt_ref, pools, bufs, sems, n_pages):
    n = pl.program_id(0)
    slot = n % 2

    def start(seq, slot_):
        for pool, buf, sem in zip(pools, bufs, sems):
            for cp in _page_copies(pt_ref, seq, pool, buf, sem, slot_, n_pages):
                cp.start()

    @pl.when(n == 0)
    def _():
        start(0, 0)

    @pl.when(n + 1 < pl.num_programs(0))
    def _():
        start(n + 1, 1 - slot)

    for pool, buf, sem in zip(pools, bufs, sems):
        for cp in _page_copies(pt_ref, n, pool, buf, sem, slot, n_pages):
            cp.wait()
    return slot


def _compress_paged_kernel(pt_ref, pk_ref, pv_ref, pek_ref, w1k_ref, w2k_ref, pev_ref, w1v_ref, w2v_ref,
                           ok_ref, ov_ref, bk_ref, bv_ref, sk_ref, sv_ref, *, n_pages):
    slot = _paged_prefetch(pt_ref, (pk_ref, pv_ref), (bk_ref, bv_ref), (sk_ref, sv_ref), n_pages)
    nblk = ok_ref.shape[1]
    ok_ref[0] = _compress_blocks(lambda c: bk_ref[slot, pl.ds(c, nblk, stride=CMP_BLOCK), :],
                                 nblk, pek_ref, w1k_ref, w2k_ref)
    ov_ref[0] = _compress_blocks(lambda c: bv_ref[slot, pl.ds(c, nblk, stride=CMP_BLOCK), :],
                                 nblk, pev_ref, w1v_ref, w2v_ref)


def _compress_paged(page_table, pool_k, pool_v, cw_k, cw_v):
    n, n_pages = page_table.shape
    past = n_pages * PAGE_SIZE
    nblk = past // CMP_BLOCK
    wspec = lambda a: pl.BlockSpec(a.shape, lambda i, pt: (0,) * a.ndim)
    grid_spec = pltpu.PrefetchScalarGridSpec(
        num_scalar_prefetch=1,
        grid=(n,),
        in_specs=[pl.BlockSpec(memory_space=pl.ANY), pl.BlockSpec(memory_space=pl.ANY)]
                 + [wspec(a) for a in cw_k + cw_v],
        out_specs=[pl.BlockSpec((1, nblk, KV_WIDTH), lambda i, pt: (i, 0, 0))] * 2,
        scratch_shapes=[pltpu.VMEM((2, past, KV_WIDTH), F32), pltpu.VMEM((2, past, KV_WIDTH), F32),
                        pltpu.SemaphoreType.DMA((2,)), pltpu.SemaphoreType.DMA((2,))],
    )
    return pl.pallas_call(
        functools.partial(_compress_paged_kernel, n_pages=n_pages),
        out_shape=[jax.ShapeDtypeStruct((n, nblk, KV_WIDTH), F32)] * 2,
        grid_spec=grid_spec,
        compiler_params=_cparams(("arbitrary",)),
        name="compress_paged",
    )(page_table, pool_k, pool_v, *cw_k, *cw_v)


def _stack_heads(x, g):
    return jnp.concatenate(
        [x[:, (g * GROUP + r) * HEAD_DIM:(g * GROUP + r + 1) * HEAD_DIM] for r in range(GROUP)], axis=0)


def _stack_cols(x, cols):
    return jnp.concatenate([x[:, c:c + 1] for c in cols], axis=0)


def _softmax_step(carry, s, mask, v):
    m, l, acc = carry
    s = jnp.where(mask, s, NEG_INF)
    m_new = jnp.maximum(m, jnp.max(s, axis=1, keepdims=True))
    p = jnp.where(mask, jnp.exp(s - m_new), 0.0)
    alpha = jnp.exp(m - m_new)
    l = alpha * l + jnp.sum(p, axis=1, keepdims=True)
    acc = alpha * acc + _dot(p.astype(BF16), v)
    return m_new, l, acc


def _softmax_init(rows):
    return (jnp.full((rows, 1), NEG_INF, F32), jnp.zeros((rows, 1), F32), jnp.zeros((rows, HEAD_DIM), F32))


def _softmax_done(carry):
    _, l, acc = carry
    return acc / jnp.maximum(l, 1e-30)


def _topk_mask(imp, cur, n_blocks):
    blk = lax.broadcasted_iota(jnp.int32, imp.shape, 1)
    forced = (blk == 0) | (blk == cur) | (blk == cur - 1)
    score = jnp.where(blk <= cur, jnp.where(forced, FORCE_SCORE, imp), -FORCE_SCORE)
    rank = jnp.zeros(imp.shape, jnp.int32)
    for b in range(n_blocks):
        col = score[:, b:b + 1]
        ahead = (col > score) | ((col == score) & (blk > b))
        rank = rank + jnp.where(ahead, 1, 0)
    return (rank < TOP_K_BLOCKS) & (blk <= cur)


def _topk_mask_t(imp, cur, n_blocks):
    blk = lax.broadcasted_iota(jnp.int32, imp.shape, 0)
    forced = (blk == 0) | (blk == cur) | (blk == cur - 1)
    score = jnp.where(blk <= cur, jnp.where(forced, FORCE_SCORE, imp), -FORCE_SCORE)
    rank = jnp.zeros(imp.shape, jnp.int32)
    for b in range(n_blocks):
        row = score[b:b + 1, :]
        rank = rank + jnp.where(row > score, 1, jnp.where(row == score, jnp.where(blk > b, 1, 0), 0))
    return jnp.where(rank < TOP_K_BLOCKS, jnp.where(blk <= cur, 1.0, 0.0), 0.0)


def _nsa_prompt_kernel(q_ref, gate_ref, ks_ref, vs_ref, kw_ref, vw_ref, kcc_ref, vcc_ref, bt_ref, bc_ref,
                       o_ref, selx_ref):
    qi = pl.program_id(1)
    seq = ks_ref.shape[1]
    nb = seq // SLC_BLOCK
    rows = GROUP * QT
    groups = range(N_KV_NSA)
    q = q_ref[...] * ATTN_SCALE
    gates_t = jax.nn.sigmoid(gate_ref[...]).T
    tpos_row = qi * QT + lax.broadcasted_iota(jnp.int32, (1, QT), 1)
    lane = lax.broadcasted_iota(jnp.int32, (1, KV_WIDTH), 1)
    own = [(lane // HEAD_DIM) == g for g in groups]
    zeros = jnp.zeros((rows, HEAD_DIM), F32)
    qg = []
    for g in groups:
        qs = _stack_heads(q, g)
        qg.append(jnp.concatenate([qs, zeros] if g == 0 else [zeros, qs], axis=1).astype(BF16))

    blk = lax.broadcasted_iota(jnp.int32, (nb, 1), 0)
    tpos_stacked = qi * QT + (lax.broadcasted_iota(jnp.int32, (1, rows), 1) & (QT - 1))
    readable = (blk + 1) * CMP_BLOCK - 1 <= tpos_stacked
    kcc = kcc_ref[0].astype(BF16)
    vcc = vcc_ref[0].astype(BF16)
    expand = (lax.broadcasted_iota(jnp.int32, (nb, seq), 1) // SLC_BLOCK
              == lax.broadcasted_iota(jnp.int32, (nb, seq), 0)).astype(BF16)
    o_cmp = []
    for g in groups:
        s = jnp.where(readable, _dot_nt(kcc, qg[g]) + bc_ref[g, 0], NEG_INF)
        e = jnp.where(readable, jnp.exp(s - jnp.max(s, axis=0, keepdims=True)), 0.0)
        p = e / jnp.maximum(jnp.sum(e, axis=0, keepdims=True), 1e-30)
        o_cmp.append(lax.dot_general(vcc, p.astype(BF16), (((0,), (0,)), ((), ())), preferred_element_type=F32))
        imp = p[:, 0:QT]
        for r in range(1, GROUP):
            imp = imp + p[:, r * QT:(r + 1) * QT]
        sel = _topk_mask_t(imp, tpos_row // SLC_BLOCK, nb)
        selx_ref[g] = lax.dot_general(expand, sel.astype(BF16), (((0,), (0,)), ((), ())),
                                      preferred_element_type=F32)

    def tile_step(kt, carry, k_ref, v_ref, penalty):
        off = pl.multiple_of(kt * QT, QT)
        k = k_ref[0, pl.ds(off, QT), :].astype(BF16)
        v = v_ref[0, pl.ds(off, QT), :]
        qk = [_dot_nt(k, qg[g]) for g in groups]
        out = []
        for g in groups:
            m, acc = carry[g]
            s = qk[g] + bt_ref[g, jnp.minimum(qi - kt, 2)]
            s = s + jnp.concatenate([penalty(g, off)] * GROUP, axis=1)
            m_new = jnp.maximum(m, jnp.max(s, axis=0, keepdims=True))
            p = jnp.exp(s - m_new).astype(BF16)
            v_ext = jnp.where(own[g], v, 1.0).astype(BF16)
            pv = lax.dot_general(v_ext, p, (((0,), (0,)), ((), ())), preferred_element_type=F32)
            out.append((m_new, jnp.exp(m - m_new) * acc + pv))
        return tuple(out)

    def finish(carry, g):
        acc = carry[g][1]
        denom = acc[(1 - g) * HEAD_DIM:(1 - g) * HEAD_DIM + 1, :]
        return acc / jnp.maximum(denom, 1e-30)

    init = tuple((jnp.full((1, rows), M_FLOOR, F32), jnp.zeros((KV_WIDTH, rows), F32)) for _ in groups)
    key_row = lax.broadcasted_iota(jnp.int32, (QT, 1), 0)

    def slc_penalty(g, off):
        causal = jnp.where(tpos_row - (off + key_row) >= 0, 0.0, NEG_INF)
        return jnp.where(selx_ref[g, pl.ds(off, QT), :] > 0.5, causal, NEG_INF)

    def win_penalty(g, off):
        dist = tpos_row - (off + key_row)
        return jnp.where(dist >= 0, jnp.where(dist <= WINDOW, 0.0, NEG_INF), NEG_INF)

    slc = lax.fori_loop(0, qi + 1, lambda kt, c: tile_step(kt, c, ks_ref, vs_ref, slc_penalty), init)
    win = lax.fori_loop(jnp.maximum(qi - WINDOW // QT, 0), qi + 1,
                        lambda kt, c: tile_step(kt, c, kw_ref, vw_ref, win_penalty), init)

    def gate_row(branch, g):
        first = branch * N_HEADS_NSA + g * GROUP
        return jnp.concatenate([gates_t[first + r:first + r + 1, :] for r in range(GROUP)], axis=1)

    o_t = [gate_row(0, g) * o_cmp[g] + gate_row(1, g) * finish(slc, g) + gate_row(2, g) * finish(win, g)
           for g in groups]
    feature_row = lax.broadcasted_iota(jnp.int32, (KV_WIDTH, 1), 0)
    o = jnp.where(feature_row < HEAD_DIM, o_t[0], o_t[1]).T
    for g in groups:
        for r in range(GROUP):
            h = g * GROUP + r
            o_ref[:, h * HEAD_DIM:(h + 1) * HEAD_DIM] = o[r * QT:(r + 1) * QT, g * HEAD_DIM:(g + 1) * HEAD_DIM]


def _nsa_prompt(q, gate, ks, vs, kw, vw, kcc, vcc, bias_tiles, bias_cmp, n, seq):
    nq = seq // QT
    nb = seq // SLC_BLOCK
    seq_spec = pl.BlockSpec((1, seq, KV_WIDTH), lambda b, i: (b, 0, 0))
    cmp_spec = pl.BlockSpec((1, nb, KV_WIDTH), lambda b, i: (b, 0, 0))
    return pl.pallas_call(
        _nsa_prompt_kernel,
        out_shape=jax.ShapeDtypeStruct((n * seq, NSA_WIDTH), F32),
        grid=(n, nq),
        in_specs=[
            pl.BlockSpec((QT, NSA_WIDTH), lambda b, i: (b * nq + i, 0)),
            pl.BlockSpec((QT, LANES), lambda b, i: (b * nq + i, 0)),
            seq_spec, seq_spec, seq_spec, seq_spec, cmp_spec, cmp_spec,
            pl.BlockSpec(bias_tiles.shape, lambda b, i: (0, 0, 0, 0)),
            pl.BlockSpec((N_KV_NSA, 1, nb, GROUP * QT), lambda b, i: (0, i, 0, 0)),
        ],
        out_specs=pl.BlockSpec((QT, NSA_WIDTH), lambda b, i: (b * nq + i, 0)),
        scratch_shapes=[pltpu.VMEM((N_KV_NSA, seq, QT), F32)],
        compiler_params=_cparams(("parallel", "arbitrary")),
        name="nsa_prompt",
    )(q, gate, ks.reshape(n, seq, KV_WIDTH), vs.reshape(n, seq, KV_WIDTH), kw.reshape(n, seq, KV_WIDTH),
      vw.reshape(n, seq, KV_WIDTH), kcc.reshape(n, nb, KV_WIDTH), vcc.reshape(n, nb, KV_WIDTH),
      bias_tiles, bias_cmp)


NEW_PAD = SUBLANES


def _row_token(t):
    return jnp.concatenate([lax.broadcasted_iota(jnp.int32, (t, 1), 0)] * GROUP, axis=0)


def _nsa_sample_cmp_kernel(q_ref, kcc_ref, vcc_ref, bc_ref, o_ref, sel_ref, *, past):
    t = q_ref.shape[1]
    nbp = kcc_ref.shape[1]
    width = sel_ref.shape[3]
    q = q_ref[0]
    pos = past + _row_token(t)
    blk_end = (lax.broadcasted_iota(jnp.int32, (1, nbp), 1) + 1) * CMP_BLOCK - 1
    cur = (past + lax.broadcasted_iota(jnp.int32, (t, 1), 0)) // SLC_BLOCK
    for g in range(N_KV_NSA):
        cols = slice(g * HEAD_DIM, (g + 1) * HEAD_DIM)
        qg = _stack_heads(q, g).astype(BF16)
        s = _dot_nt(qg, kcc_ref[0][:, cols].astype(BF16)) * ATTN_SCALE + bc_ref[g]
        mask = blk_end <= pos
        s = jnp.where(mask, s, NEG_INF)
        e = jnp.where(mask, jnp.exp(s - jnp.max(s, axis=1, keepdims=True)), 0.0)
        p = e / jnp.maximum(jnp.sum(e, axis=1, keepdims=True), 1e-30)
        o_ref[0, g] = _dot(p.astype(BF16), vcc_ref[0][:, cols].astype(BF16))
        imp = p[0:t]
        for r in range(1, GROUP):
            imp = imp + p[r * t:(r + 1) * t]
        imp = jnp.concatenate([imp, jnp.zeros((t, width - nbp), F32)], axis=1)
        sel = _topk_mask(imp, cur, nbp + 1)
        sel_ref[0, g] = jnp.where(sel, 1.0, 0.0)


def _nsa_sample_cmp(q3, kcc, vcc, bias_cmp, past):
    n, t, _ = q3.shape
    nbp = kcc.shape[1]
    width = -(-(nbp + 1) // LANES) * LANES
    return pl.pallas_call(
        functools.partial(_nsa_sample_cmp_kernel, past=past),
        out_shape=[jax.ShapeDtypeStruct((n, N_KV_NSA, GROUP * t, HEAD_DIM), F32),
                   jax.ShapeDtypeStruct((n, N_KV_NSA, t, width), F32)],
        grid=(n,),
        in_specs=[
            pl.BlockSpec((1, t, NSA_WIDTH), lambda i: (i, 0, 0)),
            pl.BlockSpec((1, nbp, KV_WIDTH), lambda i: (i, 0, 0)),
            pl.BlockSpec((1, nbp, KV_WIDTH), lambda i: (i, 0, 0)),
            pl.BlockSpec(bias_cmp.shape, lambda i: (0, 0, 0)),
        ],
        out_specs=[pl.BlockSpec((1, N_KV_NSA, GROUP * t, HEAD_DIM), lambda i: (i, 0, 0, 0)),
                   pl.BlockSpec((1, N_KV_NSA, t, width), lambda i: (i, 0, 0, 0))],
        compiler_params=_cparams(("parallel",)),
        name="nsa_sample_cmp",
    )(q3, kcc, vcc, bias_cmp)


KEY_CHUNK = 1024


def _nsa_sample_attn_kernel(pt_ref, q_ref, gate_ref, ocmp_ref, sel_ref, ksn_ref, vsn_ref, kwn_ref, vwn_ref,
                            cwk_ref, cwv_ref, pk_ref, pv_ref, bs_ref, bn_ref, bw_ref, ex_ref,
                            o_ref, bk_ref, bv_ref, sk_ref, sv_ref, selx_ref, *, n_pages):
    slot = _paged_prefetch(pt_ref, (pk_ref, pv_ref), (bk_ref, bv_ref), (sk_ref, sv_ref), n_pages)
    t = q_ref.shape[1]
    rows = GROUP * t
    past = n_pages * PAGE_SIZE
    nbp = past // SLC_BLOCK
    wb = cwk_ref.shape[1]
    q = q_ref[0]
    gates = jax.nn.sigmoid(gate_ref[0])
    tok = _row_token(t)
    jn = lax.broadcasted_iota(jnp.int32, (1, NEW_PAD), 1)
    new_mask = jn <= tok
    jw = lax.broadcasted_iota(jnp.int32, (1, wb), 1)
    wdist = wb + tok - jw
    win_mask = (wdist >= 0) & (wdist <= WINDOW)
    pad = jnp.zeros((NEW_PAD - t, HEAD_DIM), F32)

    for g in range(N_KV_NSA):
        cols = slice(g * HEAD_DIM, (g + 1) * HEAD_DIM)
        qg = _stack_heads(q, g).astype(BF16)
        sel4 = jnp.concatenate([sel_ref[0, g]] * GROUP, axis=0)
        selx_ref[...] = _dot(sel4[:, :nbp].astype(BF16), ex_ref[...])
        bias_new = bn_ref[g][:, :NEW_PAD]

        def new_rows(ref):
            return jnp.concatenate([ref[0][:, cols], pad], axis=0).astype(BF16)

        def slc_step(c, carry):
            off = pl.multiple_of(c * KEY_CHUNK, KEY_CHUNK)
            k = bk_ref[slot, pl.ds(off, KEY_CHUNK), cols].astype(BF16)
            v = bv_ref[slot, pl.ds(off, KEY_CHUNK), cols].astype(BF16)
            s = _dot_nt(qg, k) * ATTN_SCALE + bs_ref[g, :, pl.ds(off, KEY_CHUNK)]
            return _softmax_step(carry, s, selx_ref[:, pl.ds(off, KEY_CHUNK)] > 0.5, v)

        carry = lax.fori_loop(0, past // KEY_CHUNK, slc_step, _softmax_init(rows))
        s = _dot_nt(qg, new_rows(ksn_ref)) * ATTN_SCALE + bias_new
        carry = _softmax_step(carry, s, new_mask & (sel4[:, nbp:nbp + 1] > 0.5), new_rows(vsn_ref))
        o_slc = _softmax_done(carry)

        s = _dot_nt(qg, cwk_ref[0][:, cols].astype(BF16)) * ATTN_SCALE + bw_ref[g]
        carry = _softmax_step(_softmax_init(rows), s, win_mask, cwv_ref[0][:, cols].astype(BF16))
        s = _dot_nt(qg, new_rows(kwn_ref)) * ATTN_SCALE + bias_new
        carry = _softmax_step(carry, s, new_mask, new_rows(vwn_ref))
        o_win = _softmax_done(carry)

        heads = [g * GROUP + r for r in range(GROUP)]
        o = (_stack_cols(gates, heads) * ocmp_ref[0, g]
             + _stack_cols(gates, [N_HEADS_NSA + h for h in heads]) * o_slc
             + _stack_cols(gates, [2 * N_HEADS_NSA + h for h in heads]) * o_win)
        for r, h in enumerate(heads):
            o_ref[0, :, h * HEAD_DIM:(h + 1) * HEAD_DIM] = o[r * t:(r + 1) * t]


def _nsa_sample_attn(page_table, q3, gate3, o_cmp, sel, ksn, vsn, kwn, vwn, cwk, cwv, pool_k, pool_v,
                     bias_slc, bias_new, bias_win, expand):
    n, n_pages = page_table.shape
    t = q3.shape[1]
    past = n_pages * PAGE_SIZE
    wb = cwk.shape[1]
    width = sel.shape[3]
    assert past % KEY_CHUNK == 0 and t <= NEW_PAD
    per_seq = lambda shape: pl.BlockSpec((1,) + shape, lambda i, pt: (i,) + (0,) * len(shape))
    full = lambda a: pl.BlockSpec(a.shape, lambda i, pt: (0,) * a.ndim)
    anyspec = pl.BlockSpec(memory_space=pl.ANY)
    grid_spec = pltpu.PrefetchScalarGridSpec(
        num_scalar_prefetch=1,
        grid=(n,),
        in_specs=[per_seq((t, NSA_WIDTH)), per_seq((t, LANES)), per_seq((N_KV_NSA, GROUP * t, HEAD_DIM)),
                  per_seq((N_KV_NSA, t, width)),
                  per_seq((t, KV_WIDTH)), per_seq((t, KV_WIDTH)), per_seq((t, KV_WIDTH)), per_seq((t, KV_WIDTH)),
                  per_seq((wb, KV_WIDTH)), per_seq((wb, KV_WIDTH)), anyspec, anyspec,
                  full(bias_slc), full(bias_new), full(bias_win), full(expand)],
        out_specs=per_seq((t, NSA_WIDTH)),
        scratch_shapes=[pltpu.VMEM((2, past, KV_WIDTH), F32), pltpu.VMEM((2, past, KV_WIDTH), F32),
                        pltpu.SemaphoreType.DMA((2,)), pltpu.SemaphoreType.DMA((2,)),
                        pltpu.VMEM((GROUP * t, past), F32)],
    )
    return pl.pallas_call(
        functools.partial(_nsa_sample_attn_kernel, n_pages=n_pages),
        out_shape=jax.ShapeDtypeStruct((n, t, NSA_WIDTH), F32),
        grid_spec=grid_spec,
        compiler_params=_cparams(("arbitrary",)),
        name="nsa_sample_attn",
    )(page_table, q3, gate3, o_cmp, sel, ksn, vsn, kwn, vwn, cwk, cwv, pool_k, pool_v,
      bias_slc, bias_new, bias_win, expand)


def _segsum(x, ones_ref):
    hi = x.astype(BF16)
    lo = (x - hi.astype(F32)).astype(BF16)
    return _dot(hi, ones_ref[...]) + _dot(lo, ones_ref[...])


def _softplus(x):
    return jnp.maximum(x, 0.0) + jnp.log(1.0 + jnp.exp(-jnp.abs(x)))


def _rwkv_pre_kernel(p_ref, prev_ref, mu_ref, w0_ref, w2_ref, a0_ref, a2_ref, g2_ref, kk_ref, ka_ref, ones_ref,
                     r_ref, k_ref, v_ref, kkn_ref, kka_ref, d_ref, g_ref, last_ref, buf_ref):
    ti = pl.program_id(1)
    tt = p_ref.shape[1]
    p = p_ref[0]
    first = jnp.where(ti == 0, prev_ref[0], last_ref[...])
    last_ref[...] = p[tt - 1:tt]
    buf_ref[SUBLANES - 1:SUBLANES, :] = first
    buf_ref[SUBLANES:SUBLANES + tt, :] = p
    prev = buf_ref[SUBLANES - 1:SUBLANES - 1 + tt, :]
    xs = p + (prev - p) * mu_ref[...]
    W = RWKV_WIDTH
    r, k, v = xs[:, 0:W], xs[:, W:2 * W], xs[:, 2 * W:3 * W]
    c = 3 * W
    xw = xs[:, c:c + DECAY_LORA]
    xa = xs[:, c + DECAY_LORA:c + DECAY_LORA + AAA_LORA]
    xg = xs[:, c + DECAY_LORA + AAA_LORA:]
    w = -_softplus(-(w0_ref[...] + _dot(jnp.tanh(xw).astype(BF16), w2_ref[...]))) - 0.5
    a = jax.nn.sigmoid(a0_ref[...] + _dot(xa.astype(BF16), a2_ref[...]))
    kk = k * kk_ref[...]
    kk = kk / jnp.maximum(jnp.sqrt(_segsum(kk * kk, ones_ref)), 1e-12)
    r_ref[0] = r
    k_ref[0] = k * (1.0 + (a - 1.0) * ka_ref[...])
    v_ref[0] = v
    kkn_ref[0] = kk
    kka_ref[0] = kk * a
    d_ref[0] = -jnp.exp(w)
    g_ref[0] = _dot(jax.nn.sigmoid(xg).astype(BF16), g2_ref[...])


def _rwkv_pre(p3, prev, mu, w0, w2, a0, a2, g2, k_k, k_a, ones_bd):
    n, t, c = p3.shape
    tt = min(t, 256)
    assert t % tt == 0
    row = lambda a: a.reshape(1, -1)
    full = lambda a: pl.BlockSpec(a.shape, lambda b, i: (0,) * a.ndim)
    params = [row(mu), row(w0), w2, row(a0), a2, g2, row(k_k), row(k_a), ones_bd]
    out_spec = pl.BlockSpec((1, tt, RWKV_WIDTH), lambda b, i: (b, i, 0))
    return pl.pallas_call(
        _rwkv_pre_kernel,
        out_shape=[jax.ShapeDtypeStruct((n, t, RWKV_WIDTH), F32)] * 7,
        grid=(n, t // tt),
        in_specs=[pl.BlockSpec((1, tt, c), lambda b, i: (b, i, 0)),
                  pl.BlockSpec((1, 1, c), lambda b, i: (b, 0, 0))] + [full(a) for a in params],
        out_specs=[out_spec] * 7,
        scratch_shapes=[pltpu.VMEM((1, c), F32), pltpu.VMEM((SUBLANES + tt, c), F32)],
        compiler_params=_cparams(("parallel", "arbitrary")),
        name="rwkv_pre",
    )(p3, prev.reshape(n, 1, c), *params)


def _rwkv_scan_kernel(r_ref, k_ref, v_ref, kk_ref, kka_ref, ld_ref, s0_ref, y_ref, sout_ref, s_ref):
    ci = pl.program_id(1)
    t_in = r_ref.shape[1]
    C = TC

    @pl.when(ci == 0)
    def _():
        s_ref[...] = s0_ref[0]

    def rows(ref):
        x = ref[0]
        if t_in < C:
            x = jnp.concatenate([x, jnp.zeros((C - t_in, x.shape[1]), F32)], axis=0)
        return x

    ld = rows(ld_ref)
    tri = jnp.where(lax.broadcasted_iota(jnp.int32, (C, C), 1) <= lax.broadcasted_iota(jnp.int32, (C, C), 0),
                    1.0, 0.0).astype(BF16)
    hi = ld.astype(BF16)
    rem = ld - hi.astype(F32)
    mid = rem.astype(BF16)
    lo = (rem - mid.astype(F32)).astype(BF16)
    cum = _dot(tri, hi) + _dot(tri, mid) + _dot(tri, lo)
    p_inc = jnp.exp(cum)
    p_inv = jnp.exp(-cum)
    a_t = -rows(kk_ref) * jnp.exp(cum - ld)
    b_t = rows(kka_ref) * p_inv
    k_t = rows(k_ref) * p_inv
    r_t = rows(r_ref) * p_inc
    v = rows(v_ref)
    p_end = p_inc[C - 1:C]

    i2 = lax.broadcasted_iota(jnp.int32, (2 * C, 2 * C), 0)
    j2 = lax.broadcasted_iota(jnp.int32, (2 * C, 2 * C), 1)
    src = j2 & (C - 1)
    keep = src < jnp.where(i2 < C, i2, i2 - C + 1)

    heads = range(N_HEADS_RWKV)
    cols = [slice(h * HEAD_DIM, (h + 1) * HEAD_DIM) for h in heads]
    ar = [jnp.concatenate([a_t[:, c], r_t[:, c]], axis=0).astype(BF16) for c in cols]
    bk = [jnp.concatenate([b_t[:, c], k_t[:, c]], axis=0).astype(BF16) for c in cols]
    vh = [v[:, c].astype(BF16) for c in cols]
    s0 = [s_ref[h] for h in heads]
    quad = [jnp.where(keep, _dot_nt(ar[h], bk[h]), 0.0).astype(BF16) for h in heads]
    base = [_dot_nt(ar[h], s0[h].astype(BF16)) for h in heads]
    x = [base[h][:C] + _dot(quad[h][:C, C:], vh[h]) for h in heads]
    power = [quad[h][:C, :C] for h in heads]
    levels = C.bit_length() - 1
    for lvl in range(levels):
        x = [x[h] + _dot(power[h], x[h].astype(BF16)) for h in heads]
        if lvl + 1 < levels:
            power = [_dot(power[h], power[h]).astype(BF16) for h in heads]
    uv = [jnp.concatenate([x[h].astype(BF16), vh[h]], axis=0) for h in heads]
    ys = [base[h][C:] + _dot(quad[h][C:, :], uv[h]) for h in heads]
    for h in heads:
        upd = lax.dot_general(uv[h], bk[h], (((0,), (0,)), ((), ())), preferred_element_type=F32)
        s_ref[h] = (s0[h] + upd) * p_end[:, cols[h]]
    y = jnp.concatenate(ys, axis=1)
    y_ref[0] = y[:t_in]

    @pl.when(ci == pl.num_programs(1) - 1)
    def _():
        sout_ref[0] = s_ref[...]


def _rwkv_scan(r, k, v, kk, kka, d, s0):
    n, t, w = r.shape
    tc = min(t, TC)
    assert t % tc == 0 and TC & (TC - 1) == 0
    seq_spec = pl.BlockSpec((1, tc, w), lambda b, i: (b, i, 0))
    st_spec = pl.BlockSpec((1, N_HEADS_RWKV, HEAD_DIM, HEAD_DIM), lambda b, i: (b, 0, 0, 0))
    return pl.pallas_call(
        _rwkv_scan_kernel,
        out_shape=[jax.ShapeDtypeStruct((n, t, w), F32),
                   jax.ShapeDtypeStruct((n, N_HEADS_RWKV, HEAD_DIM, HEAD_DIM), F32)],
        grid=(n, t // tc),
        in_specs=[seq_spec] * 6 + [st_spec],
        out_specs=[seq_spec, st_spec],
        scratch_shapes=[pltpu.VMEM((N_HEADS_RWKV, HEAD_DIM, HEAD_DIM), F32)],
        compiler_params=_cparams(("parallel", "arbitrary")),
        name="rwkv_scan",
    )(r, k, v, kk, kka, d, s0)


def _mix_out_kernel(x_ref, nsa_ref, y_ref, r_ref, k_ref, v_ref, g_ref, rk_ref, lnw_ref, lnb_ref, ones_ref,
                    wa_ref, wb_ref, o_ref):
    y = y_ref[...]
    mean = _segsum(y, ones_ref) * (1.0 / HEAD_DIM)
    yc = y - mean
    var = _segsum(yc * yc, ones_ref) * (1.0 / HEAD_DIM)
    yn = yc * lax.rsqrt(var + GN_EPS) * lnw_ref[...] + lnb_ref[...]
    bonus = _segsum(r_ref[...] * k_ref[...] * rk_ref[...], ones_ref) * v_ref[...]
    rw = ((yn + bonus) * g_ref[...]).astype(BF16)
    o_ref[...] = x_ref[...] + _dot(nsa_ref[...].astype(BF16), wa_ref[...]) + _dot(rw, wb_ref[...])


def _mix_out(x, o_nsa, y, r, k, v, g, r_k, ln_w, ln_b, ones_bd, wa, wb):
    m, d = x.shape
    tm = min(m, 512)
    assert m % tm == 0
    row = lambda a: a.reshape(1, -1)
    tile = lambda w: pl.BlockSpec((tm, w), lambda i: (i, 0))
    full = lambda a: pl.BlockSpec(a.shape, lambda i: (0,) * a.ndim)
    params = [row(r_k), row(ln_w), row(ln_b), ones_bd, wa, wb]
    return pl.pallas_call(
        _mix_out_kernel,
        out_shape=jax.ShapeDtypeStruct((m, d), F32),
        grid=(m // tm,),
        in_specs=[tile(d), tile(NSA_WIDTH)] + [tile(RWKV_WIDTH)] * 5 + [full(a) for a in params],
        out_specs=tile(d),
        compiler_params=_cparams(("parallel",)),
        name="mix_out",
    )(x, o_nsa, y, r, k, v, g, *params)


def _block_diag2(w):
    z = jnp.zeros_like(w)
    return jnp.concatenate([jnp.concatenate([w, z], axis=-1), jnp.concatenate([z, w], axis=-1)], axis=-2)


def _compress_weights(pe, w1, w2):
    return (jnp.concatenate([pe] * N_KV_NSA, axis=1), _block_diag2(w1).astype(BF16), _block_diag2(w2).astype(BF16))


def kernel(x_prompt, x_sample, cache_cmp_k, cache_cmp_v, cache_slc_k, cache_slc_v, cache_win_k, cache_win_v,
           state_wkv, state_shift, page_table, rel_bias_table, ffn1_norm, ffn1_wg, ffn1_wu, ffn1_wd, mix_norm,
           w_in, cmp_pe_k, cmp_w1_k, cmp_w2_k, cmp_pe_v, cmp_w1_v, cmp_w2_v, shift_mu, decay_w0, decay_w2,
           aaa_a0, aaa_a2, gate_g2, k_k, k_a, r_k, ln_x_w, ln_x_b, w_out, ffn2_norm, ffn2_wg, ffn2_wu, ffn2_wd,
           final_norm):
    bp, seq, d = x_prompt.shape
    bs, ts, _ = x_sample.shape
    depth = w_in.shape[0]
    n_pages = page_table.shape[1]
    past = n_pages * PAGE_SIZE
    n_phys = cache_cmp_k.shape[1]
    wbuf = cache_win_k.shape[2]
    assert seq % QT == 0 and seq >= WINDOW and ts < CMP_BLOCK and wbuf == WINDOW
    nq = seq // QT
    nb = seq // SLC_BLOCK
    nbp = past // SLC_BLOCK
    nsa_main = NSA_WIDTH + 6 * KV_WIDTH
    nsa_cols = nsa_main + N_GATES

    bias_tiles = _bias_tiles(rel_bias_table)
    bias_cmp_p = _bias_rows(rel_bias_table, q0=0, rows=seq, n_keys=nb, key_first=CMP_BLOCK - 1, key_stride=CMP_BLOCK)
    bias_cmp_p = bias_cmp_p.reshape(N_KV_NSA, GROUP, nq, QT, nb).transpose(0, 2, 4, 1, 3).reshape(
        N_KV_NSA, nq, nb, GROUP * QT)
    stacked = lambda b: b.reshape(N_KV_NSA, GROUP * ts, b.shape[-1])
    bias_cmp_s = stacked(_bias_rows(rel_bias_table, q0=past, rows=ts, n_keys=nbp,
                                    key_first=CMP_BLOCK - 1, key_stride=CMP_BLOCK))
    bias_slc_s = stacked(_bias_rows(rel_bias_table, q0=past, rows=ts, n_keys=past, key_first=0, key_stride=1))
    bias_new_s = stacked(_bias_rows(rel_bias_table, q0=past, rows=ts, n_keys=LANES, key_first=past, key_stride=1))
    bias_win_s = stacked(_bias_rows(rel_bias_table, q0=past, rows=ts, n_keys=wbuf,
                                    key_first=past - wbuf, key_stride=1))
    expand_s = (jnp.arange(past, dtype=jnp.int32)[None, :] // SLC_BLOCK
                == jnp.arange(nbp, dtype=jnp.int32)[:, None]).astype(BF16)
    head_of = jnp.arange(RWKV_WIDTH, dtype=jnp.int32) // HEAD_DIM
    ones_bd = (head_of[:, None] == head_of[None, :]).astype(BF16)

    xp = x_prompt.reshape(bp * seq, d)
    xs = x_sample.reshape(bs * ts, d)
    states = []
    for l in range(depth):
        bf = lambda a: a[l].astype(BF16)
        w_in_l = w_in[l]
        w_pad = jnp.concatenate(
            [w_in_l[:, :nsa_main], w_in_l[:, nsa_main:nsa_cols], jnp.zeros((d, LANES - N_GATES), F32),
             w_in_l[:, nsa_cols:]], axis=1).astype(BF16)
        cw_k = _compress_weights(cmp_pe_k[l], cmp_w1_k[l], cmp_w2_k[l])
        cw_v = _compress_weights(cmp_pe_v[l], cmp_w1_v[l], cmp_w2_v[l])
        wa, wb = w_out[l][:NSA_WIDTH].astype(BF16), w_out[l][NSA_WIDTH:].astype(BF16)
        rw_pre = (shift_mu[l], decay_w0[l], bf(decay_w2), aaa_a0[l], bf(aaa_a2), bf(gate_g2), k_k[l], k_a[l], ones_bd)
        ffn1 = (ffn1_norm[l], bf(ffn1_wg), bf(ffn1_wu), bf(ffn1_wd), final_norm)
        ffn2 = (ffn2_norm[l], bf(ffn2_wg), bf(ffn2_wu), bf(ffn2_wd), final_norm)
        last = l == depth - 1

        xp = _ffn_half(xp, *ffn1, final=False)
        xs = _ffn_half(xs, *ffn1, final=False)

        q, kc, vc, ks, vs, kw, vw, gate, zr = _proj(xp, mix_norm[l], w_pad)
        kcc = _compress(kc, *cw_k)
        vcc = _compress(vc, *cw_v)
        o_nsa = _nsa_prompt(q, gate, ks, vs, kw, vw, kcc, vcc, bias_tiles, bias_cmp_p, bp, seq)
        zr3 = zr.reshape(bp, seq, RWKV_COLS)
        r, k2, v, kk, kka, dec, gg = _rwkv_pre(zr3, jnp.zeros((bp, RWKV_COLS), F32), *rw_pre)
        y, wkv_p = _rwkv_scan(r, k2, v, kk, kka, dec,
                              jnp.zeros((bp, N_HEADS_RWKV, HEAD_DIM, HEAD_DIM), F32))
        flat = lambda a: a.reshape(-1, RWKV_WIDTH)
        xp = _mix_out(xp, o_nsa, flat(y), flat(r), flat(k2), flat(v), flat(gg), r_k[l], ln_x_w[l], ln_x_b[l],
                      ones_bd, wa, wb)
        kv5 = lambda a, n_, t_: a.reshape(n_, t_, N_KV_NSA, HEAD_DIM)
        p_state = (kv5(kc, bp, seq), kv5(vc, bp, seq), kv5(ks, bp, seq), kv5(vs, bp, seq),
                   kv5(kw, bp, seq)[:, seq - WINDOW:], kv5(vw, bp, seq)[:, seq - WINDOW:],
                   wkv_p, zr3[:, seq - 1])

        q, kc, vc, ks, vs, kw, vw, gate, zr = _proj(xs, mix_norm[l], w_pad)
        pool = lambda c: c[l].reshape(n_phys, PAGE_SIZE, KV_WIDTH)
        kcc, vcc = _compress_paged(page_table, pool(cache_cmp_k), pool(cache_cmp_v), cw_k, cw_v)
        seq3 = lambda a: a.reshape(bs, ts, a.shape[-1])
        o_cmp, sel = _nsa_sample_cmp(seq3(q), kcc, vcc, bias_cmp_s, past)
        cwk = cache_win_k[l].reshape(bs, wbuf, KV_WIDTH)
        cwv = cache_win_v[l].reshape(bs, wbuf, KV_WIDTH)
        o_nsa = _nsa_sample_attn(page_table, seq3(q), seq3(gate), o_cmp, sel, seq3(ks), seq3(vs), seq3(kw),
                                 seq3(vw), cwk, cwv, pool(cache_slc_k), pool(cache_slc_v),
                                 bias_slc_s, bias_new_s, bias_win_s, expand_s)
        zr3 = seq3(zr)
        r, k2, v, kk, kka, dec, gg = _rwkv_pre(zr3, state_shift[l], *rw_pre)
        y, wkv_s = _rwkv_scan(r, k2, v, kk, kka, dec, state_wkv[l])
        xs = _mix_out(xs, o_nsa.reshape(bs * ts, NSA_WIDTH), flat(y), flat(r), flat(k2), flat(v), flat(gg),
                      r_k[l], ln_x_w[l], ln_x_b[l], ones_bd, wa, wb)
        win = lambda cache, new: jnp.concatenate([cache[l], kv5(new, bs, ts)], axis=1)[:, ts:]
        s_state = (kv5(kc, bs, ts), kv5(vc, bs, ts), kv5(ks, bs, ts), kv5(vs, bs, ts),
                   win(cache_win_k, kw), win(cache_win_v, vw), wkv_s, zr3[:, ts - 1])

        xp = _ffn_half(xp, *ffn2, final=last)
        xs = _ffn_half(xs, *ffn2, final=last)
        states.append(p_state + s_state)

    stacked_states = [jnp.stack(z) for z in zip(*states)]
    return (xp.reshape(bp, seq, d), xs.reshape(bs, ts, d), *stacked_states)
```

```python
import functools
import math

import jax
import jax.numpy as jnp
import numpy as np
from jax import lax
from jax.experimental import pallas as pl
from jax.experimental.pallas import tpu as pltpu

F32 = jnp.float32
BF16 = jnp.bfloat16

HEAD_DIM = 64
N_HEADS_NSA = 8
N_KV_NSA = 2
GROUP = N_HEADS_NSA // N_KV_NSA
N_HEADS_RWKV = 8
NSA_WIDTH = N_HEADS_NSA * HEAD_DIM
RWKV_WIDTH = N_HEADS_RWKV * HEAD_DIM
KV_WIDTH = N_KV_NSA * HEAD_DIM
CMP_BLOCK = 64
SLC_BLOCK = 64
TOP_K_BLOCKS = 16
WINDOW = 512
PAGE_SIZE = 128
N_BUCKETS = 32
MAX_DISTANCE = 128
DECAY_LORA = 64
AAA_LORA = 64
GATE_LORA = 128
NORM_EPS = 1e-6
GN_EPS = 64e-5
ATTN_SCALE = HEAD_DIM ** -0.5
FORCE_SCORE = 1e4
NEG_INF = -1e30
M_FLOOR = -1e29
RWKV_COLS = 3 * RWKV_WIDTH + DECAY_LORA + AAA_LORA + GATE_LORA
N_GATES = 3 * N_HEADS_NSA

LANES = 128
SUBLANES = 8
VMEM_LIMIT = 56 * 1024 * 1024

QT = 128
TC = 64


def _bucket_thresholds():
    max_exact = N_BUCKETS // 2
    thr = list(range(max_exact + 1))
    for b in range(max_exact + 1, N_BUCKETS):
        x = max_exact * (MAX_DISTANCE / max_exact) ** ((b - max_exact) / (N_BUCKETS - max_exact))
        thr.append(int(math.ceil(x - 1e-9)))
    return thr


BUCKET_THR = _bucket_thresholds()


def _cparams(sem):
    return pltpu.CompilerParams(dimension_semantics=sem, vmem_limit_bytes=VMEM_LIMIT)


def _dot(a, b):
    return jnp.dot(a, b, preferred_element_type=F32)


def _dot_nt(a, b):
    return lax.dot_general(a, b, (((1,), (1,)), ((), ())), preferred_element_type=F32)


def _rms(x, g):
    return x * lax.rsqrt(jnp.mean(x * x, axis=-1, keepdims=True) + NORM_EPS) * g


def _ffn_kernel(x_ref, g_ref, wg_ref, wu_ref, wd_ref, gf_ref, o_ref, h_ref, acc_ref, *, final):
    j = pl.program_id(1)

    @pl.when(j == 0)
    def _():
        h_ref[...] = _rms(x_ref[...], g_ref[...]).astype(BF16)
        acc_ref[...] = jnp.zeros_like(acc_ref)

    h = h_ref[...]
    a = _dot(h, wg_ref[...])
    b = _dot(h, wu_ref[...])
    act = (a * jax.nn.sigmoid(a) * b).astype(BF16)
    acc_ref[...] += _dot(act, wd_ref[...])

    @pl.when(j == pl.num_programs(1) - 1)
    def _():
        y = x_ref[...] + 0.5 * acc_ref[...]
        if final:
            y = _rms(y, gf_ref[...])
        o_ref[...] = y


def _ffn_half(x, g, wg, wu, wd, gf, *, final):
    m, d = x.shape
    ff = wg.shape[1]
    tm = min(m, 1024)
    tf = 256 if ff % 256 == 0 else ff
    assert m % tm == 0 and ff % tf == 0
    return pl.pallas_call(
        functools.partial(_ffn_kernel, final=final),
        out_shape=jax.ShapeDtypeStruct((m, d), F32),
        grid=(m // tm, ff // tf),
        in_specs=[
            pl.BlockSpec((tm, d), lambda i, j: (i, 0)),
            pl.BlockSpec((1, d), lambda i, j: (0, 0)),
            pl.BlockSpec((d, tf), lambda i, j: (0, j)),
            pl.BlockSpec((d, tf), lambda i, j: (0, j)),
            pl.BlockSpec((tf, d), lambda i, j: (j, 0)),
            pl.BlockSpec((1, d), lambda i, j: (0, 0)),
        ],
        out_specs=pl.BlockSpec((tm, d), lambda i, j: (i, 0)),
        scratch_shapes=[pltpu.VMEM((tm, d), BF16), pltpu.VMEM((tm, d), F32)],
        compiler_params=_cparams(("parallel", "arbitrary")),
        name="ffn_half",
    )(x, g.reshape(1, d), wg, wu, wd, gf.reshape(1, d))


def _proj_kernel(x_ref, g_ref, w_ref, q_ref, kc_ref, vc_ref, ks_ref, vs_ref, kw_ref, vw_ref, gate_ref, zr_ref):
    h = _rms(x_ref[...], g_ref[...]).astype(BF16)
    z = _dot(h, w_ref[...])
    c = 0
    for ref in (q_ref, kc_ref, vc_ref, ks_ref, vs_ref, kw_ref, vw_ref, gate_ref, zr_ref):
        w = ref.shape[1]
        ref[...] = z[:, c:c + w]
        c += w


def _proj(x, g, w_pad):
    m, d = x.shape
    tm = min(m, 512)
    widths = (NSA_WIDTH,) + (KV_WIDTH,) * 6 + (LANES, RWKV_COLS)
    assert sum(widths) == w_pad.shape[1] and m % tm == 0
    return pl.pallas_call(
        _proj_kernel,
        out_shape=[jax.ShapeDtypeStruct((m, w), F32) for w in widths],
        grid=(m // tm,),
        in_specs=[
            pl.BlockSpec((tm, d), lambda i: (i, 0)),
            pl.BlockSpec((1, d), lambda i: (0, 0)),
            pl.BlockSpec(w_pad.shape, lambda i: (0, 0)),
        ],
        out_specs=[pl.BlockSpec((tm, w), lambda i: (i, 0)) for w in widths],
        compiler_params=_cparams(("parallel",)),
        name="in_proj",
    )(x, g.reshape(1, d), w_pad)


def _out_kernel(x_ref, a_ref, b_ref, wa_ref, wb_ref, o_ref):
    o_ref[...] = (x_ref[...] + _dot(a_ref[...].astype(BF16), wa_ref[...])
                  + _dot(b_ref[...].astype(BF16), wb_ref[...]))


def _out_proj(x, a, b, wa, wb):
    m, d = x.shape
    tm = min(m, 512)
    return pl.pallas_call(
        _out_kernel,
        out_shape=jax.ShapeDtypeStruct((m, d), F32),
        grid=(m // tm,),
        in_specs=[
            pl.BlockSpec((tm, d), lambda i: (i, 0)),
            pl.BlockSpec((tm, a.shape[1]), lambda i: (i, 0)),
            pl.BlockSpec((tm, b.shape[1]), lambda i: (i, 0)),
            pl.BlockSpec(wa.shape, lambda i: (0, 0)),
            pl.BlockSpec(wb.shape, lambda i: (0, 0)),
        ],
        out_specs=pl.BlockSpec((tm, d), lambda i: (i, 0)),
        compiler_params=_cparams(("parallel",)),
        name="out_proj",
    )(x, a, b, wa, wb)


def _bias_lookup(dist, table_ref, heads):
    vals = [jnp.full(dist.shape, table_ref[0, h], F32) for h in heads]
    for b in range(1, N_BUCKETS):
        ge = dist >= BUCKET_THR[b]
        vals = [jnp.where(ge, table_ref[b, h], v) for h, v in zip(heads, vals)]
    return vals


def _bias_tiles_kernel(table_ref, o_ref):
    j = lax.broadcasted_iota(jnp.int32, (QT, QT), 0)
    i = lax.broadcasted_iota(jnp.int32, (QT, QT), 1)
    for off in range(3):
        vals = _bias_lookup(off * QT + i - j, table_ref, range(N_HEADS_NSA))
        for h, v in enumerate(vals):
            g, r = divmod(h, GROUP)
            o_ref[g, off, :, r * QT:(r + 1) * QT] = v


def _bias_tiles(table):
    return pl.pallas_call(
        _bias_tiles_kernel,
        out_shape=jax.ShapeDtypeStruct((N_KV_NSA, 3, QT, GROUP * QT), F32),
        in_specs=[pl.BlockSpec(memory_space=pltpu.SMEM)],
        name="bias_tiles",
    )(table)


def _bias_rows_kernel(table_ref, o_ref, *, q0, key_first, key_stride):
    _, _, rc, L = o_ref.shape
    i = lax.broadcasted_iota(jnp.int32, (rc, L), 0) + pl.program_id(0) * rc
    j = lax.broadcasted_iota(jnp.int32, (rc, L), 1)
    dist = (q0 + i) - (key_first + key_stride * j)
    vals = _bias_lookup(dist, table_ref, range(N_HEADS_NSA))
    for h, v in enumerate(vals):
        g, r = divmod(h, GROUP)
        o_ref[g, r] = v


def _bias_rows(table, *, q0, rows, n_keys, key_first, key_stride):
    rc = min(rows, 256)
    assert rows % rc == 0
    return pl.pallas_call(
        functools.partial(_bias_rows_kernel, q0=q0, key_first=key_first, key_stride=key_stride),
        out_shape=jax.ShapeDtypeStruct((N_KV_NSA, GROUP, rows, n_keys), F32),
        grid=(rows // rc,),
        in_specs=[pl.BlockSpec(memory_space=pltpu.SMEM)],
        out_specs=pl.BlockSpec((N_KV_NSA, GROUP, rc, n_keys), lambda i: (0, 0, i, 0)),
        compiler_params=_cparams(("parallel",)),
        name="bias_rows",
    )(table)


def _gelu_tanh(x):
    return 0.5 * x * (1.0 + jnp.tanh(math.sqrt(2.0 / math.pi) * (x + 0.044715 * (x * x * x))))


def _compress_blocks(load_rows, nblk, pe_ref, w1_ref, w2_ref):
    acc = jnp.zeros((nblk, KV_WIDTH), F32)
    for c in range(CMP_BLOCK):
        rows = load_rows(c) + pe_ref[c:c + 1, :]
        acc = acc + _dot(rows.astype(BF16), w1_ref[c])
    return _dot(_gelu_tanh(acc).astype(BF16), w2_ref[...])


def _compress_kernel(x_ref, pe_ref, w1_ref, w2_ref, o_ref):
    nblk = o_ref.shape[0]
    o_ref[...] = _compress_blocks(lambda c: x_ref[pl.ds(c, nblk, stride=CMP_BLOCK), :],
                                  nblk, pe_ref, w1_ref, w2_ref)


def _compress(x, pe2, w1bd, w2bd):
    rows = x.shape[0]
    nb = rows // CMP_BLOCK
    nblk = min(nb, 128)
    assert nb % nblk == 0
    return pl.pallas_call(
        _compress_kernel,
        out_shape=jax.ShapeDtypeStruct((nb, KV_WIDTH), F32),
        grid=(nb // nblk,),
        in_specs=[
            pl.BlockSpec((nblk * CMP_BLOCK, KV_WIDTH), lambda i: (i, 0)),
            pl.BlockSpec(pe2.shape, lambda i: (0, 0)),
            pl.BlockSpec(w1bd.shape, lambda i: (0, 0, 0)),
            pl.BlockSpec(w2bd.shape, lambda i: (0, 0)),
        ],
        out_specs=pl.BlockSpec((nblk, KV_WIDTH), lambda i: (i, 0)),
        compiler_params=_cparams(("parallel",)),
        name="compress",
    )(x, pe2, w1bd, w2bd)


def _page_copies(pt_ref, n, pool_ref, buf_ref, sem_ref, slot, n_pages):
    return [pltpu.make_async_copy(pool_ref.at[pt_ref[n, p]],
                                  buf_ref.at[slot, pl.ds(p * PAGE_SIZE, PAGE_SIZE)],
                                  sem_ref.at[slot]) for p in range(n_pages)]


def _page_copies_by_row(pt_ref, n, pool_ref, buf_ref, sem_ref, slot, n_pages):
    per_page = PAGE_SIZE // CMP_BLOCK
    return [pltpu.make_async_copy(pool_ref.at[pt_ref[n, p], pl.ds(j * CMP_BLOCK, CMP_BLOCK)],
                                  buf_ref.at[slot, :, p * per_page + j],
                                  sem_ref.at[slot])
            for p in range(n_pages) for j in range(per_page)]


def _paged_prefetch(pt_ref, pools, bufs, sems, n_pages, _page_copies=_page_copies):
    n = pl.program_id(0)
    slot = n % 2

    def start(seq, slot_):
        for pool, buf, sem in zip(pools, bufs, sems):
            for cp in _page_copies(pt_ref, seq, pool, buf, sem, slot_, n_pages):
                cp.start()

    @pl.when(n == 0)
    def _():
        start(0, 0)

    @pl.when(n + 1 < pl.num_programs(0))
    def _():
        start(n + 1, 1 - slot)

    for pool, buf, sem in zip(pools, bufs, sems):
        for cp in _page_copies(pt_ref, n, pool, buf, sem, slot, n_pages):
            cp.wait()
    return slot


def _compress_paged_kernel(pt_ref, pk_ref, pv_ref, pek_ref, w1k_ref, w2k_ref, pev_ref, w1v_ref, w2v_ref,
                           ok_ref, ov_ref, bk_ref, bv_ref, sk_ref, sv_ref, *, n_pages):
    slot = _paged_prefetch(pt_ref, (pk_ref, pv_ref), (bk_ref, bv_ref), (sk_ref, sv_ref), n_pages,
                           _page_copies_by_row)
    nblk = ok_ref.shape[1]
    ok_ref[0] = _compress_blocks(lambda c: bk_ref[slot, c], nblk, pek_ref, w1k_ref, w2k_ref)
    ov_ref[0] = _compress_blocks(lambda c: bv_ref[slot, c], nblk, pev_ref, w1v_ref, w2v_ref)


def _compress_paged(page_table, pool_k, pool_v, cw_k, cw_v):
    n, n_pages = page_table.shape
    past = n_pages * PAGE_SIZE
    nblk = past // CMP_BLOCK
    wspec = lambda a: pl.BlockSpec(a.shape, lambda i, pt: (0,) * a.ndim)
    grid_spec = pltpu.PrefetchScalarGridSpec(
        num_scalar_prefetch=1,
        grid=(n,),
        in_specs=[pl.BlockSpec(memory_space=pl.ANY), pl.BlockSpec(memory_space=pl.ANY)]
                 + [wspec(a) for a in cw_k + cw_v],
        out_specs=[pl.BlockSpec((1, nblk, KV_WIDTH), lambda i, pt: (i, 0, 0))] * 2,
        scratch_shapes=[pltpu.VMEM((2, CMP_BLOCK, nblk, KV_WIDTH), F32), pltpu.VMEM((2, CMP_BLOCK, nblk, KV_WIDTH), F32),
                        pltpu.SemaphoreType.DMA((2,)), pltpu.SemaphoreType.DMA((2,))],
    )
    return pl.pallas_call(
        functools.partial(_compress_paged_kernel, n_pages=n_pages),
        out_shape=[jax.ShapeDtypeStruct((n, nblk, KV_WIDTH), F32)] * 2,
        grid_spec=grid_spec,
        compiler_params=_cparams(("arbitrary",)),
        name="compress_paged",
    )(page_table, pool_k, pool_v, *cw_k, *cw_v)


def _stack_heads(x, g):
    return jnp.concatenate(
        [x[:, (g * GROUP + r) * HEAD_DIM:(g * GROUP + r + 1) * HEAD_DIM] for r in range(GROUP)], axis=0)


def _stack_cols(x, cols):
    return jnp.concatenate([x[:, c:c + 1] for c in cols], axis=0)


def _softmax_step(carry, s, mask, v):
    m, l, acc = carry
    s = jnp.where(mask, s, NEG_INF)
    m_new = jnp.maximum(m, jnp.max(s, axis=1, keepdims=True))
    p = jnp.where(mask, jnp.exp(s - m_new), 0.0)
    alpha = jnp.exp(m - m_new)
    l = alpha * l + jnp.sum(p, axis=1, keepdims=True)
    acc = alpha * acc + _dot(p.astype(BF16), v)
    return m_new, l, acc


def _softmax_init(rows, width):
    return (jnp.full((rows, 1), NEG_INF, F32), jnp.zeros((rows, 1), F32), jnp.zeros((rows, width), F32))


def _softmax_done(carry):
    _, l, acc = carry
    return acc / jnp.maximum(l, 1e-30)


def _topk_mask(imp, cur, n_blocks):
    blk = lax.broadcasted_iota(jnp.int32, imp.shape, 1)
    forced = (blk == 0) | (blk == cur) | (blk == cur - 1)
    score = jnp.where(blk <= cur, jnp.where(forced, FORCE_SCORE, imp), -FORCE_SCORE)
    rank = jnp.zeros(imp.shape, jnp.int32)
    for b in range(n_blocks):
        col = score[:, b:b + 1]
        ahead = (col > score) | ((col == score) & (blk > b))
        rank = rank + jnp.where(ahead, 1, 0)
    return (rank < TOP_K_BLOCKS) & (blk <= cur)


def _topk_mask_t(imp, cur, n_blocks):
    blk = lax.broadcasted_iota(jnp.int32, imp.shape, 0)
    forced = (blk == 0) | (blk == cur) | (blk == cur - 1)
    score = jnp.where(blk <= cur, jnp.where(forced, FORCE_SCORE, imp), -FORCE_SCORE)
    rank = jnp.zeros(imp.shape, jnp.int32)
    for b in range(n_blocks):
        row = score[b:b + 1, :]
        rank = rank + jnp.where(row > score, 1, jnp.where(row == score, jnp.where(blk > b, 1, 0), 0))
    return jnp.where(rank < TOP_K_BLOCKS, jnp.where(blk <= cur, 1.0, 0.0), 0.0)


def _nsa_prompt_kernel(q_ref, gate_ref, ks_ref, vs_ref, kw_ref, vw_ref, kcc_ref, vcc_ref, bt_ref, bc_ref,
                       o_ref, sel_ref, m_ref, acc_ref, pv_ref):
    qi = pl.program_id(1)
    seq = ks_ref.shape[1]
    nb = seq // SLC_BLOCK
    rows = GROUP * QT
    groups = range(N_KV_NSA)
    q = q_ref[...] * ATTN_SCALE
    gates_t = jax.nn.sigmoid(gate_ref[...]).T
    tpos_row = qi * QT + lax.broadcasted_iota(jnp.int32, (1, QT), 1)
    lane = lax.broadcasted_iota(jnp.int32, (1, KV_WIDTH), 1)
    own = [(lane // HEAD_DIM) == g for g in groups]
    zeros = jnp.zeros((rows, HEAD_DIM), F32)
    qg = []
    for g in groups:
        qs = _stack_heads(q, g)
        qg.append(jnp.concatenate([qs, zeros] if g == 0 else [zeros, qs], axis=1).astype(BF16))

    blk = lax.broadcasted_iota(jnp.int32, (nb, 1), 0)
    tpos_stacked = qi * QT + (lax.broadcasted_iota(jnp.int32, (1, rows), 1) & (QT - 1))
    readable = (blk + 1) * CMP_BLOCK - 1 <= tpos_stacked
    kcc = kcc_ref[0].astype(BF16)
    vcc = vcc_ref[0].astype(BF16)
    o_cmp = []
    for g in groups:
        s = jnp.where(readable, _dot_nt(kcc, qg[g]) + bc_ref[g, 0], NEG_INF)
        e = jnp.where(readable, jnp.exp(s - jnp.max(s, axis=0, keepdims=True)), 0.0)
        p = e / jnp.maximum(jnp.sum(e, axis=0, keepdims=True), 1e-30)
        o_cmp.append(lax.dot_general(vcc, p.astype(BF16), (((0,), (0,)), ((), ())), preferred_element_type=F32))
        imp = p[:, 0:QT]
        for r in range(1, GROUP):
            imp = imp + p[:, r * QT:(r + 1) * QT]
        sel_ref[g] = _topk_mask_t(imp, tpos_row // SLC_BLOCK, nb)

    def branch(lo, hi, k_ref, v_ref, penalty):
        def scores(kt):
            k = k_ref[0, pl.ds(pl.multiple_of(kt * QT, QT), QT), :].astype(BF16)
            return tuple(_dot_nt(k, qg[g]) for g in groups)

        def step(kt, s_cur):
            s_next = scores(jnp.minimum(kt + 1, hi - 1))
            off = pl.multiple_of(kt * QT, QT)
            v = v_ref[0, pl.ds(off, QT), :]
            for g in groups:
                m = m_ref[g]
                s = s_cur[g] + bt_ref[g, jnp.minimum(qi - kt, 2)]
                s = s + jnp.concatenate([penalty(g, off)] * GROUP, axis=1)
                m_new = jnp.maximum(m, jnp.max(s, axis=0, keepdims=True))
                p = jnp.exp(s - m_new).astype(BF16)
                v_ext = jnp.where(own[g], v, 1.0).astype(BF16)
                acc_ref[g] = jnp.exp(m - m_new) * (acc_ref[g] + pv_ref[g])
                pv_ref[g] = lax.dot_general(v_ext, p, (((0,), (0,)), ((), ())), preferred_element_type=F32)
                m_ref[g] = m_new
            return s_next

        for g in groups:
            m_ref[g] = jnp.full((1, rows), M_FLOOR, F32)
            acc_ref[g] = jnp.zeros((KV_WIDTH, rows), F32)
            pv_ref[g] = jnp.zeros((KV_WIDTH, rows), F32)
        lax.fori_loop(lo, hi, step, scores(lo))
        out = []
        for g in groups:
            acc = acc_ref[g] + pv_ref[g]
            denom = acc[(1 - g) * HEAD_DIM:(1 - g) * HEAD_DIM + 1, :]
            out.append(acc / jnp.maximum(denom, 1e-30))
        return out

    key_row = lax.broadcasted_iota(jnp.int32, (QT, 1), 0)

    def slc_penalty(g, off):
        causal = jnp.where(tpos_row - (off + key_row) >= 0, 0.0, NEG_INF)
        first = off // SLC_BLOCK
        chosen = jnp.where(key_row < SLC_BLOCK, sel_ref[g, pl.ds(first, 1), :], sel_ref[g, pl.ds(first + 1, 1), :])
        return jnp.where(chosen > 0.5, causal, NEG_INF)

    def win_penalty(g, off):
        dist = tpos_row - (off + key_row)
        return jnp.where(dist >= 0, jnp.where(dist <= WINDOW, 0.0, NEG_INF), NEG_INF)

    slc = branch(0, qi + 1, ks_ref, vs_ref, slc_penalty)
    win = branch(jnp.maximum(qi - WINDOW // QT, 0), qi + 1, kw_ref, vw_ref, win_penalty)

    def gate_row(branch, g):
        first = branch * N_HEADS_NSA + g * GROUP
        return jnp.concatenate([gates_t[first + r:first + r + 1, :] for r in range(GROUP)], axis=1)

    o_t = [gate_row(0, g) * o_cmp[g] + gate_row(1, g) * slc[g] + gate_row(2, g) * win[g] for g in groups]
    feature_row = lax.broadcasted_iota(jnp.int32, (KV_WIDTH, 1), 0)
    o = jnp.where(feature_row < HEAD_DIM, o_t[0], o_t[1]).T
    for g in groups:
        for r in range(GROUP):
            h = g * GROUP + r
            o_ref[:, h * HEAD_DIM:(h + 1) * HEAD_DIM] = o[r * QT:(r + 1) * QT, g * HEAD_DIM:(g + 1) * HEAD_DIM]


def _nsa_prompt(q, gate, ks, vs, kw, vw, kcc, vcc, bias_tiles, bias_cmp, n, seq):
    nq = seq // QT
    nb = seq // SLC_BLOCK
    seq_spec = pl.BlockSpec((1, seq, KV_WIDTH), lambda b, i: (b, 0, 0))
    cmp_spec = pl.BlockSpec((1, nb, KV_WIDTH), lambda b, i: (b, 0, 0))
    return pl.pallas_call(
        _nsa_prompt_kernel,
        out_shape=jax.ShapeDtypeStruct((n * seq, NSA_WIDTH), F32),
        grid=(n, nq),
        in_specs=[
            pl.BlockSpec((QT, NSA_WIDTH), lambda b, i: (b * nq + i, 0)),
            pl.BlockSpec((QT, LANES), lambda b, i: (b * nq + i, 0)),
            seq_spec, seq_spec, seq_spec, seq_spec, cmp_spec, cmp_spec,
            pl.BlockSpec(bias_tiles.shape, lambda b, i: (0, 0, 0, 0)),
            pl.BlockSpec((N_KV_NSA, 1, nb, GROUP * QT), lambda b, i: (0, i, 0, 0)),
        ],
        out_specs=pl.BlockSpec((QT, NSA_WIDTH), lambda b, i: (b * nq + i, 0)),
        scratch_shapes=[pltpu.VMEM((N_KV_NSA, nb, QT), F32),
                        pltpu.VMEM((N_KV_NSA, 1, GROUP * QT), F32),
                        pltpu.VMEM((N_KV_NSA, KV_WIDTH, GROUP * QT), F32),
                        pltpu.VMEM((N_KV_NSA, KV_WIDTH, GROUP * QT), F32)],
        compiler_params=_cparams(("parallel", "arbitrary")),
        name="nsa_prompt",
    )(q, gate, ks.reshape(n, seq, KV_WIDTH), vs.reshape(n, seq, KV_WIDTH), kw.reshape(n, seq, KV_WIDTH),
      vw.reshape(n, seq, KV_WIDTH), kcc.reshape(n, nb, KV_WIDTH), vcc.reshape(n, nb, KV_WIDTH),
      bias_tiles, bias_cmp)


NEW_PAD = SUBLANES


def _row_token(t):
    return jnp.concatenate([lax.broadcasted_iota(jnp.int32, (t, 1), 0)] * GROUP, axis=0)


def _nsa_sample_cmp_kernel(q_ref, kcc_ref, vcc_ref, bc_ref, o_ref, sel_ref, *, past):
    t = q_ref.shape[1]
    nbp = kcc_ref.shape[1]
    width = sel_ref.shape[3]
    q = q_ref[0]
    pos = past + _row_token(t)
    blk_end = (lax.broadcasted_iota(jnp.int32, (1, nbp), 1) + 1) * CMP_BLOCK - 1
    cur = (past + lax.broadcasted_iota(jnp.int32, (t, 1), 0)) // SLC_BLOCK
    for g in range(N_KV_NSA):
        cols = slice(g * HEAD_DIM, (g + 1) * HEAD_DIM)
        qg = _stack_heads(q, g).astype(BF16)
        s = _dot_nt(qg, kcc_ref[0][:, cols].astype(BF16)) * ATTN_SCALE + bc_ref[g]
        mask = blk_end <= pos
        s = jnp.where(mask, s, NEG_INF)
        e = jnp.where(mask, jnp.exp(s - jnp.max(s, axis=1, keepdims=True)), 0.0)
        p = e / jnp.maximum(jnp.sum(e, axis=1, keepdims=True), 1e-30)
        o_ref[0, g] = _dot(p.astype(BF16), vcc_ref[0][:, cols].astype(BF16))
        imp = p[0:t]
        for r in range(1, GROUP):
            imp = imp + p[r * t:(r + 1) * t]
        imp = jnp.concatenate([imp, jnp.zeros((t, width - nbp), F32)], axis=1)
        sel = _topk_mask(imp, cur, nbp + 1)
        sel_ref[0, g] = jnp.where(sel, 1.0, 0.0)


def _nsa_sample_cmp(q3, kcc, vcc, bias_cmp, past):
    n, t, _ = q3.shape
    nbp = kcc.shape[1]
    width = -(-(nbp + 1) // LANES) * LANES
    return pl.pallas_call(
        functools.partial(_nsa_sample_cmp_kernel, past=past),
        out_shape=[jax.ShapeDtypeStruct((n, N_KV_NSA, GROUP * t, HEAD_DIM), F32),
                   jax.ShapeDtypeStruct((n, N_KV_NSA, t, width), F32)],
        grid=(n,),
        in_specs=[
            pl.BlockSpec((1, t, NSA_WIDTH), lambda i: (i, 0, 0)),
            pl.BlockSpec((1, nbp, KV_WIDTH), lambda i: (i, 0, 0)),
            pl.BlockSpec((1, nbp, KV_WIDTH), lambda i: (i, 0, 0)),
            pl.BlockSpec(bias_cmp.shape, lambda i: (0, 0, 0)),
        ],
        out_specs=[pl.BlockSpec((1, N_KV_NSA, GROUP * t, HEAD_DIM), lambda i: (i, 0, 0, 0)),
                   pl.BlockSpec((1, N_KV_NSA, t, width), lambda i: (i, 0, 0, 0))],
        compiler_params=_cparams(("parallel",)),
        name="nsa_sample_cmp",
    )(q3, kcc, vcc, bias_cmp)


KEY_CHUNK = 1024


def _nsa_sample_attn_kernel(pt_ref, q_ref, gate_ref, ocmp_ref, sel_ref, ksn_ref, vsn_ref, kwn_ref, vwn_ref,
                            cwk_ref, cwv_ref, pk_ref, pv_ref, bs_ref, bn_ref, bw_ref, ex_ref,
                            o_ref, bk_ref, bv_ref, sk_ref, sv_ref, selx_ref, *, n_pages):
    slot = _paged_prefetch(pt_ref, (pk_ref, pv_ref), (bk_ref, bv_ref), (sk_ref, sv_ref), n_pages)
    t = q_ref.shape[1]
    rows = N_KV_NSA * GROUP * t
    past = n_pages * PAGE_SIZE
    nbp = past // SLC_BLOCK
    wb = cwk_ref.shape[1]
    groups = range(N_KV_NSA)
    q = q_ref[0] * ATTN_SCALE
    gates = jax.nn.sigmoid(gate_ref[0])
    zeros = jnp.zeros((GROUP * t, HEAD_DIM), F32)

    def own_lanes(x, g):
        return jnp.concatenate([x, zeros] if g == 0 else [zeros, x], axis=1)

    qa = jnp.concatenate([own_lanes(_stack_heads(q, g), g) for g in groups], axis=0).astype(BF16)
    sel = jnp.concatenate([sel_ref[0, g] for g in groups for _ in range(GROUP)], axis=0)
    selx_ref[...] = _dot(sel[:, :nbp].astype(BF16), ex_ref[...])
    tok = jnp.concatenate([_row_token(t)] * N_KV_NSA, axis=0)
    jn = lax.broadcasted_iota(jnp.int32, (1, NEW_PAD), 1)
    new_mask = jn <= tok
    jw = lax.broadcasted_iota(jnp.int32, (1, wb), 1)
    wdist = wb + tok - jw
    win_mask = (wdist >= 0) & (wdist <= WINDOW)
    bias_new = bn_ref[:, :NEW_PAD]
    pad = jnp.zeros((NEW_PAD - t, KV_WIDTH), F32)

    def new_rows(ref):
        return jnp.concatenate([ref[0], pad], axis=0).astype(BF16)

    def slc_step(c, carry):
        off = pl.multiple_of(c * KEY_CHUNK, KEY_CHUNK)
        k = bk_ref[slot, pl.ds(off, KEY_CHUNK), :].astype(BF16)
        v = bv_ref[slot, pl.ds(off, KEY_CHUNK), :].astype(BF16)
        s = _dot_nt(qa, k) + bs_ref[:, pl.ds(off, KEY_CHUNK)]
        return _softmax_step(carry, s, selx_ref[:, pl.ds(off, KEY_CHUNK)] > 0.5, v)

    carry = lax.fori_loop(0, past // KEY_CHUNK, slc_step, _softmax_init(rows, KV_WIDTH))
    s = _dot_nt(qa, new_rows(ksn_ref)) + bias_new
    carry = _softmax_step(carry, s, new_mask & (sel[:, nbp:nbp + 1] > 0.5), new_rows(vsn_ref))
    o_slc = _softmax_done(carry)

    s = _dot_nt(qa, cwk_ref[0].astype(BF16)) + bw_ref[...]
    carry = _softmax_step(_softmax_init(rows, KV_WIDTH), s, win_mask, cwv_ref[0].astype(BF16))
    s = _dot_nt(qa, new_rows(kwn_ref)) + bias_new
    carry = _softmax_step(carry, s, new_mask, new_rows(vwn_ref))
    o_win = _softmax_done(carry)

    def gate_col(branch):
        return _stack_cols(gates, [branch * N_HEADS_NSA + h for h in range(N_HEADS_NSA)])

    o_cmp = jnp.concatenate([own_lanes(ocmp_ref[0, g], g) for g in groups], axis=0)
    o = gate_col(0) * o_cmp + gate_col(1) * o_slc + gate_col(2) * o_win
    for h in range(N_HEADS_NSA):
        g = h // GROUP
        o_ref[0, :, h * HEAD_DIM:(h + 1) * HEAD_DIM] = o[h * t:(h + 1) * t, g * HEAD_DIM:(g + 1) * HEAD_DIM]


def _nsa_sample_attn(page_table, q3, gate3, o_cmp, sel, ksn, vsn, kwn, vwn, cwk, cwv, pool_k, pool_v,
                     bias_slc, bias_new, bias_win, expand):
    n, n_pages = page_table.shape
    t = q3.shape[1]
    past = n_pages * PAGE_SIZE
    wb = cwk.shape[1]
    width = sel.shape[3]
    assert past % KEY_CHUNK == 0 and t <= NEW_PAD
    per_seq = lambda shape: pl.BlockSpec((1,) + shape, lambda i, pt: (i,) + (0,) * len(shape))
    full = lambda a: pl.BlockSpec(a.shape, lambda i, pt: (0,) * a.ndim)
    anyspec = pl.BlockSpec(memory_space=pl.ANY)
    grid_spec = pltpu.PrefetchScalarGridSpec(
        num_scalar_prefetch=1,
        grid=(n,),
        in_specs=[per_seq((t, NSA_WIDTH)), per_seq((t, LANES)), per_seq((N_KV_NSA, GROUP * t, HEAD_DIM)),
                  per_seq((N_KV_NSA, t, width)),
                  per_seq((t, KV_WIDTH)), per_seq((t, KV_WIDTH)), per_seq((t, KV_WIDTH)), per_seq((t, KV_WIDTH)),
                  per_seq((wb, KV_WIDTH)), per_seq((wb, KV_WIDTH)), anyspec, anyspec,
                  full(bias_slc), full(bias_new), full(bias_win), full(expand)],
        out_specs=per_seq((t, NSA_WIDTH)),
        scratch_shapes=[pltpu.VMEM((2, past, KV_WIDTH), F32), pltpu.VMEM((2, past, KV_WIDTH), F32),
                        pltpu.SemaphoreType.DMA((2,)), pltpu.SemaphoreType.DMA((2,)),
                        pltpu.VMEM((N_KV_NSA * GROUP * t, past), F32)],
    )
    return pl.pallas_call(
        functools.partial(_nsa_sample_attn_kernel, n_pages=n_pages),
        out_shape=jax.ShapeDtypeStruct((n, t, NSA_WIDTH), F32),
        grid_spec=grid_spec,
        compiler_params=_cparams(("arbitrary",)),
        name="nsa_sample_attn",
    )(page_table, q3, gate3, o_cmp, sel, ksn, vsn, kwn, vwn, cwk, cwv, pool_k, pool_v,
      bias_slc, bias_new, bias_win, expand)


def _segsum(x, ones_ref):
    hi = x.astype(BF16)
    lo = (x - hi.astype(F32)).astype(BF16)
    return _dot(hi, ones_ref[...]) + _dot(lo, ones_ref[...])


def _softplus(x):
    return jnp.maximum(x, 0.0) + jnp.log(1.0 + jnp.exp(-jnp.abs(x)))


def _rwkv_pre_kernel(p_ref, prev_ref, mu_ref, w0_ref, w2_ref, a0_ref, a2_ref, g2_ref, kk_ref, ka_ref, ones_ref,
                     r_ref, k_ref, v_ref, kkn_ref, kka_ref, d_ref, g_ref, last_ref, buf_ref):
    ti = pl.program_id(1)
    tt = p_ref.shape[1]
    p = p_ref[0]
    first = jnp.where(ti == 0, prev_ref[0], last_ref[...])
    last_ref[...] = p[tt - 1:tt]
    buf_ref[SUBLANES - 1:SUBLANES, :] = first
    buf_ref[SUBLANES:SUBLANES + tt, :] = p
    prev = buf_ref[SUBLANES - 1:SUBLANES - 1 + tt, :]
    xs = p + (prev - p) * mu_ref[...]
    W = RWKV_WIDTH
    r, k, v = xs[:, 0:W], xs[:, W:2 * W], xs[:, 2 * W:3 * W]
    c = 3 * W
    xw = xs[:, c:c + DECAY_LORA]
    xa = xs[:, c + DECAY_LORA:c + DECAY_LORA + AAA_LORA]
    xg = xs[:, c + DECAY_LORA + AAA_LORA:]
    w = -_softplus(-(w0_ref[...] + _dot(jnp.tanh(xw).astype(BF16), w2_ref[...]))) - 0.5
    a = jax.nn.sigmoid(a0_ref[...] + _dot(xa.astype(BF16), a2_ref[...]))
    kk = k * kk_ref[...]
    kk = kk / jnp.maximum(jnp.sqrt(_segsum(kk * kk, ones_ref)), 1e-12)
    r_ref[0] = r
    k_ref[0] = k * (1.0 + (a - 1.0) * ka_ref[...])
    v_ref[0] = v
    kkn_ref[0] = kk
    kka_ref[0] = kk * a
    d_ref[0] = -jnp.exp(w)
    g_ref[0] = _dot(jax.nn.sigmoid(xg).astype(BF16), g2_ref[...])


def _rwkv_pre(p3, prev, mu, w0, w2, a0, a2, g2, k_k, k_a, ones_bd):
    n, t, c = p3.shape
    tt = min(t, 256)
    assert t % tt == 0
    row = lambda a: a.reshape(1, -1)
    full = lambda a: pl.BlockSpec(a.shape, lambda b, i: (0,) * a.ndim)
    params = [row(mu), row(w0), w2, row(a0), a2, g2, row(k_k), row(k_a), ones_bd]
    out_spec = pl.BlockSpec((1, tt, RWKV_WIDTH), lambda b, i: (b, i, 0))
    return pl.pallas_call(
        _rwkv_pre_kernel,
        out_shape=[jax.ShapeDtypeStruct((n, t, RWKV_WIDTH), F32)] * 7,
        grid=(n, t // tt),
        in_specs=[pl.BlockSpec((1, tt, c), lambda b, i: (b, i, 0)),
                  pl.BlockSpec((1, 1, c), lambda b, i: (b, 0, 0))] + [full(a) for a in params],
        out_specs=[out_spec] * 7,
        scratch_shapes=[pltpu.VMEM((1, c), F32), pltpu.VMEM((SUBLANES + tt, c), F32)],
        compiler_params=_cparams(("parallel", "arbitrary")),
        name="rwkv_pre",
    )(p3, prev.reshape(n, 1, c), *params)


def _rwkv_scan_kernel(r_ref, k_ref, v_ref, kk_ref, kka_ref, ld_ref, s0_ref, y_ref, sout_ref, s_ref):
    ci = pl.program_id(1)
    t_in = r_ref.shape[1]
    C = TC

    @pl.when(ci == 0)
    def _():
        s_ref[...] = s0_ref[0]

    def rows(ref):
        x = ref[0]
        if t_in < C:
            x = jnp.concatenate([x, jnp.zeros((C - t_in, x.shape[1]), F32)], axis=0)
        return x

    ld = rows(ld_ref)
    tri = jnp.where(lax.broadcasted_iota(jnp.int32, (C, C), 1) <= lax.broadcasted_iota(jnp.int32, (C, C), 0),
                    1.0, 0.0).astype(BF16)
    hi = ld.astype(BF16)
    rem = ld - hi.astype(F32)
    mid = rem.astype(BF16)
    lo = (rem - mid.astype(F32)).astype(BF16)
    cum = _dot(tri, hi) + _dot(tri, mid) + _dot(tri, lo)
    p_inc = jnp.exp(cum)
    p_inv = jnp.exp(-cum)
    a_t = -rows(kk_ref) * jnp.exp(cum - ld)
    b_t = rows(kka_ref) * p_inv
    k_t = rows(k_ref) * p_inv
    r_t = rows(r_ref) * p_inc
    v = rows(v_ref)
    p_end = p_inc[C - 1:C]

    i2 = lax.broadcasted_iota(jnp.int32, (2 * C, 2 * C), 0)
    j2 = lax.broadcasted_iota(jnp.int32, (2 * C, 2 * C), 1)
    src = j2 & (C - 1)
    keep = src < jnp.where(i2 < C, i2, i2 - C + 1)

    heads = range(N_HEADS_RWKV)
    cols = [slice(h * HEAD_DIM, (h + 1) * HEAD_DIM) for h in heads]
    ar = [jnp.concatenate([a_t[:, c], r_t[:, c]], axis=0).astype(BF16) for c in cols]
    bk = [jnp.concatenate([b_t[:, c], k_t[:, c]], axis=0).astype(BF16) for c in cols]
    vh = [v[:, c].astype(BF16) for c in cols]
    s0 = [s_ref[h] for h in heads]
    quad = [jnp.where(keep, _dot_nt(ar[h], bk[h]), 0.0).astype(BF16) for h in heads]
    base = [_dot_nt(ar[h], s0[h].astype(BF16)) for h in heads]
    x = [base[h][:C] + _dot(quad[h][:C, C:], vh[h]) for h in heads]
    power = [quad[h][:C, :C] for h in heads]
    levels = C.bit_length() - 1
    for lvl in range(levels):
        x = [x[h] + _dot(power[h], x[h].astype(BF16)) for h in heads]
        if lvl + 1 < levels:
            power = [_dot(power[h], power[h]).astype(BF16) for h in heads]
    uv = [jnp.concatenate([x[h].astype(BF16), vh[h]], axis=0) for h in heads]
    ys = [base[h][C:] + _dot(quad[h][C:, :], uv[h]) for h in heads]
    for h in heads:
        upd = lax.dot_general(uv[h], bk[h], (((0,), (0,)), ((), ())), preferred_element_type=F32)
        s_ref[h] = (s0[h] + upd) * p_end[:, cols[h]]
    y = jnp.concatenate(ys, axis=1)
    y_ref[0] = y[:t_in]

    @pl.when(ci == pl.num_programs(1) - 1)
    def _():
        sout_ref[0] = s_ref[...]


def _rwkv_scan(r, k, v, kk, kka, d, s0):
    n, t, w = r.shape
    tc = min(t, TC)
    assert t % tc == 0 and TC & (TC - 1) == 0
    seq_spec = pl.BlockSpec((1, tc, w), lambda b, i: (b, i, 0))
    st_spec = pl.BlockSpec((1, N_HEADS_RWKV, HEAD_DIM, HEAD_DIM), lambda b, i: (b, 0, 0, 0))
    return pl.pallas_call(
        _rwkv_scan_kernel,
        out_shape=[jax.ShapeDtypeStruct((n, t, w), F32),
                   jax.ShapeDtypeStruct((n, N_HEADS_RWKV, HEAD_DIM, HEAD_DIM), F32)],
        grid=(n, t // tc),
        in_specs=[seq_spec] * 6 + [st_spec],
        out_specs=[seq_spec, st_spec],
        scratch_shapes=[pltpu.VMEM((N_HEADS_RWKV, HEAD_DIM, HEAD_DIM), F32)],
        compiler_params=_cparams(("parallel", "arbitrary")),
        name="rwkv_scan",
    )(r, k, v, kk, kka, d, s0)


def _mix_out_kernel(x_ref, nsa_ref, y_ref, r_ref, k_ref, v_ref, g_ref, rk_ref, lnw_ref, lnb_ref, ones_ref,
                    wa_ref, wb_ref, o_ref):
    y = y_ref[...]
    mean = _segsum(y, ones_ref) * (1.0 / HEAD_DIM)
    yc = y - mean
    var = _segsum(yc * yc, ones_ref) * (1.0 / HEAD_DIM)
    yn = yc * lax.rsqrt(var + GN_EPS) * lnw_ref[...] + lnb_ref[...]
    bonus = _segsum(r_ref[...] * k_ref[...] * rk_ref[...], ones_ref) * v_ref[...]
    rw = ((yn + bonus) * g_ref[...]).astype(BF16)
    o_ref[...] = x_ref[...] + _dot(nsa_ref[...].astype(BF16), wa_ref[...]) + _dot(rw, wb_ref[...])


def _mix_out(x, o_nsa, y, r, k, v, g, r_k, ln_w, ln_b, ones_bd, wa, wb):
    m, d = x.shape
    tm = min(m, 512)
    assert m % tm == 0
    row = lambda a: a.reshape(1, -1)
    tile = lambda w: pl.BlockSpec((tm, w), lambda i: (i, 0))
    full = lambda a: pl.BlockSpec(a.shape, lambda i: (0,) * a.ndim)
    params = [row(r_k), row(ln_w), row(ln_b), ones_bd, wa, wb]
    return pl.pallas_call(
        _mix_out_kernel,
        out_shape=jax.ShapeDtypeStruct((m, d), F32),
        grid=(m // tm,),
        in_specs=[tile(d), tile(NSA_WIDTH)] + [tile(RWKV_WIDTH)] * 5 + [full(a) for a in params],
        out_specs=tile(d),
        compiler_params=_cparams(("parallel",)),
        name="mix_out",
    )(x, o_nsa, y, r, k, v, g, *params)


def _block_diag2(w):
    z = jnp.zeros_like(w)
    return jnp.concatenate([jnp.concatenate([w, z], axis=-1), jnp.concatenate([z, w], axis=-1)], axis=-2)


def _compress_weights(pe, w1, w2):
    return (jnp.concatenate([pe] * N_KV_NSA, axis=1), _block_diag2(w1).astype(BF16), _block_diag2(w2).astype(BF16))


def kernel(x_prompt, x_sample, cache_cmp_k, cache_cmp_v, cache_slc_k, cache_slc_v, cache_win_k, cache_win_v,
           state_wkv, state_shift, page_table, rel_bias_table, ffn1_norm, ffn1_wg, ffn1_wu, ffn1_wd, mix_norm,
           w_in, cmp_pe_k, cmp_w1_k, cmp_w2_k, cmp_pe_v, cmp_w1_v, cmp_w2_v, shift_mu, decay_w0, decay_w2,
           aaa_a0, aaa_a2, gate_g2, k_k, k_a, r_k, ln_x_w, ln_x_b, w_out, ffn2_norm, ffn2_wg, ffn2_wu, ffn2_wd,
           final_norm):
    bp, seq, d = x_prompt.shape
    bs, ts, _ = x_sample.shape
    depth = w_in.shape[0]
    n_pages = page_table.shape[1]
    past = n_pages * PAGE_SIZE
    n_phys = cache_cmp_k.shape[1]
    wbuf = cache_win_k.shape[2]
    assert seq % QT == 0 and seq >= WINDOW and ts < CMP_BLOCK and wbuf == WINDOW
    nq = seq // QT
    nb = seq // SLC_BLOCK
    nbp = past // SLC_BLOCK
    nsa_main = NSA_WIDTH + 6 * KV_WIDTH
    nsa_cols = nsa_main + N_GATES

    bias_tiles = _bias_tiles(rel_bias_table)
    bias_cmp_p = _bias_rows(rel_bias_table, q0=0, rows=seq, n_keys=nb, key_first=CMP_BLOCK - 1, key_stride=CMP_BLOCK)
    bias_cmp_p = bias_cmp_p.reshape(N_KV_NSA, GROUP, nq, QT, nb).transpose(0, 2, 4, 1, 3).reshape(
        N_KV_NSA, nq, nb, GROUP * QT)
    bias_cmp_s = _bias_rows(rel_bias_table, q0=past, rows=ts, n_keys=nbp, key_first=CMP_BLOCK - 1,
                            key_stride=CMP_BLOCK).reshape(N_KV_NSA, GROUP * ts, nbp)
    stacked = lambda b: b.reshape(N_HEADS_NSA * ts, b.shape[-1])
    bias_slc_s = stacked(_bias_rows(rel_bias_table, q0=past, rows=ts, n_keys=past, key_first=0, key_stride=1))
    bias_new_s = stacked(_bias_rows(rel_bias_table, q0=past, rows=ts, n_keys=LANES, key_first=past, key_stride=1))
    bias_win_s = stacked(_bias_rows(rel_bias_table, q0=past, rows=ts, n_keys=wbuf,
                                    key_first=past - wbuf, key_stride=1))
    expand_s = (jnp.arange(past, dtype=jnp.int32)[None, :] // SLC_BLOCK
                == jnp.arange(nbp, dtype=jnp.int32)[:, None]).astype(BF16)
    head_of = jnp.arange(RWKV_WIDTH, dtype=jnp.int32) // HEAD_DIM
    ones_bd = (head_of[:, None] == head_of[None, :]).astype(BF16)

    xp = x_prompt.reshape(bp * seq, d)
    xs = x_sample.reshape(bs * ts, d)
    states = []
    for l in range(depth):
        bf = lambda a: a[l].astype(BF16)
        w_in_l = w_in[l]
        w_pad = jnp.concatenate(
            [w_in_l[:, :nsa_main], w_in_l[:, nsa_main:nsa_cols], jnp.zeros((d, LANES - N_GATES), F32),
             w_in_l[:, nsa_cols:]], axis=1).astype(BF16)
        cw_k = _compress_weights(cmp_pe_k[l], cmp_w1_k[l], cmp_w2_k[l])
        cw_v = _compress_weights(cmp_pe_v[l], cmp_w1_v[l], cmp_w2_v[l])
        wa, wb = w_out[l][:NSA_WIDTH].astype(BF16), w_out[l][NSA_WIDTH:].astype(BF16)
        rw_pre = (shift_mu[l], decay_w0[l], bf(decay_w2), aaa_a0[l], bf(aaa_a2), bf(gate_g2), k_k[l], k_a[l], ones_bd)
        ffn1 = (ffn1_norm[l], bf(ffn1_wg), bf(ffn1_wu), bf(ffn1_wd), final_norm)
        ffn2 = (ffn2_norm[l], bf(ffn2_wg), bf(ffn2_wu), bf(ffn2_wd), final_norm)
        last = l == depth - 1

        xp = _ffn_half(xp, *ffn1, final=False)
        xs = _ffn_half(xs, *ffn1, final=False)

        q, kc, vc, ks, vs, kw, vw, gate, zr = _proj(xp, mix_norm[l], w_pad)
        kcc = _compress(kc, *cw_k)
        vcc = _compress(vc, *cw_v)
        o_nsa = _nsa_prompt(q, gate, ks, vs, kw, vw, kcc, vcc, bias_tiles, bias_cmp_p, bp, seq)
        zr3 = zr.reshape(bp, seq, RWKV_COLS)
        r, k2, v, kk, kka, dec, gg = _rwkv_pre(zr3, jnp.zeros((bp, RWKV_COLS), F32), *rw_pre)
        y, wkv_p = _rwkv_scan(r, k2, v, kk, kka, dec,
                              jnp.zeros((bp, N_HEADS_RWKV, HEAD_DIM, HEAD_DIM), F32))
        flat = lambda a: a.reshape(-1, RWKV_WIDTH)
        xp = _mix_out(xp, o_nsa, flat(y), flat(r), flat(k2), flat(v), flat(gg), r_k[l], ln_x_w[l], ln_x_b[l],
                      ones_bd, wa, wb)
        kv5 = lambda a, n_, t_: a.reshape(n_, t_, N_KV_NSA, HEAD_DIM)
        p_state = (kv5(kc, bp, seq), kv5(vc, bp, seq), kv5(ks, bp, seq), kv5(vs, bp, seq),
                   kv5(kw, bp, seq)[:, seq - WINDOW:], kv5(vw, bp, seq)[:, seq - WINDOW:],
                   wkv_p, zr3[:, seq - 1])

        q, kc, vc, ks, vs, kw, vw, gate, zr = _proj(xs, mix_norm[l], w_pad)
        pool = lambda c: c[l].reshape(n_phys, PAGE_SIZE, KV_WIDTH)
        kcc, vcc = _compress_paged(page_table, pool(cache_cmp_k), pool(cache_cmp_v), cw_k, cw_v)
        seq3 = lambda a: a.reshape(bs, ts, a.shape[-1])
        o_cmp, sel = _nsa_sample_cmp(seq3(q), kcc, vcc, bias_cmp_s, past)
        cwk = cache_win_k[l].reshape(bs, wbuf, KV_WIDTH)
        cwv = cache_win_v[l].reshape(bs, wbuf, KV_WIDTH)
        o_nsa = _nsa_sample_attn(page_table, seq3(q), seq3(gate), o_cmp, sel, seq3(ks), seq3(vs), seq3(kw),
                                 seq3(vw), cwk, cwv, pool(cache_slc_k), pool(cache_slc_v),
                                 bias_slc_s, bias_new_s, bias_win_s, expand_s)
        zr3 = seq3(zr)
        r, k2, v, kk, kka, dec, gg = _rwkv_pre(zr3, state_shift[l], *rw_pre)
        y, wkv_s = _rwkv_scan(r, k2, v, kk, kka, dec, state_wkv[l])
        xs = _mix_out(xs, o_nsa.reshape(bs * ts, NSA_WIDTH), flat(y), flat(r), flat(k2), flat(v), flat(gg),
                      r_k[l], ln_x_w[l], ln_x_b[l], ones_bd, wa, wb)
        win = lambda cache, new: jnp.concatenate([cache[l], kv5(new, bs, ts)], axis=1)[:, ts:]
        s_state = (kv5(kc, bs, ts), kv5(vc, bs, ts), kv5(ks, bs, ts), kv5(vs, bs, ts),
                   win(cache_win_k, kw), win(cache_win_v, vw), wkv_s, zr3[:, ts - 1])

        xp = _ffn_half(xp, *ffn2, final=last)
        xs = _ffn_half(xs, *ffn2, final=last)
        states.append(p_state + s_state)

    stacked_states = [jnp.stack(z) for z in zip(*states)]
    return (xp.reshape(bp, seq, d), xs.reshape(bs, ts, d), *stacked_states)
```

```python
import functools
import math

import jax
import jax.numpy as jnp
import numpy as np
from jax import lax
from jax.experimental import pallas as pl
from jax.experimental.pallas import tpu as pltpu

F32 = jnp.float32
BF16 = jnp.bfloat16

HEAD_DIM = 64
N_HEADS_NSA = 8
N_KV_NSA = 2
GROUP = N_HEADS_NSA // N_KV_NSA
N_HEADS_RWKV = 8
NSA_WIDTH = N_HEADS_NSA * HEAD_DIM
RWKV_WIDTH = N_HEADS_RWKV * HEAD_DIM
KV_WIDTH = N_KV_NSA * HEAD_DIM
CMP_BLOCK = 64
SLC_BLOCK = 64
TOP_K_BLOCKS = 16
WINDOW = 512
PAGE_SIZE = 128
N_BUCKETS = 32
MAX_DISTANCE = 128
DECAY_LORA = 64
AAA_LORA = 64
GATE_LORA = 128
NORM_EPS = 1e-6
GN_EPS = 64e-5
ATTN_SCALE = HEAD_DIM ** -0.5
FORCE_SCORE = 1e4
NEG_INF = -1e30
M_FLOOR = -1e29
RWKV_COLS = 3 * RWKV_WIDTH + DECAY_LORA + AAA_LORA + GATE_LORA
N_GATES = 3 * N_HEADS_NSA

LANES = 128
SUBLANES = 8
VMEM_LIMIT = 56 * 1024 * 1024

QT = 128
TC = 64


def _bucket_thresholds():
    max_exact = N_BUCKETS // 2
    thr = list(range(max_exact + 1))
    for b in range(max_exact + 1, N_BUCKETS):
        x = max_exact * (MAX_DISTANCE / max_exact) ** ((b - max_exact) / (N_BUCKETS - max_exact))
        thr.append(int(math.ceil(x - 1e-9)))
    return thr


BUCKET_THR = _bucket_thresholds()


def _cparams(sem):
    return pltpu.CompilerParams(dimension_semantics=sem, vmem_limit_bytes=VMEM_LIMIT)


def _dot(a, b):
    return jnp.dot(a, b, preferred_element_type=F32)


def _dot_nt(a, b):
    return lax.dot_general(a, b, (((1,), (1,)), ((), ())), preferred_element_type=F32)


def _rms(x, g):
    return x * lax.rsqrt(jnp.mean(x * x, axis=-1, keepdims=True) + NORM_EPS) * g


def _ffn_kernel(x_ref, g_ref, wg_ref, wu_ref, wd_ref, gf_ref, o_ref, h_ref, acc_ref, *, final):
    j = pl.program_id(1)

    @pl.when(j == 0)
    def _():
        h_ref[...] = _rms(x_ref[...], g_ref[...]).astype(BF16)
        acc_ref[...] = jnp.zeros_like(acc_ref)

    h = h_ref[...]
    a = _dot(h, wg_ref[...])
    b = _dot(h, wu_ref[...])
    act = (a * jax.nn.sigmoid(a) * b).astype(BF16)
    acc_ref[...] += _dot(act, wd_ref[...])

    @pl.when(j == pl.num_programs(1) - 1)
    def _():
        y = x_ref[...] + 0.5 * acc_ref[...]
        if final:
            y = _rms(y, gf_ref[...])
        o_ref[...] = y


def _ffn_half(x, g, wg, wu, wd, gf, *, final):
    m, d = x.shape
    ff = wg.shape[1]
    tm = min(m, 1024)
    tf = 256 if ff % 256 == 0 else ff
    assert m % tm == 0 and ff % tf == 0
    return pl.pallas_call(
        functools.partial(_ffn_kernel, final=final),
        out_shape=jax.ShapeDtypeStruct((m, d), F32),
        grid=(m // tm, ff // tf),
        in_specs=[
            pl.BlockSpec((tm, d), lambda i, j: (i, 0)),
            pl.BlockSpec((1, d), lambda i, j: (0, 0)),
            pl.BlockSpec((d, tf), lambda i, j: (0, j)),
            pl.BlockSpec((d, tf), lambda i, j: (0, j)),
            pl.BlockSpec((tf, d), lambda i, j: (j, 0)),
            pl.BlockSpec((1, d), lambda i, j: (0, 0)),
        ],
        out_specs=pl.BlockSpec((tm, d), lambda i, j: (i, 0)),
        scratch_shapes=[pltpu.VMEM((tm, d), BF16), pltpu.VMEM((tm, d), F32)],
        compiler_params=_cparams(("parallel", "arbitrary")),
        name="ffn_half",
    )(x, g.reshape(1, d), wg, wu, wd, gf.reshape(1, d))


def _proj_kernel(x_ref, g_ref, w_ref, q_ref, kc_ref, vc_ref, ks_ref, vs_ref, kw_ref, vw_ref, gate_ref, zr_ref):
    h = _rms(x_ref[...], g_ref[...]).astype(BF16)
    z = _dot(h, w_ref[...])
    c = 0
    for ref in (q_ref, kc_ref, vc_ref, ks_ref, vs_ref, kw_ref, vw_ref, gate_ref, zr_ref):
        w = ref.shape[1]
        ref[...] = z[:, c:c + w]
        c += w


def _proj(x, g, w_pad):
    m, d = x.shape
    tm = min(m, 512)
    widths = (NSA_WIDTH,) + (KV_WIDTH,) * 6 + (LANES, RWKV_COLS)
    assert sum(widths) == w_pad.shape[1] and m % tm == 0
    return pl.pallas_call(
        _proj_kernel,
        out_shape=[jax.ShapeDtypeStruct((m, w), F32) for w in widths],
        grid=(m // tm,),
        in_specs=[
            pl.BlockSpec((tm, d), lambda i: (i, 0)),
            pl.BlockSpec((1, d), lambda i: (0, 0)),
            pl.BlockSpec(w_pad.shape, lambda i: (0, 0)),
        ],
        out_specs=[pl.BlockSpec((tm, w), lambda i: (i, 0)) for w in widths],
        compiler_params=_cparams(("parallel",)),
        name="in_proj",
    )(x, g.reshape(1, d), w_pad)


def _out_kernel(x_ref, a_ref, b_ref, wa_ref, wb_ref, o_ref):
    o_ref[...] = (x_ref[...] + _dot(a_ref[...].astype(BF16), wa_ref[...])
                  + _dot(b_ref[...].astype(BF16), wb_ref[...]))


def _out_proj(x, a, b, wa, wb):
    m, d = x.shape
    tm = min(m, 512)
    return pl.pallas_call(
        _out_kernel,
        out_shape=jax.ShapeDtypeStruct((m, d), F32),
        grid=(m // tm,),
        in_specs=[
            pl.BlockSpec((tm, d), lambda i: (i, 0)),
            pl.BlockSpec((tm, a.shape[1]), lambda i: (i, 0)),
            pl.BlockSpec((tm, b.shape[1]), lambda i: (i, 0)),
            pl.BlockSpec(wa.shape, lambda i: (0, 0)),
            pl.BlockSpec(wb.shape, lambda i: (0, 0)),
        ],
        out_specs=pl.BlockSpec((tm, d), lambda i: (i, 0)),
        compiler_params=_cparams(("parallel",)),
        name="out_proj",
    )(x, a, b, wa, wb)


def _bias_lookup(dist, table_ref, heads):
    vals = [jnp.full(dist.shape, table_ref[0, h], F32) for h in heads]
    for b in range(1, N_BUCKETS):
        ge = dist >= BUCKET_THR[b]
        vals = [jnp.where(ge, table_ref[b, h], v) for h, v in zip(heads, vals)]
    return vals


def _bias_tiles_kernel(table_ref, o_ref):
    j = lax.broadcasted_iota(jnp.int32, (QT, QT), 0)
    i = lax.broadcasted_iota(jnp.int32, (QT, QT), 1)
    for off in range(3):
        vals = _bias_lookup(off * QT + i - j, table_ref, range(N_HEADS_NSA))
        for h, v in enumerate(vals):
            g, r = divmod(h, GROUP)
            o_ref[g, off, :, r * QT:(r + 1) * QT] = v


def _bias_tiles(table):
    return pl.pallas_call(
        _bias_tiles_kernel,
        out_shape=jax.ShapeDtypeStruct((N_KV_NSA, 3, QT, GROUP * QT), F32),
        in_specs=[pl.BlockSpec(memory_space=pltpu.SMEM)],
        name="bias_tiles",
    )(table)


def _bias_rows_kernel(table_ref, o_ref, *, q0, key_first, key_stride):
    _, _, rc, L = o_ref.shape
    i = lax.broadcasted_iota(jnp.int32, (rc, L), 0) + pl.program_id(0) * rc
    j = lax.broadcasted_iota(jnp.int32, (rc, L), 1)
    dist = (q0 + i) - (key_first + key_stride * j)
    vals = _bias_lookup(dist, table_ref, range(N_HEADS_NSA))
    for h, v in enumerate(vals):
        g, r = divmod(h, GROUP)
        o_ref[g, r] = v


def _bias_rows(table, *, q0, rows, n_keys, key_first, key_stride):
    rc = min(rows, 256)
    assert rows % rc == 0
    return pl.pallas_call(
        functools.partial(_bias_rows_kernel, q0=q0, key_first=key_first, key_stride=key_stride),
        out_shape=jax.ShapeDtypeStruct((N_KV_NSA, GROUP, rows, n_keys), F32),
        grid=(rows // rc,),
        in_specs=[pl.BlockSpec(memory_space=pltpu.SMEM)],
        out_specs=pl.BlockSpec((N_KV_NSA, GROUP, rc, n_keys), lambda i: (0, 0, i, 0)),
        compiler_params=_cparams(("parallel",)),
        name="bias_rows",
    )(table)


def _gelu_tanh(x):
    return 0.5 * x * (1.0 + jnp.tanh(math.sqrt(2.0 / math.pi) * (x + 0.044715 * (x * x * x))))


def _compress_blocks(load_rows, nblk, pe_ref, w1_ref, w2_ref):
    acc = jnp.zeros((nblk, KV_WIDTH), F32)
    for c in range(CMP_BLOCK):
        rows = load_rows(c) + pe_ref[c:c + 1, :]
        acc = acc + _dot(rows.astype(BF16), w1_ref[c])
    return _dot(_gelu_tanh(acc).astype(BF16), w2_ref[...])


def _compress_kernel(x_ref, pe_ref, w1_ref, w2_ref, o_ref):
    nblk = o_ref.shape[0]
    o_ref[...] = _compress_blocks(lambda c: x_ref[pl.ds(c, nblk, stride=CMP_BLOCK), :],
                                  nblk, pe_ref, w1_ref, w2_ref)


def _compress(x, pe2, w1bd, w2bd):
    rows = x.shape[0]
    nb = rows // CMP_BLOCK
    nblk = min(nb, 128)
    assert nb % nblk == 0
    return pl.pallas_call(
        _compress_kernel,
        out_shape=jax.ShapeDtypeStruct((nb, KV_WIDTH), F32),
        grid=(nb // nblk,),
        in_specs=[
            pl.BlockSpec((nblk * CMP_BLOCK, KV_WIDTH), lambda i: (i, 0)),
            pl.BlockSpec(pe2.shape, lambda i: (0, 0)),
            pl.BlockSpec(w1bd.shape, lambda i: (0, 0, 0)),
            pl.BlockSpec(w2bd.shape, lambda i: (0, 0)),
        ],
        out_specs=pl.BlockSpec((nblk, KV_WIDTH), lambda i: (i, 0)),
        compiler_params=_cparams(("parallel",)),
        name="compress",
    )(x, pe2, w1bd, w2bd)


def _page_copies_by_row(pt_ref, n, pool_ref, buf_ref, sem_ref, slot, n_pages):
    per_page = PAGE_SIZE // CMP_BLOCK
    rows = CMP_BLOCK * N_KV_NSA
    return [pltpu.make_async_copy(pool_ref.at[pl.ds((pt_ref[n, p] * per_page + j) * rows, rows)],
                                  buf_ref.at[slot, :, p * per_page + j],
                                  sem_ref.at[slot])
            for p in range(n_pages) for j in range(per_page)]


def _page_copies_by_offset(pt_ref, n, pool_ref, buf_ref, sem_ref, slot, n_pages):
    rows = PAGE_SIZE * N_KV_NSA
    return [pltpu.make_async_copy(pool_ref.at[pl.ds(pt_ref[n, p] * rows, rows)], buf_ref.at[slot, :, p],
                                  sem_ref.at[slot]) for p in range(n_pages)]


def _paged_prefetch(pt_ref, pools, bufs, sems, n_pages, _page_copies):
    n = pl.program_id(0)
    slot = n % 2

    def start(seq, slot_):
        for pool, buf, sem in zip(pools, bufs, sems):
            for cp in _page_copies(pt_ref, seq, pool, buf, sem, slot_, n_pages):
                cp.start()

    @pl.when(n == 0)
    def _():
        start(0, 0)

    @pl.when(n + 1 < pl.num_programs(0))
    def _():
        start(n + 1, 1 - slot)

    for pool, buf, sem in zip(pools, bufs, sems):
        for cp in _page_copies(pt_ref, n, pool, buf, sem, slot, n_pages):
            cp.wait()
    return slot


def _compress_paged_kernel(pt_ref, pk_ref, pv_ref, pek_ref, w1k_ref, w2k_ref, pev_ref, w1v_ref, w2v_ref,
                           ok_ref, ov_ref, bk_ref, bv_ref, sk_ref, sv_ref, *, n_pages):
    slot = _paged_prefetch(pt_ref, (pk_ref, pv_ref), (bk_ref, bv_ref), (sk_ref, sv_ref), n_pages,
                           _page_copies_by_row)
    nblk = ok_ref.shape[1]

    def compress(buf_ref, pe_ref, w1_ref, w2_ref):
        acc = [jnp.zeros((nblk, HEAD_DIM), F32) for _ in range(N_KV_NSA)]
        for c in range(CMP_BLOCK):
            for g in range(N_KV_NSA):
                rows = buf_ref[slot, c * N_KV_NSA + g] + pe_ref[c:c + 1, :]
                acc[g] = acc[g] + _dot(rows.astype(BF16), w1_ref[c])
        return jnp.concatenate([_dot(_gelu_tanh(a).astype(BF16), w2_ref[...]) for a in acc], axis=1)

    ok_ref[0] = compress(bk_ref, pek_ref, w1k_ref, w2k_ref)
    ov_ref[0] = compress(bv_ref, pev_ref, w1v_ref, w2v_ref)


def _compress_paged(page_table, pool_k, pool_v, cw_k, cw_v):
    n, n_pages = page_table.shape
    past = n_pages * PAGE_SIZE
    nblk = past // CMP_BLOCK
    wspec = lambda a: pl.BlockSpec(a.shape, lambda i, pt: (0,) * a.ndim)
    grid_spec = pltpu.PrefetchScalarGridSpec(
        num_scalar_prefetch=1,
        grid=(n,),
        in_specs=[pl.BlockSpec(memory_space=pl.ANY), pl.BlockSpec(memory_space=pl.ANY)]
                 + [wspec(a) for a in cw_k + cw_v],
        out_specs=[pl.BlockSpec((1, nblk, KV_WIDTH), lambda i, pt: (i, 0, 0))] * 2,
        scratch_shapes=[pltpu.VMEM((2, CMP_BLOCK * N_KV_NSA, nblk, HEAD_DIM), F32),
                        pltpu.VMEM((2, CMP_BLOCK * N_KV_NSA, nblk, HEAD_DIM), F32),
                        pltpu.SemaphoreType.DMA((2,)), pltpu.SemaphoreType.DMA((2,))],
    )
    return pl.pallas_call(
        functools.partial(_compress_paged_kernel, n_pages=n_pages),
        out_shape=[jax.ShapeDtypeStruct((n, nblk, KV_WIDTH), F32)] * 2,
        grid_spec=grid_spec,
        compiler_params=_cparams(("arbitrary",)),
        name="compress_paged",
    )(page_table, pool_k, pool_v, *cw_k, *cw_v)


def _stack_heads(x, g):
    return jnp.concatenate(
        [x[:, (g * GROUP + r) * HEAD_DIM:(g * GROUP + r + 1) * HEAD_DIM] for r in range(GROUP)], axis=0)


def _stack_cols(x, cols):
    return jnp.concatenate([x[:, c:c + 1] for c in cols], axis=0)


def _softmax_step(carry, s, mask, v):
    m, l, acc = carry
    s = jnp.where(mask, s, NEG_INF)
    m_new = jnp.maximum(m, jnp.max(s, axis=1, keepdims=True))
    p = jnp.where(mask, jnp.exp(s - m_new), 0.0)
    alpha = jnp.exp(m - m_new)
    l = alpha * l + jnp.sum(p, axis=1, keepdims=True)
    pv = v(p.astype(BF16)) if callable(v) else _dot(p.astype(BF16), v)
    return m_new, l, alpha * acc + pv


def _softmax_init(rows, width):
    return (jnp.full((rows, 1), NEG_INF, F32), jnp.zeros((rows, 1), F32), jnp.zeros((rows, width), F32))


def _softmax_done(carry):
    _, l, acc = carry
    return acc / jnp.maximum(l, 1e-30)


def _topk_mask(imp, cur, n_blocks):
    blk = lax.broadcasted_iota(jnp.int32, imp.shape, 1)
    forced = (blk == 0) | (blk == cur) | (blk == cur - 1)
    score = jnp.where(blk <= cur, jnp.where(forced, FORCE_SCORE, imp), -FORCE_SCORE)
    rank = jnp.zeros(imp.shape, jnp.int32)
    for b in range(n_blocks):
        col = score[:, b:b + 1]
        ahead = (col > score) | ((col == score) & (blk > b))
        rank = rank + jnp.where(ahead, 1, 0)
    return (rank < TOP_K_BLOCKS) & (blk <= cur)


def _topk_mask_t(imp, cur, n_blocks):
    blk = lax.broadcasted_iota(jnp.int32, imp.shape, 0)
    forced = (blk == 0) | (blk == cur) | (blk == cur - 1)
    score = jnp.where(blk <= cur, jnp.where(forced, FORCE_SCORE, imp), -FORCE_SCORE)
    rank = jnp.zeros(imp.shape, jnp.int32)
    for b in range(n_blocks):
        row = score[b:b + 1, :]
        rank = rank + jnp.where(row > score, 1, jnp.where(row == score, jnp.where(blk > b, 1, 0), 0))
    return jnp.where(rank < TOP_K_BLOCKS, jnp.where(blk <= cur, 1.0, 0.0), 0.0)


def _nsa_prompt_kernel(q_ref, gate_ref, ks_ref, vs_ref, kw_ref, vw_ref, kcc_ref, vcc_ref, bt_ref, bc_ref,
                       o_ref, sel_ref, m_ref, acc_ref, pv_ref):
    qi = pl.program_id(1)
    seq = ks_ref.shape[1]
    nb = seq // SLC_BLOCK
    rows = GROUP * QT
    groups = range(N_KV_NSA)
    q = q_ref[...] * ATTN_SCALE
    gates_t = jax.nn.sigmoid(gate_ref[...]).T
    tpos_row = qi * QT + lax.broadcasted_iota(jnp.int32, (1, QT), 1)
    lane = lax.broadcasted_iota(jnp.int32, (1, KV_WIDTH), 1)
    own = [(lane // HEAD_DIM) == g for g in groups]
    zeros = jnp.zeros((rows, HEAD_DIM), F32)
    qg = []
    for g in groups:
        qs = _stack_heads(q, g)
        qg.append(jnp.concatenate([qs, zeros] if g == 0 else [zeros, qs], axis=1).astype(BF16))

    blk = lax.broadcasted_iota(jnp.int32, (nb, 1), 0)
    tpos_stacked = qi * QT + (lax.broadcasted_iota(jnp.int32, (1, rows), 1) & (QT - 1))
    readable = (blk + 1) * CMP_BLOCK - 1 <= tpos_stacked
    kcc = kcc_ref[0].astype(BF16)
    vcc = vcc_ref[0].astype(BF16)
    o_cmp = []
    for g in groups:
        s = jnp.where(readable, _dot_nt(kcc, qg[g]) + bc_ref[g, 0], NEG_INF)
        e = jnp.where(readable, jnp.exp(s - jnp.max(s, axis=0, keepdims=True)), 0.0)
        p = e / jnp.maximum(jnp.sum(e, axis=0, keepdims=True), 1e-30)
        o_cmp.append(lax.dot_general(vcc, p.astype(BF16), (((0,), (0,)), ((), ())), preferred_element_type=F32))
        imp = p[:, 0:QT]
        for r in range(1, GROUP):
            imp = imp + p[:, r * QT:(r + 1) * QT]
        sel_ref[g] = _topk_mask_t(imp, tpos_row // SLC_BLOCK, nb)

    def branch(lo, hi, k_ref, v_ref, penalty):
        def scores(kt):
            k = k_ref[0, pl.ds(pl.multiple_of(kt * QT, QT), QT), :].astype(BF16)
            return tuple(_dot_nt(k, qg[g]) for g in groups)

        def step(kt, s_cur):
            s_next = scores(jnp.minimum(kt + 1, hi - 1))
            off = pl.multiple_of(kt * QT, QT)
            v = v_ref[0, pl.ds(off, QT), :]
            for g in groups:
                m = m_ref[g]
                s = s_cur[g] + bt_ref[g, jnp.minimum(qi - kt, 2)]
                s = s + jnp.concatenate([penalty(g, off)] * GROUP, axis=1)
                m_new = jnp.maximum(m, jnp.max(s, axis=0, keepdims=True))
                p = jnp.exp(s - m_new).astype(BF16)
                v_ext = jnp.where(own[g], v, 1.0).astype(BF16)
                acc_ref[g] = jnp.exp(m - m_new) * (acc_ref[g] + pv_ref[g])
                pv_ref[g] = lax.dot_general(v_ext, p, (((0,), (0,)), ((), ())), preferred_element_type=F32)
                m_ref[g] = m_new
            return s_next

        for g in groups:
            m_ref[g] = jnp.full((1, rows), M_FLOOR, F32)
            acc_ref[g] = jnp.zeros((KV_WIDTH, rows), F32)
            pv_ref[g] = jnp.zeros((KV_WIDTH, rows), F32)
        lax.fori_loop(lo, hi, step, scores(lo))
        out = []
        for g in groups:
            acc = acc_ref[g] + pv_ref[g]
            denom = acc[(1 - g) * HEAD_DIM:(1 - g) * HEAD_DIM + 1, :]
            out.append(acc / jnp.maximum(denom, 1e-30))
        return out

    key_row = lax.broadcasted_iota(jnp.int32, (QT, 1), 0)

    def slc_penalty(g, off):
        causal = jnp.where(tpos_row - (off + key_row) >= 0, 0.0, NEG_INF)
        first = off // SLC_BLOCK
        chosen = jnp.where(key_row < SLC_BLOCK, sel_ref[g, pl.ds(first, 1), :], sel_ref[g, pl.ds(first + 1, 1), :])
        return jnp.where(chosen > 0.5, causal, NEG_INF)

    def win_penalty(g, off):
        dist = tpos_row - (off + key_row)
        return jnp.where(dist >= 0, jnp.where(dist <= WINDOW, 0.0, NEG_INF), NEG_INF)

    slc = branch(0, qi + 1, ks_ref, vs_ref, slc_penalty)
    win = branch(jnp.maximum(qi - WINDOW // QT, 0), qi + 1, kw_ref, vw_ref, win_penalty)

    def gate_row(branch, g):
        first = branch * N_HEADS_NSA + g * GROUP
        return jnp.concatenate([gates_t[first + r:first + r + 1, :] for r in range(GROUP)], axis=1)

    o_t = [gate_row(0, g) * o_cmp[g] + gate_row(1, g) * slc[g] + gate_row(2, g) * win[g] for g in groups]
    feature_row = lax.broadcasted_iota(jnp.int32, (KV_WIDTH, 1), 0)
    o = jnp.where(feature_row < HEAD_DIM, o_t[0], o_t[1]).T
    for g in groups:
        for r in range(GROUP):
            h = g * GROUP + r
            o_ref[:, h * HEAD_DIM:(h + 1) * HEAD_DIM] = o[r * QT:(r + 1) * QT, g * HEAD_DIM:(g + 1) * HEAD_DIM]


def _nsa_prompt(q, gate, ks, vs, kw, vw, kcc, vcc, bias_tiles, bias_cmp, n, seq):
    nq = seq // QT
    nb = seq // SLC_BLOCK
    seq_spec = pl.BlockSpec((1, seq, KV_WIDTH), lambda b, i: (b, 0, 0))
    cmp_spec = pl.BlockSpec((1, nb, KV_WIDTH), lambda b, i: (b, 0, 0))
    return pl.pallas_call(
        _nsa_prompt_kernel,
        out_shape=jax.ShapeDtypeStruct((n * seq, NSA_WIDTH), F32),
        grid=(n, nq),
        in_specs=[
            pl.BlockSpec((QT, NSA_WIDTH), lambda b, i: (b * nq + i, 0)),
            pl.BlockSpec((QT, LANES), lambda b, i: (b * nq + i, 0)),
            seq_spec, seq_spec, seq_spec, seq_spec, cmp_spec, cmp_spec,
            pl.BlockSpec(bias_tiles.shape, lambda b, i: (0, 0, 0, 0)),
            pl.BlockSpec((N_KV_NSA, 1, nb, GROUP * QT), lambda b, i: (0, i, 0, 0)),
        ],
        out_specs=pl.BlockSpec((QT, NSA_WIDTH), lambda b, i: (b * nq + i, 0)),
        scratch_shapes=[pltpu.VMEM((N_KV_NSA, nb, QT), F32),
                        pltpu.VMEM((N_KV_NSA, 1, GROUP * QT), F32),
                        pltpu.VMEM((N_KV_NSA, KV_WIDTH, GROUP * QT), F32),
                        pltpu.VMEM((N_KV_NSA, KV_WIDTH, GROUP * QT), F32)],
        compiler_params=_cparams(("parallel", "arbitrary")),
        name="nsa_prompt",
    )(q, gate, ks.reshape(n, seq, KV_WIDTH), vs.reshape(n, seq, KV_WIDTH), kw.reshape(n, seq, KV_WIDTH),
      vw.reshape(n, seq, KV_WIDTH), kcc.reshape(n, nb, KV_WIDTH), vcc.reshape(n, nb, KV_WIDTH),
      bias_tiles, bias_cmp)


NEW_PAD = SUBLANES


def _row_token(t):
    return jnp.concatenate([lax.broadcasted_iota(jnp.int32, (t, 1), 0)] * GROUP, axis=0)


def _nsa_sample_cmp_kernel(q_ref, kcc_ref, vcc_ref, bc_ref, o_ref, sel_ref, *, past):
    t = q_ref.shape[1]
    nbp = kcc_ref.shape[1]
    width = sel_ref.shape[3]
    q = q_ref[0]
    pos = past + _row_token(t)
    blk_end = (lax.broadcasted_iota(jnp.int32, (1, nbp), 1) + 1) * CMP_BLOCK - 1
    cur = (past + lax.broadcasted_iota(jnp.int32, (t, 1), 0)) // SLC_BLOCK
    for g in range(N_KV_NSA):
        cols = slice(g * HEAD_DIM, (g + 1) * HEAD_DIM)
        qg = _stack_heads(q, g).astype(BF16)
        s = _dot_nt(qg, kcc_ref[0][:, cols].astype(BF16)) * ATTN_SCALE + bc_ref[g]
        mask = blk_end <= pos
        s = jnp.where(mask, s, NEG_INF)
        e = jnp.where(mask, jnp.exp(s - jnp.max(s, axis=1, keepdims=True)), 0.0)
        p = e / jnp.maximum(jnp.sum(e, axis=1, keepdims=True), 1e-30)
        o_ref[0, g] = _dot(p.astype(BF16), vcc_ref[0][:, cols].astype(BF16))
        imp = p[0:t]
        for r in range(1, GROUP):
            imp = imp + p[r * t:(r + 1) * t]
        imp = jnp.concatenate([imp, jnp.zeros((t, width - nbp), F32)], axis=1)
        sel = _topk_mask(imp, cur, nbp + 1)
        sel_ref[0, g] = jnp.where(sel, 1.0, 0.0)


def _nsa_sample_cmp(q3, kcc, vcc, bias_cmp, past):
    n, t, _ = q3.shape
    nbp = kcc.shape[1]
    width = -(-(nbp + 1) // LANES) * LANES
    return pl.pallas_call(
        functools.partial(_nsa_sample_cmp_kernel, past=past),
        out_shape=[jax.ShapeDtypeStruct((n, N_KV_NSA, GROUP * t, HEAD_DIM), F32),
                   jax.ShapeDtypeStruct((n, N_KV_NSA, t, width), F32)],
        grid=(n,),
        in_specs=[
            pl.BlockSpec((1, t, NSA_WIDTH), lambda i: (i, 0, 0)),
            pl.BlockSpec((1, nbp, KV_WIDTH), lambda i: (i, 0, 0)),
            pl.BlockSpec((1, nbp, KV_WIDTH), lambda i: (i, 0, 0)),
            pl.BlockSpec(bias_cmp.shape, lambda i: (0, 0, 0)),
        ],
        out_specs=[pl.BlockSpec((1, N_KV_NSA, GROUP * t, HEAD_DIM), lambda i: (i, 0, 0, 0)),
                   pl.BlockSpec((1, N_KV_NSA, t, width), lambda i: (i, 0, 0, 0))],
        compiler_params=_cparams(("parallel",)),
        name="nsa_sample_cmp",
    )(q3, kcc, vcc, bias_cmp)


KEY_CHUNK = 1024


def _sample_key_chunk(n_pages):
    return max(1, KEY_CHUNK // n_pages) * n_pages


def _nsa_sample_attn_kernel(pt_ref, q_ref, gate_ref, ocmp_ref, sel_ref, ksn_ref, vsn_ref, kwn_ref, vwn_ref,
                            cwk_ref, cwv_ref, pk_ref, pv_ref, bs_ref, bn_ref, bw_ref, ex_ref,
                            o_ref, bk_ref, bv_ref, sk_ref, sv_ref, selx_ref, *, n_pages, key_chunk):
    slot = _paged_prefetch(pt_ref, (pk_ref, pv_ref), (bk_ref, bv_ref), (sk_ref, sv_ref), n_pages,
                           _page_copies_by_offset)
    t = q_ref.shape[1]
    rows = N_KV_NSA * GROUP * t
    past = n_pages * PAGE_SIZE
    nbp = past // SLC_BLOCK
    wb = cwk_ref.shape[1]
    groups = range(N_KV_NSA)
    q = q_ref[0] * ATTN_SCALE
    gates = jax.nn.sigmoid(gate_ref[0])
    zeros = jnp.zeros((GROUP * t, HEAD_DIM), F32)

    def own_lanes(x, g):
        return jnp.concatenate([x, zeros] if g == 0 else [zeros, x], axis=1)

    qa = jnp.concatenate([own_lanes(_stack_heads(q, g), g) for g in groups], axis=0).astype(BF16)
    sel = jnp.concatenate([sel_ref[0, g] for g in groups for _ in range(GROUP)], axis=0)
    selx_ref[...] = _dot(sel[:, :nbp].astype(BF16), ex_ref[...])
    tok = jnp.concatenate([_row_token(t)] * N_KV_NSA, axis=0)
    jn = lax.broadcasted_iota(jnp.int32, (1, NEW_PAD), 1)
    new_mask = jn <= tok
    jw = lax.broadcasted_iota(jnp.int32, (1, wb), 1)
    wdist = wb + tok - jw
    win_mask = (wdist >= 0) & (wdist <= WINDOW)
    bias_new = bn_ref[:, :NEW_PAD]
    pad = jnp.zeros((NEW_PAD - t, KV_WIDTH), F32)

    def new_rows(ref):
        return jnp.concatenate([ref[0], pad], axis=0).astype(BF16)

    per_chunk = key_chunk // n_pages
    qg = [_stack_heads(q, g).astype(BF16) for g in groups]

    def chunk_rows(buf_ref, c, g):
        return jnp.concatenate([buf_ref[slot, (c * per_chunk + i) * N_KV_NSA + g] for i in range(per_chunk)],
                               axis=0).astype(BF16)

    def slc_step(c, carry):
        off = pl.multiple_of(c * key_chunk, key_chunk)
        s = jnp.concatenate([_dot_nt(qg[g], chunk_rows(bk_ref, c, g)) for g in groups], axis=0)
        s = s + bs_ref[:, pl.ds(off, key_chunk)]

        def pv(p):
            return jnp.concatenate([own_lanes(_dot(p[g * GROUP * t:(g + 1) * GROUP * t], chunk_rows(bv_ref, c, g)), g)
                                    for g in groups], axis=0)

        return _softmax_step(carry, s, selx_ref[:, pl.ds(off, key_chunk)] > 0.5, pv)

    carry = lax.fori_loop(0, past // key_chunk, slc_step, _softmax_init(rows, KV_WIDTH))
    s = _dot_nt(qa, new_rows(ksn_ref)) + bias_new
    carry = _softmax_step(carry, s, new_mask & (sel[:, nbp:nbp + 1] > 0.5), new_rows(vsn_ref))
    o_slc = _softmax_done(carry)

    s = _dot_nt(qa, cwk_ref[0].astype(BF16)) + bw_ref[...]
    carry = _softmax_step(_softmax_init(rows, KV_WIDTH), s, win_mask, cwv_ref[0].astype(BF16))
    s = _dot_nt(qa, new_rows(kwn_ref)) + bias_new
    carry = _softmax_step(carry, s, new_mask, new_rows(vwn_ref))
    o_win = _softmax_done(carry)

    def gate_col(branch):
        return _stack_cols(gates, [branch * N_HEADS_NSA + h for h in range(N_HEADS_NSA)])

    o_cmp = jnp.concatenate([own_lanes(ocmp_ref[0, g], g) for g in groups], axis=0)
    o = gate_col(0) * o_cmp + gate_col(1) * o_slc + gate_col(2) * o_win
    for h in range(N_HEADS_NSA):
        g = h // GROUP
        o_ref[0, :, h * HEAD_DIM:(h + 1) * HEAD_DIM] = o[h * t:(h + 1) * t, g * HEAD_DIM:(g + 1) * HEAD_DIM]


def _nsa_sample_attn(page_table, q3, gate3, o_cmp, sel, ksn, vsn, kwn, vwn, cwk, cwv, pool_k, pool_v,
                     bias_slc, bias_new, bias_win, expand):
    n, n_pages = page_table.shape
    t = q3.shape[1]
    past = n_pages * PAGE_SIZE
    wb = cwk.shape[1]
    width = sel.shape[3]
    key_chunk = _sample_key_chunk(n_pages)
    assert past % key_chunk == 0 and n_pages % SUBLANES == 0 and t <= NEW_PAD
    per_seq = lambda shape: pl.BlockSpec((1,) + shape, lambda i, pt: (i,) + (0,) * len(shape))
    full = lambda a: pl.BlockSpec(a.shape, lambda i, pt: (0,) * a.ndim)
    anyspec = pl.BlockSpec(memory_space=pl.ANY)
    grid_spec = pltpu.PrefetchScalarGridSpec(
        num_scalar_prefetch=1,
        grid=(n,),
        in_specs=[per_seq((t, NSA_WIDTH)), per_seq((t, LANES)), per_seq((N_KV_NSA, GROUP * t, HEAD_DIM)),
                  per_seq((N_KV_NSA, t, width)),
                  per_seq((t, KV_WIDTH)), per_seq((t, KV_WIDTH)), per_seq((t, KV_WIDTH)), per_seq((t, KV_WIDTH)),
                  per_seq((wb, KV_WIDTH)), per_seq((wb, KV_WIDTH)), anyspec, anyspec,
                  full(bias_slc), full(bias_new), full(bias_win), full(expand)],
        out_specs=per_seq((t, NSA_WIDTH)),
        scratch_shapes=[pltpu.VMEM((2, PAGE_SIZE * N_KV_NSA, n_pages, HEAD_DIM), F32),
                        pltpu.VMEM((2, PAGE_SIZE * N_KV_NSA, n_pages, HEAD_DIM), F32),
                        pltpu.SemaphoreType.DMA((2,)), pltpu.SemaphoreType.DMA((2,)),
                        pltpu.VMEM((N_KV_NSA * GROUP * t, past), F32)],
    )
    return pl.pallas_call(
        functools.partial(_nsa_sample_attn_kernel, n_pages=n_pages, key_chunk=key_chunk),
        out_shape=jax.ShapeDtypeStruct((n, t, NSA_WIDTH), F32),
        grid_spec=grid_spec,
        compiler_params=_cparams(("arbitrary",)),
        name="nsa_sample_attn",
    )(page_table, q3, gate3, o_cmp, sel, ksn, vsn, kwn, vwn, cwk, cwv, pool_k, pool_v,
      bias_slc, bias_new, bias_win, expand)


def _segsum(x, ones_ref):
    hi = x.astype(BF16)
    lo = (x - hi.astype(F32)).astype(BF16)
    return _dot(hi, ones_ref[...]) + _dot(lo, ones_ref[...])


def _softplus(x):
    return jnp.maximum(x, 0.0) + jnp.log(1.0 + jnp.exp(-jnp.abs(x)))


def _rwkv_pre_kernel(p_ref, prev_ref, mu_ref, w0_ref, w2_ref, a0_ref, a2_ref, g2_ref, kk_ref, ka_ref, ones_ref,
                     r_ref, k_ref, v_ref, kkn_ref, kka_ref, d_ref, g_ref, last_ref, buf_ref):
    ti = pl.program_id(1)
    tt = p_ref.shape[1]
    p = p_ref[0]
    first = jnp.where(ti == 0, prev_ref[0], last_ref[...])
    last_ref[...] = p[tt - 1:tt]
    buf_ref[SUBLANES - 1:SUBLANES, :] = first
    buf_ref[SUBLANES:SUBLANES + tt, :] = p
    prev = buf_ref[SUBLANES - 1:SUBLANES - 1 + tt, :]
    xs = p + (prev - p) * mu_ref[...]
    W = RWKV_WIDTH
    r, k, v = xs[:, 0:W], xs[:, W:2 * W], xs[:, 2 * W:3 * W]
    c = 3 * W
    xw = xs[:, c:c + DECAY_LORA]
    xa = xs[:, c + DECAY_LORA:c + DECAY_LORA + AAA_LORA]
    xg = xs[:, c + DECAY_LORA + AAA_LORA:]
    w = -_softplus(-(w0_ref[...] + _dot(jnp.tanh(xw).astype(BF16), w2_ref[...]))) - 0.5
    a = jax.nn.sigmoid(a0_ref[...] + _dot(xa.astype(BF16), a2_ref[...]))
    kk = k * kk_ref[...]
    kk = kk / jnp.maximum(jnp.sqrt(_segsum(kk * kk, ones_ref)), 1e-12)
    r_ref[0] = r
    k_ref[0] = k * (1.0 + (a - 1.0) * ka_ref[...])
    v_ref[0] = v
    kkn_ref[0] = kk
    kka_ref[0] = kk * a
    d_ref[0] = -jnp.exp(w)
    g_ref[0] = _dot(jax.nn.sigmoid(xg).astype(BF16), g2_ref[...])


def _rwkv_pre(p3, prev, mu, w0, w2, a0, a2, g2, k_k, k_a, ones_bd):
    n, t, c = p3.shape
    tt = min(t, 256)
    assert t % tt == 0
    row = lambda a: a.reshape(1, -1)
    full = lambda a: pl.BlockSpec(a.shape, lambda b, i: (0,) * a.ndim)
    params = [row(mu), row(w0), w2, row(a0), a2, g2, row(k_k), row(k_a), ones_bd]
    out_spec = pl.BlockSpec((1, tt, RWKV_WIDTH), lambda b, i: (b, i, 0))
    return pl.pallas_call(
        _rwkv_pre_kernel,
        out_shape=[jax.ShapeDtypeStruct((n, t, RWKV_WIDTH), F32)] * 7,
        grid=(n, t // tt),
        in_specs=[pl.BlockSpec((1, tt, c), lambda b, i: (b, i, 0)),
                  pl.BlockSpec((1, 1, c), lambda b, i: (b, 0, 0))] + [full(a) for a in params],
        out_specs=[out_spec] * 7,
        scratch_shapes=[pltpu.VMEM((1, c), F32), pltpu.VMEM((SUBLANES + tt, c), F32)],
        compiler_params=_cparams(("parallel", "arbitrary")),
        name="rwkv_pre",
    )(p3, prev.reshape(n, 1, c), *params)


def _rwkv_scan_kernel(r_ref, k_ref, v_ref, kk_ref, kka_ref, ld_ref, s0_ref, y_ref, sout_ref, s_ref):
    ci = pl.program_id(1)
    t_in = r_ref.shape[1]
    C = TC

    @pl.when(ci == 0)
    def _():
        s_ref[...] = s0_ref[0]

    def rows(ref):
        x = ref[0]
        if t_in < C:
            x = jnp.concatenate([x, jnp.zeros((C - t_in, x.shape[1]), F32)], axis=0)
        return x

    ld = rows(ld_ref)
    tri = jnp.where(lax.broadcasted_iota(jnp.int32, (C, C), 1) <= lax.broadcasted_iota(jnp.int32, (C, C), 0),
                    1.0, 0.0).astype(BF16)
    hi = ld.astype(BF16)
    rem = ld - hi.astype(F32)
    mid = rem.astype(BF16)
    lo = (rem - mid.astype(F32)).astype(BF16)
    cum = _dot(tri, hi) + _dot(tri, mid) + _dot(tri, lo)
    p_inc = jnp.exp(cum)
    p_inv = jnp.exp(-cum)
    a_t = -rows(kk_ref) * jnp.exp(cum - ld)
    b_t = rows(kka_ref) * p_inv
    k_t = rows(k_ref) * p_inv
    r_t = rows(r_ref) * p_inc
    v = rows(v_ref)
    p_end = p_inc[C - 1:C]

    i2 = lax.broadcasted_iota(jnp.int32, (2 * C, 2 * C), 0)
    j2 = lax.broadcasted_iota(jnp.int32, (2 * C, 2 * C), 1)
    src = j2 & (C - 1)
    keep = src < jnp.where(i2 < C, i2, i2 - C + 1)

    heads = range(N_HEADS_RWKV)
    cols = [slice(h * HEAD_DIM, (h + 1) * HEAD_DIM) for h in heads]
    ar = [jnp.concatenate([a_t[:, c], r_t[:, c]], axis=0).astype(BF16) for c in cols]
    bk = [jnp.concatenate([b_t[:, c], k_t[:, c]], axis=0).astype(BF16) for c in cols]
    vh = [v[:, c].astype(BF16) for c in cols]
    s0 = [s_ref[h] for h in heads]
    quad = [jnp.where(keep, _dot_nt(ar[h], bk[h]), 0.0).astype(BF16) for h in heads]
    base = [_dot_nt(ar[h], s0[h].astype(BF16)) for h in heads]
    x = [base[h][:C] + _dot(quad[h][:C, C:], vh[h]) for h in heads]
    power = [quad[h][:C, :C] for h in heads]
    levels = C.bit_length() - 1
    for lvl in range(levels):
        x = [x[h] + _dot(power[h], x[h].astype(BF16)) for h in heads]
        if lvl + 1 < levels:
            power = [_dot(power[h], power[h]).astype(BF16) for h in heads]
    uv = [jnp.concatenate([x[h].astype(BF16), vh[h]], axis=0) for h in heads]
    ys = [base[h][C:] + _dot(quad[h][C:, :], uv[h]) for h in heads]
    for h in heads:
        upd = lax.dot_general(uv[h], bk[h], (((0,), (0,)), ((), ())), preferred_element_type=F32)
        s_ref[h] = (s0[h] + upd) * p_end[:, cols[h]]
    y = jnp.concatenate(ys, axis=1)
    y_ref[0] = y[:t_in]

    @pl.when(ci == pl.num_programs(1) - 1)
    def _():
        sout_ref[0] = s_ref[...]


def _rwkv_scan(r, k, v, kk, kka, d, s0):
    n, t, w = r.shape
    tc = min(t, TC)
    assert t % tc == 0 and TC & (TC - 1) == 0
    seq_spec = pl.BlockSpec((1, tc, w), lambda b, i: (b, i, 0))
    st_spec = pl.BlockSpec((1, N_HEADS_RWKV, HEAD_DIM, HEAD_DIM), lambda b, i: (b, 0, 0, 0))
    return pl.pallas_call(
        _rwkv_scan_kernel,
        out_shape=[jax.ShapeDtypeStruct((n, t, w), F32),
                   jax.ShapeDtypeStruct((n, N_HEADS_RWKV, HEAD_DIM, HEAD_DIM), F32)],
        grid=(n, t // tc),
        in_specs=[seq_spec] * 6 + [st_spec],
        out_specs=[seq_spec, st_spec],
        scratch_shapes=[pltpu.VMEM((N_HEADS_RWKV, HEAD_DIM, HEAD_DIM), F32)],
        compiler_params=_cparams(("parallel", "arbitrary")),
        name="rwkv_scan",
    )(r, k, v, kk, kka, d, s0)


def _mix_out_kernel(x_ref, nsa_ref, y_ref, r_ref, k_ref, v_ref, g_ref, rk_ref, lnw_ref, lnb_ref, ones_ref,
                    wa_ref, wb_ref, o_ref):
    y = y_ref[...]
    mean = _segsum(y, ones_ref) * (1.0 / HEAD_DIM)
    yc = y - mean
    var = _segsum(yc * yc, ones_ref) * (1.0 / HEAD_DIM)
    yn = yc * lax.rsqrt(var + GN_EPS) * lnw_ref[...] + lnb_ref[...]
    bonus = _segsum(r_ref[...] * k_ref[...] * rk_ref[...], ones_ref) * v_ref[...]
    rw = ((yn + bonus) * g_ref[...]).astype(BF16)
    o_ref[...] = x_ref[...] + _dot(nsa_ref[...].astype(BF16), wa_ref[...]) + _dot(rw, wb_ref[...])


def _mix_out(x, o_nsa, y, r, k, v, g, r_k, ln_w, ln_b, ones_bd, wa, wb):
    m, d = x.shape
    tm = min(m, 512)
    assert m % tm == 0
    row = lambda a: a.reshape(1, -1)
    tile = lambda w: pl.BlockSpec((tm, w), lambda i: (i, 0))
    full = lambda a: pl.BlockSpec(a.shape, lambda i: (0,) * a.ndim)
    params = [row(r_k), row(ln_w), row(ln_b), ones_bd, wa, wb]
    return pl.pallas_call(
        _mix_out_kernel,
        out_shape=jax.ShapeDtypeStruct((m, d), F32),
        grid=(m // tm,),
        in_specs=[tile(d), tile(NSA_WIDTH)] + [tile(RWKV_WIDTH)] * 5 + [full(a) for a in params],
        out_specs=tile(d),
        compiler_params=_cparams(("parallel",)),
        name="mix_out",
    )(x, o_nsa, y, r, k, v, g, *params)


def _block_diag2(w):
    z = jnp.zeros_like(w)
    return jnp.concatenate([jnp.concatenate([w, z], axis=-1), jnp.concatenate([z, w], axis=-1)], axis=-2)


def _compress_weights(pe, w1, w2):
    return (jnp.concatenate([pe] * N_KV_NSA, axis=1), _block_diag2(w1).astype(BF16), _block_diag2(w2).astype(BF16))


def kernel(x_prompt, x_sample, cache_cmp_k, cache_cmp_v, cache_slc_k, cache_slc_v, cache_win_k, cache_win_v,
           state_wkv, state_shift, page_table, rel_bias_table, ffn1_norm, ffn1_wg, ffn1_wu, ffn1_wd, mix_norm,
           w_in, cmp_pe_k, cmp_w1_k, cmp_w2_k, cmp_pe_v, cmp_w1_v, cmp_w2_v, shift_mu, decay_w0, decay_w2,
           aaa_a0, aaa_a2, gate_g2, k_k, k_a, r_k, ln_x_w, ln_x_b, w_out, ffn2_norm, ffn2_wg, ffn2_wu, ffn2_wd,
           final_norm):
    bp, seq, d = x_prompt.shape
    bs, ts, _ = x_sample.shape
    depth = w_in.shape[0]
    n_pages = page_table.shape[1]
    past = n_pages * PAGE_SIZE
    n_phys = cache_cmp_k.shape[1]
    wbuf = cache_win_k.shape[2]
    assert seq % QT == 0 and seq >= WINDOW and ts < CMP_BLOCK and wbuf == WINDOW
    nq = seq // QT
    nb = seq // SLC_BLOCK
    nbp = past // SLC_BLOCK
    nsa_main = NSA_WIDTH + 6 * KV_WIDTH
    nsa_cols = nsa_main + N_GATES

    bias_tiles = _bias_tiles(rel_bias_table)
    bias_cmp_p = _bias_rows(rel_bias_table, q0=0, rows=seq, n_keys=nb, key_first=CMP_BLOCK - 1, key_stride=CMP_BLOCK)
    bias_cmp_p = bias_cmp_p.reshape(N_KV_NSA, GROUP, nq, QT, nb).transpose(0, 2, 4, 1, 3).reshape(
        N_KV_NSA, nq, nb, GROUP * QT)
    bias_cmp_s = _bias_rows(rel_bias_table, q0=past, rows=ts, n_keys=nbp, key_first=CMP_BLOCK - 1,
                            key_stride=CMP_BLOCK).reshape(N_KV_NSA, GROUP * ts, nbp)
    stacked = lambda b: b.reshape(N_HEADS_NSA * ts, b.shape[-1])
    key_order = (jnp.arange(PAGE_SIZE, dtype=jnp.int32)[:, None]
                 + jnp.arange(n_pages, dtype=jnp.int32)[None, :] * PAGE_SIZE).reshape(-1)
    bias_slc_s = stacked(_bias_rows(rel_bias_table, q0=past, rows=ts, n_keys=past, key_first=0,
                                    key_stride=1))[:, key_order]
    bias_new_s = stacked(_bias_rows(rel_bias_table, q0=past, rows=ts, n_keys=LANES, key_first=past, key_stride=1))
    bias_win_s = stacked(_bias_rows(rel_bias_table, q0=past, rows=ts, n_keys=wbuf,
                                    key_first=past - wbuf, key_stride=1))
    expand_s = (key_order[None, :] // SLC_BLOCK == jnp.arange(nbp, dtype=jnp.int32)[:, None]).astype(BF16)
    head_of = jnp.arange(RWKV_WIDTH, dtype=jnp.int32) // HEAD_DIM
    ones_bd = (head_of[:, None] == head_of[None, :]).astype(BF16)

    xp = x_prompt.reshape(bp * seq, d)
    xs = x_sample.reshape(bs * ts, d)
    states = []
    for l in range(depth):
        bf = lambda a: a[l].astype(BF16)
        w_in_l = w_in[l]
        w_pad = jnp.concatenate(
            [w_in_l[:, :nsa_main], w_in_l[:, nsa_main:nsa_cols], jnp.zeros((d, LANES - N_GATES), F32),
             w_in_l[:, nsa_cols:]], axis=1).astype(BF16)
        cw_k = _compress_weights(cmp_pe_k[l], cmp_w1_k[l], cmp_w2_k[l])
        cw_v = _compress_weights(cmp_pe_v[l], cmp_w1_v[l], cmp_w2_v[l])
        wa, wb = w_out[l][:NSA_WIDTH].astype(BF16), w_out[l][NSA_WIDTH:].astype(BF16)
        rw_pre = (shift_mu[l], decay_w0[l], bf(decay_w2), aaa_a0[l], bf(aaa_a2), bf(gate_g2), k_k[l], k_a[l], ones_bd)
        ffn1 = (ffn1_norm[l], bf(ffn1_wg), bf(ffn1_wu), bf(ffn1_wd), final_norm)
        ffn2 = (ffn2_norm[l], bf(ffn2_wg), bf(ffn2_wu), bf(ffn2_wd), final_norm)
        last = l == depth - 1

        xp = _ffn_half(xp, *ffn1, final=False)
        xs = _ffn_half(xs, *ffn1, final=False)

        q, kc, vc, ks, vs, kw, vw, gate, zr = _proj(xp, mix_norm[l], w_pad)
        kcc = _compress(kc, *cw_k)
        vcc = _compress(vc, *cw_v)
        o_nsa = _nsa_prompt(q, gate, ks, vs, kw, vw, kcc, vcc, bias_tiles, bias_cmp_p, bp, seq)
        zr3 = zr.reshape(bp, seq, RWKV_COLS)
        r, k2, v, kk, kka, dec, gg = _rwkv_pre(zr3, jnp.zeros((bp, RWKV_COLS), F32), *rw_pre)
        y, wkv_p = _rwkv_scan(r, k2, v, kk, kka, dec,
                              jnp.zeros((bp, N_HEADS_RWKV, HEAD_DIM, HEAD_DIM), F32))
        flat = lambda a: a.reshape(-1, RWKV_WIDTH)
        xp = _mix_out(xp, o_nsa, flat(y), flat(r), flat(k2), flat(v), flat(gg), r_k[l], ln_x_w[l], ln_x_b[l],
                      ones_bd, wa, wb)
        kv5 = lambda a, n_, t_: a.reshape(n_, t_, N_KV_NSA, HEAD_DIM)
        p_state = (kv5(kc, bp, seq), kv5(vc, bp, seq), kv5(ks, bp, seq), kv5(vs, bp, seq),
                   kv5(kw, bp, seq)[:, seq - WINDOW:], kv5(vw, bp, seq)[:, seq - WINDOW:],
                   wkv_p, zr3[:, seq - 1])

        q, kc, vc, ks, vs, kw, vw, gate, zr = _proj(xs, mix_norm[l], w_pad)
        rows_of = lambda c: c[l].reshape(n_phys * PAGE_SIZE * N_KV_NSA, HEAD_DIM)
        raw = lambda pe, w1, w2: (pe[l], w1[l].astype(BF16), w2[l].astype(BF16))
        kcc, vcc = _compress_paged(page_table, rows_of(cache_cmp_k), rows_of(cache_cmp_v),
                                   raw(cmp_pe_k, cmp_w1_k, cmp_w2_k), raw(cmp_pe_v, cmp_w1_v, cmp_w2_v))
        seq3 = lambda a: a.reshape(bs, ts, a.shape[-1])
        o_cmp, sel = _nsa_sample_cmp(seq3(q), kcc, vcc, bias_cmp_s, past)
        cwk = cache_win_k[l].reshape(bs, wbuf, KV_WIDTH)
        cwv = cache_win_v[l].reshape(bs, wbuf, KV_WIDTH)
        o_nsa = _nsa_sample_attn(page_table, seq3(q), seq3(gate), o_cmp, sel, seq3(ks), seq3(vs), seq3(kw),
                                 seq3(vw), cwk, cwv, rows_of(cache_slc_k), rows_of(cache_slc_v),
                                 bias_slc_s, bias_new_s, bias_win_s, expand_s)
        zr3 = seq3(zr)
        r, k2, v, kk, kka, dec, gg = _rwkv_pre(zr3, state_shift[l], *rw_pre)
        y, wkv_s = _rwkv_scan(r, k2, v, kk, kka, dec, state_wkv[l])
        xs = _mix_out(xs, o_nsa.reshape(bs * ts, NSA_WIDTH), flat(y), flat(r), flat(k2), flat(v), flat(gg),
                      r_k[l], ln_x_w[l], ln_x_b[l], ones_bd, wa, wb)
        win = lambda cache, new: jnp.concatenate([cache[l], kv5(new, bs, ts)], axis=1)[:, ts:]
        s_state = (kv5(kc, bs, ts), kv5(vc, bs, ts), kv5(ks, bs, ts), kv5(vs, bs, ts),
                   win(cache_win_k, kw), win(cache_win_v, vw), wkv_s, zr3[:, ts - 1])

        xp = _ffn_half(xp, *ffn2, final=last)
        xs = _ffn_half(xs, *ffn2, final=last)
        states.append(p_state + s_state)

    stacked_states = [jnp.stack(z) for z in zip(*states)]
    return (xp.reshape(bp, seq, d), xs.reshape(bs, ts, d), *stacked_states)
```

```python
import functools
import math

import jax
import jax.numpy as jnp
import numpy as np
from jax import lax
from jax.experimental import pallas as pl
from jax.experimental.pallas import tpu as pltpu

F32 = jnp.float32
BF16 = jnp.bfloat16

HEAD_DIM = 64
N_HEADS_NSA = 8
N_KV_NSA = 2
GROUP = N_HEADS_NSA // N_KV_NSA
N_HEADS_RWKV = 8
NSA_WIDTH = N_HEADS_NSA * HEAD_DIM
RWKV_WIDTH = N_HEADS_RWKV * HEAD_DIM
KV_WIDTH = N_KV_NSA * HEAD_DIM
CMP_BLOCK = 64
SLC_BLOCK = 64
TOP_K_BLOCKS = 16
WINDOW = 512
PAGE_SIZE = 128
N_BUCKETS = 32
MAX_DISTANCE = 128
DECAY_LORA = 64
AAA_LORA = 64
GATE_LORA = 128
NORM_EPS = 1e-6
GN_EPS = 64e-5
ATTN_SCALE = HEAD_DIM ** -0.5
FORCE_SCORE = 1e4
NEG_INF = -1e30
M_FLOOR = -1e29
RWKV_COLS = 3 * RWKV_WIDTH + DECAY_LORA + AAA_LORA + GATE_LORA
N_GATES = 3 * N_HEADS_NSA

LANES = 128
SUBLANES = 8
VMEM_LIMIT = 56 * 1024 * 1024

QT = 128
TC = 64


def _bucket_thresholds():
    max_exact = N_BUCKETS // 2
    thr = list(range(max_exact + 1))
    for b in range(max_exact + 1, N_BUCKETS):
        x = max_exact * (MAX_DISTANCE / max_exact) ** ((b - max_exact) / (N_BUCKETS - max_exact))
        thr.append(int(math.ceil(x - 1e-9)))
    return thr


BUCKET_THR = _bucket_thresholds()


def _cparams(sem):
    return pltpu.CompilerParams(dimension_semantics=sem, vmem_limit_bytes=VMEM_LIMIT)


def _dot(a, b):
    return jnp.dot(a, b, preferred_element_type=F32)


def _dot_nt(a, b):
    return lax.dot_general(a, b, (((1,), (1,)), ((), ())), preferred_element_type=F32)


def _rms(x, g):
    return x * lax.rsqrt(jnp.mean(x * x, axis=-1, keepdims=True) + NORM_EPS) * g


def _ffn_kernel(x_ref, g_ref, wg_ref, wu_ref, wd_ref, gf_ref, o_ref, h_ref, acc_ref, *, final):
    j = pl.program_id(1)

    @pl.when(j == 0)
    def _():
        h_ref[...] = _rms(x_ref[...], g_ref[...]).astype(BF16)
        acc_ref[...] = jnp.zeros_like(acc_ref)

    h = h_ref[...]
    a = _dot(h, wg_ref[...])
    b = _dot(h, wu_ref[...])
    act = (a * jax.nn.sigmoid(a) * b).astype(BF16)
    acc_ref[...] += _dot(act, wd_ref[...])

    @pl.when(j == pl.num_programs(1) - 1)
    def _():
        y = x_ref[...] + 0.5 * acc_ref[...]
        if final:
            y = _rms(y, gf_ref[...])
        o_ref[...] = y


def _ffn_half(x, g, wg, wu, wd, gf, *, final):
    m, d = x.shape
    ff = wg.shape[1]
    tm = min(m, 1024)
    tf = 256 if ff % 256 == 0 else ff
    assert m % tm == 0 and ff % tf == 0
    return pl.pallas_call(
        functools.partial(_ffn_kernel, final=final),
        out_shape=jax.ShapeDtypeStruct((m, d), F32),
        grid=(m // tm, ff // tf),
        in_specs=[
            pl.BlockSpec((tm, d), lambda i, j: (i, 0)),
            pl.BlockSpec((1, d), lambda i, j: (0, 0)),
            pl.BlockSpec((d, tf), lambda i, j: (0, j)),
            pl.BlockSpec((d, tf), lambda i, j: (0, j)),
            pl.BlockSpec((tf, d), lambda i, j: (j, 0)),
            pl.BlockSpec((1, d), lambda i, j: (0, 0)),
        ],
        out_specs=pl.BlockSpec((tm, d), lambda i, j: (i, 0)),
        scratch_shapes=[pltpu.VMEM((tm, d), BF16), pltpu.VMEM((tm, d), F32)],
        compiler_params=_cparams(("parallel", "arbitrary")),
        name="ffn_half",
    )(x, g.reshape(1, d), wg, wu, wd, gf.reshape(1, d))


def _proj_kernel(x_ref, g_ref, w_ref, q_ref, kc_ref, vc_ref, ks_ref, vs_ref, kw_ref, vw_ref, gate_ref, zr_ref):
    h = _rms(x_ref[...], g_ref[...]).astype(BF16)
    z = _dot(h, w_ref[...])
    c = 0
    for ref in (q_ref, kc_ref, vc_ref, ks_ref, vs_ref, kw_ref, vw_ref, gate_ref, zr_ref):
        w = ref.shape[1]
        ref[...] = z[:, c:c + w]
        c += w


def _proj(x, g, w_pad):
    m, d = x.shape
    tm = min(m, 512)
    widths = (NSA_WIDTH,) + (KV_WIDTH,) * 6 + (LANES, RWKV_COLS)
    assert sum(widths) == w_pad.shape[1] and m % tm == 0
    return pl.pallas_call(
        _proj_kernel,
        out_shape=[jax.ShapeDtypeStruct((m, w), F32) for w in widths],
        grid=(m // tm,),
        in_specs=[
            pl.BlockSpec((tm, d), lambda i: (i, 0)),
            pl.BlockSpec((1, d), lambda i: (0, 0)),
            pl.BlockSpec(w_pad.shape, lambda i: (0, 0)),
        ],
        out_specs=[pl.BlockSpec((tm, w), lambda i: (i, 0)) for w in widths],
        compiler_params=_cparams(("parallel",)),
        name="in_proj",
    )(x, g.reshape(1, d), w_pad)


def _out_kernel(x_ref, a_ref, b_ref, wa_ref, wb_ref, o_ref):
    o_ref[...] = (x_ref[...] + _dot(a_ref[...].astype(BF16), wa_ref[...])
                  + _dot(b_ref[...].astype(BF16), wb_ref[...]))


def _out_proj(x, a, b, wa, wb):
    m, d = x.shape
    tm = min(m, 512)
    return pl.pallas_call(
        _out_kernel,
        out_shape=jax.ShapeDtypeStruct((m, d), F32),
        grid=(m // tm,),
        in_specs=[
            pl.BlockSpec((tm, d), lambda i: (i, 0)),
            pl.BlockSpec((tm, a.shape[1]), lambda i: (i, 0)),
            pl.BlockSpec((tm, b.shape[1]), lambda i: (i, 0)),
            pl.BlockSpec(wa.shape, lambda i: (0, 0)),
            pl.BlockSpec(wb.shape, lambda i: (0, 0)),
        ],
        out_specs=pl.BlockSpec((tm, d), lambda i: (i, 0)),
        compiler_params=_cparams(("parallel",)),
        name="out_proj",
    )(x, a, b, wa, wb)


def _bias_lookup(dist, table_ref, heads):
    vals = [jnp.full(dist.shape, table_ref[0, h], F32) for h in heads]
    for b in range(1, N_BUCKETS):
        ge = dist >= BUCKET_THR[b]
        vals = [jnp.where(ge, table_ref[b, h], v) for h, v in zip(heads, vals)]
    return vals


def _bias_tiles_kernel(table_ref, o_ref):
    j = lax.broadcasted_iota(jnp.int32, (QT, QT), 0)
    i = lax.broadcasted_iota(jnp.int32, (QT, QT), 1)
    for off in range(3):
        vals = _bias_lookup(off * QT + i - j, table_ref, range(N_HEADS_NSA))
        for h, v in enumerate(vals):
            g, r = divmod(h, GROUP)
            o_ref[g, off, :, r * QT:(r + 1) * QT] = v


def _bias_tiles(table):
    return pl.pallas_call(
        _bias_tiles_kernel,
        out_shape=jax.ShapeDtypeStruct((N_KV_NSA, 3, QT, GROUP * QT), F32),
        in_specs=[pl.BlockSpec(memory_space=pltpu.SMEM)],
        name="bias_tiles",
    )(table)


def _bias_rows_kernel(table_ref, o_ref, *, q0, key_first, key_stride):
    _, _, rc, L = o_ref.shape
    i = lax.broadcasted_iota(jnp.int32, (rc, L), 0) + pl.program_id(0) * rc
    j = lax.broadcasted_iota(jnp.int32, (rc, L), 1)
    dist = (q0 + i) - (key_first + key_stride * j)
    vals = _bias_lookup(dist, table_ref, range(N_HEADS_NSA))
    for h, v in enumerate(vals):
        g, r = divmod(h, GROUP)
        o_ref[g, r] = v


def _bias_rows(table, *, q0, rows, n_keys, key_first, key_stride):
    rc = min(rows, 256)
    assert rows % rc == 0
    return pl.pallas_call(
        functools.partial(_bias_rows_kernel, q0=q0, key_first=key_first, key_stride=key_stride),
        out_shape=jax.ShapeDtypeStruct((N_KV_NSA, GROUP, rows, n_keys), F32),
        grid=(rows // rc,),
        in_specs=[pl.BlockSpec(memory_space=pltpu.SMEM)],
        out_specs=pl.BlockSpec((N_KV_NSA, GROUP, rc, n_keys), lambda i: (0, 0, i, 0)),
        compiler_params=_cparams(("parallel",)),
        name="bias_rows",
    )(table)


def _gelu_tanh(x):
    return 0.5 * x * (1.0 + jnp.tanh(math.sqrt(2.0 / math.pi) * (x + 0.044715 * (x * x * x))))


def _compress_blocks(load_rows, nblk, pe_ref, w1_ref, w2_ref):
    acc = jnp.zeros((nblk, KV_WIDTH), F32)
    for c in range(CMP_BLOCK):
        rows = load_rows(c) + pe_ref[c:c + 1, :]
        acc = acc + _dot(rows.astype(BF16), w1_ref[c])
    return _dot(_gelu_tanh(acc).astype(BF16), w2_ref[...])


def _compress_kernel(x_ref, pe_ref, w1_ref, w2_ref, o_ref):
    nblk = o_ref.shape[0]
    o_ref[...] = _compress_blocks(lambda c: x_ref[pl.ds(c, nblk, stride=CMP_BLOCK), :],
                                  nblk, pe_ref, w1_ref, w2_ref)


def _compress(x, pe2, w1bd, w2bd):
    rows = x.shape[0]
    nb = rows // CMP_BLOCK
    nblk = min(nb, 128)
    assert nb % nblk == 0
    return pl.pallas_call(
        _compress_kernel,
        out_shape=jax.ShapeDtypeStruct((nb, KV_WIDTH), F32),
        grid=(nb // nblk,),
        in_specs=[
            pl.BlockSpec((nblk * CMP_BLOCK, KV_WIDTH), lambda i: (i, 0)),
            pl.BlockSpec(pe2.shape, lambda i: (0, 0)),
            pl.BlockSpec(w1bd.shape, lambda i: (0, 0, 0)),
            pl.BlockSpec(w2bd.shape, lambda i: (0, 0)),
        ],
        out_specs=pl.BlockSpec((nblk, KV_WIDTH), lambda i: (i, 0)),
        compiler_params=_cparams(("parallel",)),
        name="compress",
    )(x, pe2, w1bd, w2bd)


def _page_copies_by_feature(pt_ref, n, pool_ref, buf_ref, sem_ref, slot, n_pages):
    return [pltpu.make_async_copy(pool_ref.at[pt_ref[n, p]], buf_ref.at[slot, :, :, p], sem_ref.at[slot])
            for p in range(n_pages)]


def _page_copies_by_page(pt_ref, n, pool_ref, buf_ref, sem_ref, slot, n_pages):
    return [pltpu.make_async_copy(pool_ref.at[pt_ref[n, p]], buf_ref.at[slot, p], sem_ref.at[slot])
            for p in range(n_pages)]


def _paged_prefetch(pt_ref, pools, bufs, sems, n_pages, _page_copies):
    n = pl.program_id(0)
    slot = n % 2

    def start(seq, slot_):
        for pool, buf, sem in zip(pools, bufs, sems):
            for cp in _page_copies(pt_ref, seq, pool, buf, sem, slot_, n_pages):
                cp.start()

    @pl.when(n == 0)
    def _():
        start(0, 0)

    @pl.when(n + 1 < pl.num_programs(0))
    def _():
        start(n + 1, 1 - slot)

    for pool, buf, sem in zip(pools, bufs, sems):
        for cp in _page_copies(pt_ref, n, pool, buf, sem, slot, n_pages):
            cp.wait()
    return slot


def _compress_paged_kernel(pt_ref, pk_ref, pv_ref, pek_ref, w1k_ref, w2k_ref, pev_ref, w1v_ref, w2v_ref,
                           ok_ref, ov_ref, bk_ref, bv_ref, sk_ref, sv_ref, *, n_pages):
    slot = _paged_prefetch(pt_ref, (pk_ref, pv_ref), (bk_ref, bv_ref), (sk_ref, sv_ref), n_pages,
                           _page_copies_by_feature)

    def compress(buf_ref, pe_ref, w1_ref, w2_ref):
        acc = [jnp.zeros((n_pages, PAGE_SIZE), F32) for _ in range(N_KV_NSA)]
        for d in range(HEAD_DIM):
            for g in range(N_KV_NSA):
                rows = buf_ref[slot, g, d] + pe_ref[d:d + 1, :]
                acc[g] = acc[g] + _dot(rows.astype(BF16), w1_ref[d])
        out = [_dot(_gelu_tanh(a).astype(BF16), w2_ref[...]) for a in acc]
        return jnp.concatenate([out[g][:, j * HEAD_DIM:(j + 1) * HEAD_DIM]
                                for j in range(PAGE_SIZE // CMP_BLOCK) for g in range(N_KV_NSA)], axis=1)

    ok_ref[0] = compress(bk_ref, pek_ref, w1k_ref, w2k_ref)
    ov_ref[0] = compress(bv_ref, pev_ref, w1v_ref, w2v_ref)


def _compress_paged(page_table, pool_k, pool_v, cw_k, cw_v):
    n, n_pages = page_table.shape
    past = n_pages * PAGE_SIZE
    nblk = past // CMP_BLOCK
    per_page = PAGE_SIZE // CMP_BLOCK
    assert per_page == 2
    wspec = lambda a: pl.BlockSpec(a.shape, lambda i, pt: (0,) * a.ndim)
    grid_spec = pltpu.PrefetchScalarGridSpec(
        num_scalar_prefetch=1,
        grid=(n,),
        in_specs=[pl.BlockSpec(memory_space=pl.ANY), pl.BlockSpec(memory_space=pl.ANY)]
                 + [wspec(a) for a in cw_k + cw_v],
        out_specs=[pl.BlockSpec((1, n_pages, per_page * KV_WIDTH), lambda i, pt: (i, 0, 0))] * 2,
        scratch_shapes=[pltpu.VMEM((2, N_KV_NSA, HEAD_DIM, n_pages, PAGE_SIZE), F32),
                        pltpu.VMEM((2, N_KV_NSA, HEAD_DIM, n_pages, PAGE_SIZE), F32),
                        pltpu.SemaphoreType.DMA((2,)), pltpu.SemaphoreType.DMA((2,))],
    )
    kcc, vcc = pl.pallas_call(
        functools.partial(_compress_paged_kernel, n_pages=n_pages),
        out_shape=[jax.ShapeDtypeStruct((n, n_pages, per_page * KV_WIDTH), F32)] * 2,
        grid_spec=grid_spec,
        compiler_params=_cparams(("arbitrary",)),
        name="compress_paged",
    )(page_table, pool_k, pool_v, *cw_k, *cw_v)
    return kcc.reshape(n, nblk, KV_WIDTH), vcc.reshape(n, nblk, KV_WIDTH)


def _stack_heads(x, g):
    return jnp.concatenate(
        [x[:, (g * GROUP + r) * HEAD_DIM:(g * GROUP + r + 1) * HEAD_DIM] for r in range(GROUP)], axis=0)


def _stack_cols(x, cols):
    return jnp.concatenate([x[:, c:c + 1] for c in cols], axis=0)


def _softmax_step(carry, s, mask, v):
    m, l, acc = carry
    s = jnp.where(mask, s, NEG_INF)
    m_new = jnp.maximum(m, jnp.max(s, axis=1, keepdims=True))
    p = jnp.where(mask, jnp.exp(s - m_new), 0.0)
    alpha = jnp.exp(m - m_new)
    l = alpha * l + jnp.sum(p, axis=1, keepdims=True)
    pv = v(p.astype(BF16)) if callable(v) else _dot(p.astype(BF16), v)
    return m_new, l, alpha * acc + pv


def _softmax_init(rows, width):
    return (jnp.full((rows, 1), NEG_INF, F32), jnp.zeros((rows, 1), F32), jnp.zeros((rows, width), F32))


def _softmax_done(carry):
    _, l, acc = carry
    return acc / jnp.maximum(l, 1e-30)


def _topk_mask(imp, cur, n_blocks):
    blk = lax.broadcasted_iota(jnp.int32, imp.shape, 1)
    forced = (blk == 0) | (blk == cur) | (blk == cur - 1)
    score = jnp.where(blk <= cur, jnp.where(forced, FORCE_SCORE, imp), -FORCE_SCORE)
    rank = jnp.zeros(imp.shape, jnp.int32)
    for b in range(n_blocks):
        col = score[:, b:b + 1]
        ahead = (col > score) | ((col == score) & (blk > b))
        rank = rank + jnp.where(ahead, 1, 0)
    return (rank < TOP_K_BLOCKS) & (blk <= cur)


def _topk_mask_t(imp, cur, n_blocks):
    blk = lax.broadcasted_iota(jnp.int32, imp.shape, 0)
    forced = (blk == 0) | (blk == cur) | (blk == cur - 1)
    score = jnp.where(blk <= cur, jnp.where(forced, FORCE_SCORE, imp), -FORCE_SCORE)
    rank = jnp.zeros(imp.shape, jnp.int32)
    for b in range(n_blocks):
        row = score[b:b + 1, :]
        rank = rank + jnp.where(row > score, 1, jnp.where(row == score, jnp.where(blk > b, 1, 0), 0))
    return jnp.where(rank < TOP_K_BLOCKS, jnp.where(blk <= cur, 1.0, 0.0), 0.0)


def _nsa_prompt_kernel(q_ref, gate_ref, ks_ref, vs_ref, kw_ref, vw_ref, kcc_ref, vcc_ref, bt_ref, bc_ref,
                       o_ref, sel_ref, m_ref, acc_ref, pv_ref):
    qi = pl.program_id(1)
    seq = ks_ref.shape[1]
    nb = seq // SLC_BLOCK
    rows = GROUP * QT
    groups = range(N_KV_NSA)
    q = q_ref[...] * ATTN_SCALE
    gates_t = jax.nn.sigmoid(gate_ref[...]).T
    tpos_row = qi * QT + lax.broadcasted_iota(jnp.int32, (1, QT), 1)
    lane = lax.broadcasted_iota(jnp.int32, (1, KV_WIDTH), 1)
    own = [(lane // HEAD_DIM) == g for g in groups]
    zeros = jnp.zeros((rows, HEAD_DIM), F32)
    qg = []
    for g in groups:
        qs = _stack_heads(q, g)
        qg.append(jnp.concatenate([qs, zeros] if g == 0 else [zeros, qs], axis=1).astype(BF16))

    blk = lax.broadcasted_iota(jnp.int32, (nb, 1), 0)
    tpos_stacked = qi * QT + (lax.broadcasted_iota(jnp.int32, (1, rows), 1) & (QT - 1))
    readable = (blk + 1) * CMP_BLOCK - 1 <= tpos_stacked
    kcc = kcc_ref[0].astype(BF16)
    vcc = vcc_ref[0].astype(BF16)
    o_cmp = []
    for g in groups:
        s = jnp.where(readable, _dot_nt(kcc, qg[g]) + bc_ref[g, 0], NEG_INF)
        e = jnp.where(readable, jnp.exp(s - jnp.max(s, axis=0, keepdims=True)), 0.0)
        p = e / jnp.maximum(jnp.sum(e, axis=0, keepdims=True), 1e-30)
        o_cmp.append(lax.dot_general(vcc, p.astype(BF16), (((0,), (0,)), ((), ())), preferred_element_type=F32))
        imp = p[:, 0:QT]
        for r in range(1, GROUP):
            imp = imp + p[:, r * QT:(r + 1) * QT]
        sel_ref[g] = _topk_mask_t(imp, tpos_row // SLC_BLOCK, nb)

    def branch(lo, hi, k_ref, v_ref, penalty):
        def scores(kt):
            k = k_ref[0, pl.ds(pl.multiple_of(kt * QT, QT), QT), :].astype(BF16)
            return tuple(_dot_nt(k, qg[g]) for g in groups)

        def step(kt, s_cur):
            s_next = scores(jnp.minimum(kt + 1, hi - 1))
            off = pl.multiple_of(kt * QT, QT)
            v = v_ref[0, pl.ds(off, QT), :]
            for g in groups:
                m = m_ref[g]
                s = s_cur[g] + bt_ref[g, jnp.minimum(qi - kt, 2)]
                s = s + jnp.concatenate([penalty(g, off)] * GROUP, axis=1)
                m_new = jnp.maximum(m, jnp.max(s, axis=0, keepdims=True))
                p = jnp.exp(s - m_new).astype(BF16)
                v_ext = jnp.where(own[g], v, 1.0).astype(BF16)
                acc_ref[g] = jnp.exp(m - m_new) * (acc_ref[g] + pv_ref[g])
                pv_ref[g] = lax.dot_general(v_ext, p, (((0,), (0,)), ((), ())), preferred_element_type=F32)
                m_ref[g] = m_new
            return s_next

        for g in groups:
            m_ref[g] = jnp.full((1, rows), M_FLOOR, F32)
            acc_ref[g] = jnp.zeros((KV_WIDTH, rows), F32)
            pv_ref[g] = jnp.zeros((KV_WIDTH, rows), F32)
        lax.fori_loop(lo, hi, step, scores(lo))
        out = []
        for g in groups:
            acc = acc_ref[g] + pv_ref[g]
            denom = acc[(1 - g) * HEAD_DIM:(1 - g) * HEAD_DIM + 1, :]
            out.append(acc / jnp.maximum(denom, 1e-30))
        return out

    key_row = lax.broadcasted_iota(jnp.int32, (QT, 1), 0)

    def slc_penalty(g, off):
        causal = jnp.where(tpos_row - (off + key_row) >= 0, 0.0, NEG_INF)
        first = off // SLC_BLOCK
        chosen = jnp.where(key_row < SLC_BLOCK, sel_ref[g, pl.ds(first, 1), :], sel_ref[g, pl.ds(first + 1, 1), :])
        return jnp.where(chosen > 0.5, causal, NEG_INF)

    def win_penalty(g, off):
        dist = tpos_row - (off + key_row)
        return jnp.where(dist >= 0, jnp.where(dist <= WINDOW, 0.0, NEG_INF), NEG_INF)

    slc = branch(0, qi + 1, ks_ref, vs_ref, slc_penalty)
    win = branch(jnp.maximum(qi - WINDOW // QT, 0), qi + 1, kw_ref, vw_ref, win_penalty)

    def gate_row(branch, g):
        first = branch * N_HEADS_NSA + g * GROUP
        return jnp.concatenate([gates_t[first + r:first + r + 1, :] for r in range(GROUP)], axis=1)

    o_t = [gate_row(0, g) * o_cmp[g] + gate_row(1, g) * slc[g] + gate_row(2, g) * win[g] for g in groups]
    feature_row = lax.broadcasted_iota(jnp.int32, (KV_WIDTH, 1), 0)
    o = jnp.where(feature_row < HEAD_DIM, o_t[0], o_t[1]).T
    for g in groups:
        for r in range(GROUP):
            h = g * GROUP + r
            o_ref[:, h * HEAD_DIM:(h + 1) * HEAD_DIM] = o[r * QT:(r + 1) * QT, g * HEAD_DIM:(g + 1) * HEAD_DIM]


def _nsa_prompt(q, gate, ks, vs, kw, vw, kcc, vcc, bias_tiles, bias_cmp, n, seq):
    nq = seq // QT
    nb = seq // SLC_BLOCK
    seq_spec = pl.BlockSpec((1, seq, KV_WIDTH), lambda b, i: (b, 0, 0))
    cmp_spec = pl.BlockSpec((1, nb, KV_WIDTH), lambda b, i: (b, 0, 0))
    return pl.pallas_call(
        _nsa_prompt_kernel,
        out_shape=jax.ShapeDtypeStruct((n * seq, NSA_WIDTH), F32),
        grid=(n, nq),
        in_specs=[
            pl.BlockSpec((QT, NSA_WIDTH), lambda b, i: (b * nq + i, 0)),
            pl.BlockSpec((QT, LANES), lambda b, i: (b * nq + i, 0)),
            seq_spec, seq_spec, seq_spec, seq_spec, cmp_spec, cmp_spec,
            pl.BlockSpec(bias_tiles.shape, lambda b, i: (0, 0, 0, 0)),
            pl.BlockSpec((N_KV_NSA, 1, nb, GROUP * QT), lambda b, i: (0, i, 0, 0)),
        ],
        out_specs=pl.BlockSpec((QT, NSA_WIDTH), lambda b, i: (b * nq + i, 0)),
        scratch_shapes=[pltpu.VMEM((N_KV_NSA, nb, QT), F32),
                        pltpu.VMEM((N_KV_NSA, 1, GROUP * QT), F32),
                        pltpu.VMEM((N_KV_NSA, KV_WIDTH, GROUP * QT), F32),
                        pltpu.VMEM((N_KV_NSA, KV_WIDTH, GROUP * QT), F32)],
        compiler_params=_cparams(("parallel", "arbitrary")),
        name="nsa_prompt",
    )(q, gate, ks.reshape(n, seq, KV_WIDTH), vs.reshape(n, seq, KV_WIDTH), kw.reshape(n, seq, KV_WIDTH),
      vw.reshape(n, seq, KV_WIDTH), kcc.reshape(n, nb, KV_WIDTH), vcc.reshape(n, nb, KV_WIDTH),
      bias_tiles, bias_cmp)


NEW_PAD = SUBLANES


def _row_token(t):
    return jnp.concatenate([lax.broadcasted_iota(jnp.int32, (t, 1), 0)] * GROUP, axis=0)


def _nsa_sample_cmp_kernel(q_ref, kcc_ref, vcc_ref, bc_ref, o_ref, sel_ref, *, past):
    t = q_ref.shape[1]
    nbp = kcc_ref.shape[1]
    width = sel_ref.shape[3]
    q = q_ref[0]
    pos = past + _row_token(t)
    blk_end = (lax.broadcasted_iota(jnp.int32, (1, nbp), 1) + 1) * CMP_BLOCK - 1
    cur = (past + lax.broadcasted_iota(jnp.int32, (t, 1), 0)) // SLC_BLOCK
    for g in range(N_KV_NSA):
        cols = slice(g * HEAD_DIM, (g + 1) * HEAD_DIM)
        qg = _stack_heads(q, g).astype(BF16)
        s = _dot_nt(qg, kcc_ref[0][:, cols].astype(BF16)) * ATTN_SCALE + bc_ref[g]
        mask = blk_end <= pos
        s = jnp.where(mask, s, NEG_INF)
        e = jnp.where(mask, jnp.exp(s - jnp.max(s, axis=1, keepdims=True)), 0.0)
        p = e / jnp.maximum(jnp.sum(e, axis=1, keepdims=True), 1e-30)
        o_ref[0, g] = _dot(p.astype(BF16), vcc_ref[0][:, cols].astype(BF16))
        imp = p[0:t]
        for r in range(1, GROUP):
            imp = imp + p[r * t:(r + 1) * t]
        imp = jnp.concatenate([imp, jnp.zeros((t, width - nbp), F32)], axis=1)
        sel = _topk_mask(imp, cur, nbp + 1)
        sel_ref[0, g] = jnp.where(sel, 1.0, 0.0)


def _nsa_sample_cmp(q3, kcc, vcc, bias_cmp, past):
    n, t, _ = q3.shape
    nbp = kcc.shape[1]
    width = -(-(nbp + 1) // LANES) * LANES
    return pl.pallas_call(
        functools.partial(_nsa_sample_cmp_kernel, past=past),
        out_shape=[jax.ShapeDtypeStruct((n, N_KV_NSA, GROUP * t, HEAD_DIM), F32),
                   jax.ShapeDtypeStruct((n, N_KV_NSA, t, width), F32)],
        grid=(n,),
        in_specs=[
            pl.BlockSpec((1, t, NSA_WIDTH), lambda i: (i, 0, 0)),
            pl.BlockSpec((1, nbp, KV_WIDTH), lambda i: (i, 0, 0)),
            pl.BlockSpec((1, nbp, KV_WIDTH), lambda i: (i, 0, 0)),
            pl.BlockSpec(bias_cmp.shape, lambda i: (0, 0, 0)),
        ],
        out_specs=[pl.BlockSpec((1, N_KV_NSA, GROUP * t, HEAD_DIM), lambda i: (i, 0, 0, 0)),
                   pl.BlockSpec((1, N_KV_NSA, t, width), lambda i: (i, 0, 0, 0))],
        compiler_params=_cparams(("parallel",)),
        name="nsa_sample_cmp",
    )(q3, kcc, vcc, bias_cmp)


KEY_CHUNK = 1024


def _nsa_sample_attn_kernel(pt_ref, q_ref, gate_ref, ocmp_ref, sel_ref, ksn_ref, vsn_ref, kwn_ref, vwn_ref,
                            cwk_ref, cwv_ref, pk_ref, pv_ref, bs_ref, bn_ref, bw_ref, ex_ref,
                            o_ref, bk_ref, bv_ref, sk_ref, sv_ref, selx_ref, *, n_pages, key_chunk):
    slot = _paged_prefetch(pt_ref, (pk_ref, pv_ref), (bk_ref, bv_ref), (sk_ref, sv_ref), n_pages,
                           _page_copies_by_page)
    t = q_ref.shape[1]
    rows = N_KV_NSA * GROUP * t
    past = n_pages * PAGE_SIZE
    nbp = past // SLC_BLOCK
    wb = cwk_ref.shape[3]
    groups = range(N_KV_NSA)
    q = q_ref[0] * ATTN_SCALE
    gates = jax.nn.sigmoid(gate_ref[0])
    zeros = jnp.zeros((GROUP * t, HEAD_DIM), F32)

    def own_lanes(x, g):
        return jnp.concatenate([x, zeros] if g == 0 else [zeros, x], axis=1)

    qa = jnp.concatenate([own_lanes(_stack_heads(q, g), g) for g in groups], axis=0).astype(BF16)
    sel = jnp.concatenate([sel_ref[0, g] for g in groups for _ in range(GROUP)], axis=0)
    selx_ref[...] = _dot(sel[:, :nbp].astype(BF16), ex_ref[...])
    tok = jnp.concatenate([_row_token(t)] * N_KV_NSA, axis=0)
    jn = lax.broadcasted_iota(jnp.int32, (1, NEW_PAD), 1)
    new_mask = jn <= tok
    jw = lax.broadcasted_iota(jnp.int32, (1, wb), 1)
    wdist = wb + tok - jw
    win_mask = (wdist >= 0) & (wdist <= WINDOW)
    bias_new = bn_ref[:, :NEW_PAD]
    pad = jnp.zeros((NEW_PAD - t, KV_WIDTH), F32)

    def new_rows(ref):
        return jnp.concatenate([ref[0], pad], axis=0).astype(BF16)

    pages_per_chunk = key_chunk // PAGE_SIZE
    qg = [_stack_heads(q, g).astype(BF16) for g in groups]

    def scores(k_t):
        return jnp.concatenate([_dot(qg[g], k_t(g)) for g in groups], axis=0)

    def values(v_t):
        return lambda p: jnp.concatenate(
            [own_lanes(_dot_nt(p[g * GROUP * t:(g + 1) * GROUP * t], v_t(g)), g) for g in groups], axis=0)

    def slc_step(c, carry):
        off = pl.multiple_of(c * key_chunk, key_chunk)

        def chunk(buf_ref):
            return lambda g: jnp.concatenate([buf_ref[slot, c * pages_per_chunk + i, g]
                                              for i in range(pages_per_chunk)], axis=1).astype(BF16)

        s = scores(chunk(bk_ref)) + bs_ref[:, pl.ds(off, key_chunk)]
        return _softmax_step(carry, s, selx_ref[:, pl.ds(off, key_chunk)] > 0.5, values(chunk(bv_ref)))

    carry = lax.fori_loop(0, past // key_chunk, slc_step, _softmax_init(rows, KV_WIDTH))
    s = _dot_nt(qa, new_rows(ksn_ref)) + bias_new
    carry = _softmax_step(carry, s, new_mask & (sel[:, nbp:nbp + 1] > 0.5), new_rows(vsn_ref))
    o_slc = _softmax_done(carry)

    s = scores(lambda g: cwk_ref[0, g].astype(BF16)) + bw_ref[...]
    carry = _softmax_step(_softmax_init(rows, KV_WIDTH), s, win_mask, values(lambda g: cwv_ref[0, g].astype(BF16)))
    s = _dot_nt(qa, new_rows(kwn_ref)) + bias_new
    carry = _softmax_step(carry, s, new_mask, new_rows(vwn_ref))
    o_win = _softmax_done(carry)

    def gate_col(branch):
        return _stack_cols(gates, [branch * N_HEADS_NSA + h for h in range(N_HEADS_NSA)])

    o_cmp = jnp.concatenate([own_lanes(ocmp_ref[0, g], g) for g in groups], axis=0)
    o = gate_col(0) * o_cmp + gate_col(1) * o_slc + gate_col(2) * o_win
    for h in range(N_HEADS_NSA):
        g = h // GROUP
        o_ref[0, :, h * HEAD_DIM:(h + 1) * HEAD_DIM] = o[h * t:(h + 1) * t, g * HEAD_DIM:(g + 1) * HEAD_DIM]


def _nsa_sample_attn(page_table, q3, gate3, o_cmp, sel, ksn, vsn, kwn, vwn, cwk, cwv, pool_k, pool_v,
                     bias_slc, bias_new, bias_win, expand):
    n, n_pages = page_table.shape
    t = q3.shape[1]
    past = n_pages * PAGE_SIZE
    wb = cwk.shape[3]
    width = sel.shape[3]
    key_chunk = min(KEY_CHUNK, past)
    assert past % key_chunk == 0 and key_chunk % PAGE_SIZE == 0 and t <= NEW_PAD
    per_seq = lambda shape: pl.BlockSpec((1,) + shape, lambda i, pt: (i,) + (0,) * len(shape))
    full = lambda a: pl.BlockSpec(a.shape, lambda i, pt: (0,) * a.ndim)
    anyspec = pl.BlockSpec(memory_space=pl.ANY)
    grid_spec = pltpu.PrefetchScalarGridSpec(
        num_scalar_prefetch=1,
        grid=(n,),
        in_specs=[per_seq((t, NSA_WIDTH)), per_seq((t, LANES)), per_seq((N_KV_NSA, GROUP * t, HEAD_DIM)),
                  per_seq((N_KV_NSA, t, width)),
                  per_seq((t, KV_WIDTH)), per_seq((t, KV_WIDTH)), per_seq((t, KV_WIDTH)), per_seq((t, KV_WIDTH)),
                  per_seq((N_KV_NSA, HEAD_DIM, wb)), per_seq((N_KV_NSA, HEAD_DIM, wb)), anyspec, anyspec,
                  full(bias_slc), full(bias_new), full(bias_win), full(expand)],
        out_specs=per_seq((t, NSA_WIDTH)),
        scratch_shapes=[pltpu.VMEM((2, n_pages, N_KV_NSA, HEAD_DIM, PAGE_SIZE), F32),
                        pltpu.VMEM((2, n_pages, N_KV_NSA, HEAD_DIM, PAGE_SIZE), F32),
                        pltpu.SemaphoreType.DMA((2,)), pltpu.SemaphoreType.DMA((2,)),
                        pltpu.VMEM((N_KV_NSA * GROUP * t, past), F32)],
    )
    return pl.pallas_call(
        functools.partial(_nsa_sample_attn_kernel, n_pages=n_pages, key_chunk=key_chunk),
        out_shape=jax.ShapeDtypeStruct((n, t, NSA_WIDTH), F32),
        grid_spec=grid_spec,
        compiler_params=_cparams(("arbitrary",)),
        name="nsa_sample_attn",
    )(page_table, q3, gate3, o_cmp, sel, ksn, vsn, kwn, vwn, cwk, cwv, pool_k, pool_v,
      bias_slc, bias_new, bias_win, expand)


def _segsum(x, ones_ref):
    hi = x.astype(BF16)
    lo = (x - hi.astype(F32)).astype(BF16)
    return _dot(hi, ones_ref[...]) + _dot(lo, ones_ref[...])


def _softplus(x):
    return jnp.maximum(x, 0.0) + jnp.log(1.0 + jnp.exp(-jnp.abs(x)))


def _rwkv_pre_kernel(p_ref, prev_ref, mu_ref, w0_ref, w2_ref, a0_ref, a2_ref, g2_ref, kk_ref, ka_ref, ones_ref,
                     r_ref, k_ref, v_ref, kkn_ref, kka_ref, d_ref, g_ref, last_ref, buf_ref):
    ti = pl.program_id(1)
    tt = p_ref.shape[1]
    p = p_ref[0]
    first = jnp.where(ti == 0, prev_ref[0], last_ref[...])
    last_ref[...] = p[tt - 1:tt]
    buf_ref[SUBLANES - 1:SUBLANES, :] = first
    buf_ref[SUBLANES:SUBLANES + tt, :] = p
    prev = buf_ref[SUBLANES - 1:SUBLANES - 1 + tt, :]
    xs = p + (prev - p) * mu_ref[...]
    W = RWKV_WIDTH
    r, k, v = xs[:, 0:W], xs[:, W:2 * W], xs[:, 2 * W:3 * W]
    c = 3 * W
    xw = xs[:, c:c + DECAY_LORA]
    xa = xs[:, c + DECAY_LORA:c + DECAY_LORA + AAA_LORA]
    xg = xs[:, c + DECAY_LORA + AAA_LORA:]
    w = -_softplus(-(w0_ref[...] + _dot(jnp.tanh(xw).astype(BF16), w2_ref[...]))) - 0.5
    a = jax.nn.sigmoid(a0_ref[...] + _dot(xa.astype(BF16), a2_ref[...]))
    kk = k * kk_ref[...]
    kk = kk / jnp.maximum(jnp.sqrt(_segsum(kk * kk, ones_ref)), 1e-12)
    r_ref[0] = r
    k_ref[0] = k * (1.0 + (a - 1.0) * ka_ref[...])
    v_ref[0] = v
    kkn_ref[0] = kk
    kka_ref[0] = kk * a
    d_ref[0] = -jnp.exp(w)
    g_ref[0] = _dot(jax.nn.sigmoid(xg).astype(BF16), g2_ref[...])


def _rwkv_pre(p3, prev, mu, w0, w2, a0, a2, g2, k_k, k_a, ones_bd):
    n, t, c = p3.shape
    tt = min(t, 256)
    assert t % tt == 0
    row = lambda a: a.reshape(1, -1)
    full = lambda a: pl.BlockSpec(a.shape, lambda b, i: (0,) * a.ndim)
    params = [row(mu), row(w0), w2, row(a0), a2, g2, row(k_k), row(k_a), ones_bd]
    out_spec = pl.BlockSpec((1, tt, RWKV_WIDTH), lambda b, i: (b, i, 0))
    return pl.pallas_call(
        _rwkv_pre_kernel,
        out_shape=[jax.ShapeDtypeStruct((n, t, RWKV_WIDTH), F32)] * 7,
        grid=(n, t // tt),
        in_specs=[pl.BlockSpec((1, tt, c), lambda b, i: (b, i, 0)),
                  pl.BlockSpec((1, 1, c), lambda b, i: (b, 0, 0))] + [full(a) for a in params],
        out_specs=[out_spec] * 7,
        scratch_shapes=[pltpu.VMEM((1, c), F32), pltpu.VMEM((SUBLANES + tt, c), F32)],
        compiler_params=_cparams(("parallel", "arbitrary")),
        name="rwkv_pre",
    )(p3, prev.reshape(n, 1, c), *params)


def _rwkv_scan_kernel(r_ref, k_ref, v_ref, kk_ref, kka_ref, ld_ref, s0_ref, y_ref, sout_ref, s_ref):
    ci = pl.program_id(1)
    t_in = r_ref.shape[1]
    C = TC

    @pl.when(ci == 0)
    def _():
        s_ref[...] = s0_ref[0]

    def rows(ref):
        x = ref[0]
        if t_in < C:
            x = jnp.concatenate([x, jnp.zeros((C - t_in, x.shape[1]), F32)], axis=0)
        return x

    ld = rows(ld_ref)
    tri = jnp.where(lax.broadcasted_iota(jnp.int32, (C, C), 1) <= lax.broadcasted_iota(jnp.int32, (C, C), 0),
                    1.0, 0.0).astype(BF16)
    hi = ld.astype(BF16)
    rem = ld - hi.astype(F32)
    mid = rem.astype(BF16)
    lo = (rem - mid.astype(F32)).astype(BF16)
    cum = _dot(tri, hi) + _dot(tri, mid) + _dot(tri, lo)
    p_inc = jnp.exp(cum)
    p_inv = jnp.exp(-cum)
    a_t = -rows(kk_ref) * jnp.exp(cum - ld)
    b_t = rows(kka_ref) * p_inv
    k_t = rows(k_ref) * p_inv
    r_t = rows(r_ref) * p_inc
    v = rows(v_ref)
    p_end = p_inc[C - 1:C]

    i2 = lax.broadcasted_iota(jnp.int32, (2 * C, 2 * C), 0)
    j2 = lax.broadcasted_iota(jnp.int32, (2 * C, 2 * C), 1)
    src = j2 & (C - 1)
    keep = src < jnp.where(i2 < C, i2, i2 - C + 1)

    heads = range(N_HEADS_RWKV)
    cols = [slice(h * HEAD_DIM, (h + 1) * HEAD_DIM) for h in heads]
    ar = [jnp.concatenate([a_t[:, c], r_t[:, c]], axis=0).astype(BF16) for c in cols]
    bk = [jnp.concatenate([b_t[:, c], k_t[:, c]], axis=0).astype(BF16) for c in cols]
    vh = [v[:, c].astype(BF16) for c in cols]
    s0 = [s_ref[h] for h in heads]
    quad = [jnp.where(keep, _dot_nt(ar[h], bk[h]), 0.0).astype(BF16) for h in heads]
    base = [_dot_nt(ar[h], s0[h].astype(BF16)) for h in heads]
    x = [base[h][:C] + _dot(quad[h][:C, C:], vh[h]) for h in heads]
    power = [quad[h][:C, :C] for h in heads]
    levels = C.bit_length() - 1
    for lvl in range(levels):
        x = [x[h] + _dot(power[h], x[h].astype(BF16)) for h in heads]
        if lvl + 1 < levels:
            power = [_dot(power[h], power[h]).astype(BF16) for h in heads]
    uv = [jnp.concatenate([x[h].astype(BF16), vh[h]], axis=0) for h in heads]
    ys = [base[h][C:] + _dot(quad[h][C:, :], uv[h]) for h in heads]
    for h in heads:
        upd = lax.dot_general(uv[h], bk[h], (((0,), (0,)), ((), ())), preferred_element_type=F32)
        s_ref[h] = (s0[h] + upd) * p_end[:, cols[h]]
    y = jnp.concatenate(ys, axis=1)
    y_ref[0] = y[:t_in]

    @pl.when(ci == pl.num_programs(1) - 1)
    def _():
        sout_ref[0] = s_ref[...]


def _rwkv_scan(r, k, v, kk, kka, d, s0):
    n, t, w = r.shape
    tc = min(t, TC)
    assert t % tc == 0 and TC & (TC - 1) == 0
    seq_spec = pl.BlockSpec((1, tc, w), lambda b, i: (b, i, 0))
    st_spec = pl.BlockSpec((1, N_HEADS_RWKV, HEAD_DIM, HEAD_DIM), lambda b, i: (b, 0, 0, 0))
    return pl.pallas_call(
        _rwkv_scan_kernel,
        out_shape=[jax.ShapeDtypeStruct((n, t, w), F32),
                   jax.ShapeDtypeStruct((n, N_HEADS_RWKV, HEAD_DIM, HEAD_DIM), F32)],
        grid=(n, t // tc),
        in_specs=[seq_spec] * 6 + [st_spec],
        out_specs=[seq_spec, st_spec],
        scratch_shapes=[pltpu.VMEM((N_HEADS_RWKV, HEAD_DIM, HEAD_DIM), F32)],
        compiler_params=_cparams(("parallel", "arbitrary")),
        name="rwkv_scan",
    )(r, k, v, kk, kka, d, s0)


def _mix_out_kernel(x_ref, nsa_ref, y_ref, r_ref, k_ref, v_ref, g_ref, rk_ref, lnw_ref, lnb_ref, ones_ref,
                    wa_ref, wb_ref, o_ref):
    y = y_ref[...]
    mean = _segsum(y, ones_ref) * (1.0 / HEAD_DIM)
    yc = y - mean
    var = _segsum(yc * yc, ones_ref) * (1.0 / HEAD_DIM)
    yn = yc * lax.rsqrt(var + GN_EPS) * lnw_ref[...] + lnb_ref[...]
    bonus = _segsum(r_ref[...] * k_ref[...] * rk_ref[...], ones_ref) * v_ref[...]
    rw = ((yn + bonus) * g_ref[...]).astype(BF16)
    o_ref[...] = x_ref[...] + _dot(nsa_ref[...].astype(BF16), wa_ref[...]) + _dot(rw, wb_ref[...])


def _mix_out(x, o_nsa, y, r, k, v, g, r_k, ln_w, ln_b, ones_bd, wa, wb):
    m, d = x.shape
    tm = min(m, 512)
    assert m % tm == 0
    row = lambda a: a.reshape(1, -1)
    tile = lambda w: pl.BlockSpec((tm, w), lambda i: (i, 0))
    full = lambda a: pl.BlockSpec(a.shape, lambda i: (0,) * a.ndim)
    params = [row(r_k), row(ln_w), row(ln_b), ones_bd, wa, wb]
    return pl.pallas_call(
        _mix_out_kernel,
        out_shape=jax.ShapeDtypeStruct((m, d), F32),
        grid=(m // tm,),
        in_specs=[tile(d), tile(NSA_WIDTH)] + [tile(RWKV_WIDTH)] * 5 + [full(a) for a in params],
        out_specs=tile(d),
        compiler_params=_cparams(("parallel",)),
        name="mix_out",
    )(x, o_nsa, y, r, k, v, g, *params)


def _block_diag2(w):
    z = jnp.zeros_like(w)
    return jnp.concatenate([jnp.concatenate([w, z], axis=-1), jnp.concatenate([z, w], axis=-1)], axis=-2)


def _compress_weights_t(pe, w1, w2):
    per_page = PAGE_SIZE // CMP_BLOCK
    return (jnp.concatenate([pe.T] * per_page, axis=1), _block_diag2(jnp.transpose(w1, (1, 0, 2))).astype(BF16),
            _block_diag2(w2).astype(BF16))


def _compress_weights(pe, w1, w2):
    return (jnp.concatenate([pe] * N_KV_NSA, axis=1), _block_diag2(w1).astype(BF16), _block_diag2(w2).astype(BF16))


def kernel(x_prompt, x_sample, cache_cmp_k, cache_cmp_v, cache_slc_k, cache_slc_v, cache_win_k, cache_win_v,
           state_wkv, state_shift, page_table, rel_bias_table, ffn1_norm, ffn1_wg, ffn1_wu, ffn1_wd, mix_norm,
           w_in, cmp_pe_k, cmp_w1_k, cmp_w2_k, cmp_pe_v, cmp_w1_v, cmp_w2_v, shift_mu, decay_w0, decay_w2,
           aaa_a0, aaa_a2, gate_g2, k_k, k_a, r_k, ln_x_w, ln_x_b, w_out, ffn2_norm, ffn2_wg, ffn2_wu, ffn2_wd,
           final_norm):
    bp, seq, d = x_prompt.shape
    bs, ts, _ = x_sample.shape
    depth = w_in.shape[0]
    n_pages = page_table.shape[1]
    past = n_pages * PAGE_SIZE
    n_phys = cache_cmp_k.shape[1]
    wbuf = cache_win_k.shape[2]
    assert seq % QT == 0 and seq >= WINDOW and ts < CMP_BLOCK and wbuf == WINDOW
    nq = seq // QT
    nb = seq // SLC_BLOCK
    nbp = past // SLC_BLOCK
    nsa_main = NSA_WIDTH + 6 * KV_WIDTH
    nsa_cols = nsa_main + N_GATES

    bias_tiles = _bias_tiles(rel_bias_table)
    bias_cmp_p = _bias_rows(rel_bias_table, q0=0, rows=seq, n_keys=nb, key_first=CMP_BLOCK - 1, key_stride=CMP_BLOCK)
    bias_cmp_p = bias_cmp_p.reshape(N_KV_NSA, GROUP, nq, QT, nb).transpose(0, 2, 4, 1, 3).reshape(
        N_KV_NSA, nq, nb, GROUP * QT)
    bias_cmp_s = _bias_rows(rel_bias_table, q0=past, rows=ts, n_keys=nbp, key_first=CMP_BLOCK - 1,
                            key_stride=CMP_BLOCK).reshape(N_KV_NSA, GROUP * ts, nbp)
    stacked = lambda b: b.reshape(N_HEADS_NSA * ts, b.shape[-1])
    bias_slc_s = stacked(_bias_rows(rel_bias_table, q0=past, rows=ts, n_keys=past, key_first=0, key_stride=1))
    bias_new_s = stacked(_bias_rows(rel_bias_table, q0=past, rows=ts, n_keys=LANES, key_first=past, key_stride=1))
    bias_win_s = stacked(_bias_rows(rel_bias_table, q0=past, rows=ts, n_keys=wbuf,
                                    key_first=past - wbuf, key_stride=1))
    expand_s = (jnp.arange(past, dtype=jnp.int32)[None, :] // SLC_BLOCK
                == jnp.arange(nbp, dtype=jnp.int32)[:, None]).astype(BF16)
    head_of = jnp.arange(RWKV_WIDTH, dtype=jnp.int32) // HEAD_DIM
    ones_bd = (head_of[:, None] == head_of[None, :]).astype(BF16)

    xp = x_prompt.reshape(bp * seq, d)
    xs = x_sample.reshape(bs * ts, d)
    states = []
    for l in range(depth):
        bf = lambda a: a[l].astype(BF16)
        w_in_l = w_in[l]
        w_pad = jnp.concatenate(
            [w_in_l[:, :nsa_main], w_in_l[:, nsa_main:nsa_cols], jnp.zeros((d, LANES - N_GATES), F32),
             w_in_l[:, nsa_cols:]], axis=1).astype(BF16)
        cw_k = _compress_weights(cmp_pe_k[l], cmp_w1_k[l], cmp_w2_k[l])
        cw_v = _compress_weights(cmp_pe_v[l], cmp_w1_v[l], cmp_w2_v[l])
        wa, wb = w_out[l][:NSA_WIDTH].astype(BF16), w_out[l][NSA_WIDTH:].astype(BF16)
        rw_pre = (shift_mu[l], decay_w0[l], bf(decay_w2), aaa_a0[l], bf(aaa_a2), bf(gate_g2), k_k[l], k_a[l], ones_bd)
        ffn1 = (ffn1_norm[l], bf(ffn1_wg), bf(ffn1_wu), bf(ffn1_wd), final_norm)
        ffn2 = (ffn2_norm[l], bf(ffn2_wg), bf(ffn2_wu), bf(ffn2_wd), final_norm)
        last = l == depth - 1

        xp = _ffn_half(xp, *ffn1, final=False)
        xs = _ffn_half(xs, *ffn1, final=False)

        q, kc, vc, ks, vs, kw, vw, gate, zr = _proj(xp, mix_norm[l], w_pad)
        kcc = _compress(kc, *cw_k)
        vcc = _compress(vc, *cw_v)
        o_nsa = _nsa_prompt(q, gate, ks, vs, kw, vw, kcc, vcc, bias_tiles, bias_cmp_p, bp, seq)
        zr3 = zr.reshape(bp, seq, RWKV_COLS)
        r, k2, v, kk, kka, dec, gg = _rwkv_pre(zr3, jnp.zeros((bp, RWKV_COLS), F32), *rw_pre)
        y, wkv_p = _rwkv_scan(r, k2, v, kk, kka, dec,
                              jnp.zeros((bp, N_HEADS_RWKV, HEAD_DIM, HEAD_DIM), F32))
        flat = lambda a: a.reshape(-1, RWKV_WIDTH)
        xp = _mix_out(xp, o_nsa, flat(y), flat(r), flat(k2), flat(v), flat(gg), r_k[l], ln_x_w[l], ln_x_b[l],
                      ones_bd, wa, wb)
        kv5 = lambda a, n_, t_: a.reshape(n_, t_, N_KV_NSA, HEAD_DIM)
        p_state = (kv5(kc, bp, seq), kv5(vc, bp, seq), kv5(ks, bp, seq), kv5(vs, bp, seq),
                   kv5(kw, bp, seq)[:, seq - WINDOW:], kv5(vw, bp, seq)[:, seq - WINDOW:],
                   wkv_p, zr3[:, seq - 1])

        q, kc, vc, ks, vs, kw, vw, gate, zr = _proj(xs, mix_norm[l], w_pad)
        feature_major = lambda c: jnp.transpose(c[l], (0, 2, 3, 1))
        kcc, vcc = _compress_paged(page_table, feature_major(cache_cmp_k), feature_major(cache_cmp_v),
                                   _compress_weights_t(cmp_pe_k[l], cmp_w1_k[l], cmp_w2_k[l]),
                                   _compress_weights_t(cmp_pe_v[l], cmp_w1_v[l], cmp_w2_v[l]))
        seq3 = lambda a: a.reshape(bs, ts, a.shape[-1])
        o_cmp, sel = _nsa_sample_cmp(seq3(q), kcc, vcc, bias_cmp_s, past)
        cwk = feature_major(cache_win_k)
        cwv = feature_major(cache_win_v)
        o_nsa = _nsa_sample_attn(page_table, seq3(q), seq3(gate), o_cmp, sel, seq3(ks), seq3(vs), seq3(kw),
                                 seq3(vw), cwk, cwv, feature_major(cache_slc_k), feature_major(cache_slc_v),
                                 bias_slc_s, bias_new_s, bias_win_s, expand_s)
        zr3 = seq3(zr)
        r, k2, v, kk, kka, dec, gg = _rwkv_pre(zr3, state_shift[l], *rw_pre)
        y, wkv_s = _rwkv_scan(r, k2, v, kk, kka, dec, state_wkv[l])
        xs = _mix_out(xs, o_nsa.reshape(bs * ts, NSA_WIDTH), flat(y), flat(r), flat(k2), flat(v), flat(gg),
                      r_k[l], ln_x_w[l], ln_x_b[l], ones_bd, wa, wb)
        win = lambda cache, new: jnp.concatenate([cache[l], kv5(new, bs, ts)], axis=1)[:, ts:]
        s_state = (kv5(kc, bs, ts), kv5(vc, bs, ts), kv5(ks, bs, ts), kv5(vs, bs, ts),
                   win(cache_win_k, kw), win(cache_win_v, vw), wkv_s, zr3[:, ts - 1])

        xp = _ffn_half(xp, *ffn2, final=last)
        xs = _ffn_half(xs, *ffn2, final=last)
        states.append(p_state + s_state)

    stacked_states = [jnp.stack(z) for z in zip(*states)]
    return (xp.reshape(bp, seq, d), xs.reshape(bs, ts, d), *stacked_states)
```

```python
import functools
import math

import jax
import jax.numpy as jnp
import numpy as np
from jax import lax
from jax.experimental import pallas as pl
from jax.experimental.pallas import tpu as pltpu

F32 = jnp.float32
BF16 = jnp.bfloat16

HEAD_DIM = 64
N_HEADS_NSA = 8
N_KV_NSA = 2
GROUP = N_HEADS_NSA // N_KV_NSA
N_HEADS_RWKV = 8
NSA_WIDTH = N_HEADS_NSA * HEAD_DIM
RWKV_WIDTH = N_HEADS_RWKV * HEAD_DIM
KV_WIDTH = N_KV_NSA * HEAD_DIM
CMP_BLOCK = 64
SLC_BLOCK = 64
TOP_K_BLOCKS = 16
WINDOW = 512
PAGE_SIZE = 128
N_BUCKETS = 32
MAX_DISTANCE = 128
DECAY_LORA = 64
AAA_LORA = 64
GATE_LORA = 128
NORM_EPS = 1e-6
GN_EPS = 64e-5
ATTN_SCALE = HEAD_DIM ** -0.5
FORCE_SCORE = 1e4
NEG_INF = -1e30
M_FLOOR = -1e29
RWKV_COLS = 3 * RWKV_WIDTH + DECAY_LORA + AAA_LORA + GATE_LORA
N_GATES = 3 * N_HEADS_NSA

LANES = 128
SUBLANES = 8
VMEM_LIMIT = 56 * 1024 * 1024

QT = 128
TC = 64


def _bucket_thresholds():
    max_exact = N_BUCKETS // 2
    thr = list(range(max_exact + 1))
    for b in range(max_exact + 1, N_BUCKETS):
        x = max_exact * (MAX_DISTANCE / max_exact) ** ((b - max_exact) / (N_BUCKETS - max_exact))
        thr.append(int(math.ceil(x - 1e-9)))
    return thr


BUCKET_THR = _bucket_thresholds()


def _cparams(sem):
    return pltpu.CompilerParams(dimension_semantics=sem, vmem_limit_bytes=VMEM_LIMIT)


def _dot(a, b):
    return jnp.dot(a, b, preferred_element_type=F32)


def _dot_nt(a, b):
    return lax.dot_general(a, b, (((1,), (1,)), ((), ())), preferred_element_type=F32)


def _rms(x, g):
    return x * lax.rsqrt(jnp.mean(x * x, axis=-1, keepdims=True) + NORM_EPS) * g


def _ffn_kernel(x_ref, g_ref, wg_ref, wu_ref, wd_ref, gf_ref, o_ref, h_ref, acc_ref, *, final):
    j = pl.program_id(1)

    @pl.when(j == 0)
    def _():
        h_ref[...] = _rms(x_ref[...], g_ref[...]).astype(BF16)
        acc_ref[...] = jnp.zeros_like(acc_ref)

    h = h_ref[...]
    a = _dot(h, wg_ref[...])
    b = _dot(h, wu_ref[...])
    act = (a * jax.nn.sigmoid(a) * b).astype(BF16)
    acc_ref[...] += _dot(act, wd_ref[...])

    @pl.when(j == pl.num_programs(1) - 1)
    def _():
        y = x_ref[...] + 0.5 * acc_ref[...]
        if final:
            y = _rms(y, gf_ref[...])
        o_ref[...] = y


def _ffn_half(x, g, wg, wu, wd, gf, *, final):
    m, d = x.shape
    ff = wg.shape[1]
    tm = min(m, 1024)
    tf = 256 if ff % 256 == 0 else ff
    assert m % tm == 0 and ff % tf == 0
    return pl.pallas_call(
        functools.partial(_ffn_kernel, final=final),
        out_shape=jax.ShapeDtypeStruct((m, d), F32),
        grid=(m // tm, ff // tf),
        in_specs=[
            pl.BlockSpec((tm, d), lambda i, j: (i, 0)),
            pl.BlockSpec((1, d), lambda i, j: (0, 0)),
            pl.BlockSpec((d, tf), lambda i, j: (0, j)),
            pl.BlockSpec((d, tf), lambda i, j: (0, j)),
            pl.BlockSpec((tf, d), lambda i, j: (j, 0)),
            pl.BlockSpec((1, d), lambda i, j: (0, 0)),
        ],
        out_specs=pl.BlockSpec((tm, d), lambda i, j: (i, 0)),
        scratch_shapes=[pltpu.VMEM((tm, d), BF16), pltpu.VMEM((tm, d), F32)],
        compiler_params=_cparams(("parallel", "arbitrary")),
        name="ffn_half",
    )(x, g.reshape(1, d), wg, wu, wd, gf.reshape(1, d))


N_STATE_ROWS = 4


def _proj_kernel(x_ref, g_ref, w_ref, q_ref, kc_ref, vc_ref, ks_ref, vs_ref, kw_ref, vw_ref, gate_ref, zr_ref,
                 *state_t_refs):
    h = _rms(x_ref[...], g_ref[...]).astype(BF16)
    z = _dot(h, w_ref[...])
    c = 0
    for ref in (q_ref, kc_ref, vc_ref, ks_ref, vs_ref, kw_ref, vw_ref, gate_ref, zr_ref):
        w = ref.shape[1]
        ref[...] = z[:, c:c + w]
        c += w
    for i, ref in enumerate(state_t_refs):
        c = NSA_WIDTH + i * KV_WIDTH
        ref[0] = z[:, c:c + KV_WIDTH].T


def _proj(x, g, w_pad, seq=None):
    m, d = x.shape
    tm = min(m, 512)
    widths = (NSA_WIDTH,) + (KV_WIDTH,) * 6 + (LANES, RWKV_COLS)
    assert sum(widths) == w_pad.shape[1] and m % tm == 0
    out_shape = [jax.ShapeDtypeStruct((m, w), F32) for w in widths]
    out_specs = [pl.BlockSpec((tm, w), lambda i: (i, 0)) for w in widths]
    if seq is not None:
        assert seq % tm == 0
        per_seq = seq // tm
        out_shape += [jax.ShapeDtypeStruct((m // seq, KV_WIDTH, seq), F32)] * N_STATE_ROWS
        out_specs += [pl.BlockSpec((1, KV_WIDTH, tm), lambda i: (i // per_seq, 0, i % per_seq))] * N_STATE_ROWS
    return pl.pallas_call(
        _proj_kernel,
        out_shape=out_shape,
        grid=(m // tm,),
        in_specs=[
            pl.BlockSpec((tm, d), lambda i: (i, 0)),
            pl.BlockSpec((1, d), lambda i: (0, 0)),
            pl.BlockSpec(w_pad.shape, lambda i: (0, 0)),
        ],
        out_specs=out_specs,
        compiler_params=_cparams(("parallel",)),
        name="in_proj",
    )(x, g.reshape(1, d), w_pad)


def _out_kernel(x_ref, a_ref, b_ref, wa_ref, wb_ref, o_ref):
    o_ref[...] = (x_ref[...] + _dot(a_ref[...].astype(BF16), wa_ref[...])
                  + _dot(b_ref[...].astype(BF16), wb_ref[...]))


def _out_proj(x, a, b, wa, wb):
    m, d = x.shape
    tm = min(m, 512)
    return pl.pallas_call(
        _out_kernel,
        out_shape=jax.ShapeDtypeStruct((m, d), F32),
        grid=(m // tm,),
        in_specs=[
            pl.BlockSpec((tm, d), lambda i: (i, 0)),
            pl.BlockSpec((tm, a.shape[1]), lambda i: (i, 0)),
            pl.BlockSpec((tm, b.shape[1]), lambda i: (i, 0)),
            pl.BlockSpec(wa.shape, lambda i: (0, 0)),
            pl.BlockSpec(wb.shape, lambda i: (0, 0)),
        ],
        out_specs=pl.BlockSpec((tm, d), lambda i: (i, 0)),
        compiler_params=_cparams(("parallel",)),
        name="out_proj",
    )(x, a, b, wa, wb)


def _bias_lookup(dist, table_ref, heads):
    vals = [jnp.full(dist.shape, table_ref[0, h], F32) for h in heads]
    for b in range(1, N_BUCKETS):
        ge = dist >= BUCKET_THR[b]
        vals = [jnp.where(ge, table_ref[b, h], v) for h, v in zip(heads, vals)]
    return vals


def _bias_tiles_kernel(table_ref, o_ref):
    j = lax.broadcasted_iota(jnp.int32, (QT, QT), 0)
    i = lax.broadcasted_iota(jnp.int32, (QT, QT), 1)
    for off in range(3):
        vals = _bias_lookup(off * QT + i - j, table_ref, range(N_HEADS_NSA))
        for h, v in enumerate(vals):
            g, r = divmod(h, GROUP)
            o_ref[g, off, :, r * QT:(r + 1) * QT] = v


def _bias_tiles(table):
    return pl.pallas_call(
        _bias_tiles_kernel,
        out_shape=jax.ShapeDtypeStruct((N_KV_NSA, 3, QT, GROUP * QT), F32),
        in_specs=[pl.BlockSpec(memory_space=pltpu.SMEM)],
        name="bias_tiles",
    )(table)


def _bias_rows_kernel(table_ref, o_ref, *, q0, key_first, key_stride):
    _, _, rc, L = o_ref.shape
    i = lax.broadcasted_iota(jnp.int32, (rc, L), 0) + pl.program_id(0) * rc
    j = lax.broadcasted_iota(jnp.int32, (rc, L), 1)
    dist = (q0 + i) - (key_first + key_stride * j)
    vals = _bias_lookup(dist, table_ref, range(N_HEADS_NSA))
    for h, v in enumerate(vals):
        g, r = divmod(h, GROUP)
        o_ref[g, r] = v


def _bias_rows(table, *, q0, rows, n_keys, key_first, key_stride):
    rc = min(rows, 256)
    assert rows % rc == 0
    return pl.pallas_call(
        functools.partial(_bias_rows_kernel, q0=q0, key_first=key_first, key_stride=key_stride),
        out_shape=jax.ShapeDtypeStruct((N_KV_NSA, GROUP, rows, n_keys), F32),
        grid=(rows // rc,),
        in_specs=[pl.BlockSpec(memory_space=pltpu.SMEM)],
        out_specs=pl.BlockSpec((N_KV_NSA, GROUP, rc, n_keys), lambda i: (0, 0, i, 0)),
        compiler_params=_cparams(("parallel",)),
        name="bias_rows",
    )(table)


def _gelu_tanh(x):
    return 0.5 * x * (1.0 + jnp.tanh(math.sqrt(2.0 / math.pi) * (x + 0.044715 * (x * x * x))))


def _compress_blocks(load_rows, nblk, pe_ref, w1_ref, w2_ref):
    acc = jnp.zeros((nblk, KV_WIDTH), F32)
    for c in range(CMP_BLOCK):
        rows = load_rows(c) + pe_ref[c:c + 1, :]
        acc = acc + _dot(rows.astype(BF16), w1_ref[c])
    return _dot(_gelu_tanh(acc).astype(BF16), w2_ref[...])


def _compress_kernel(x_ref, pe_ref, w1_ref, w2_ref, o_ref):
    nblk = o_ref.shape[0]
    o_ref[...] = _compress_blocks(lambda c: x_ref[pl.ds(c, nblk, stride=CMP_BLOCK), :],
                                  nblk, pe_ref, w1_ref, w2_ref)


def _compress(x, pe2, w1bd, w2bd):
    rows = x.shape[0]
    nb = rows // CMP_BLOCK
    nblk = min(nb, 128)
    assert nb % nblk == 0
    return pl.pallas_call(
        _compress_kernel,
        out_shape=jax.ShapeDtypeStruct((nb, KV_WIDTH), F32),
        grid=(nb // nblk,),
        in_specs=[
            pl.BlockSpec((nblk * CMP_BLOCK, KV_WIDTH), lambda i: (i, 0)),
            pl.BlockSpec(pe2.shape, lambda i: (0, 0)),
            pl.BlockSpec(w1bd.shape, lambda i: (0, 0, 0)),
            pl.BlockSpec(w2bd.shape, lambda i: (0, 0)),
        ],
        out_specs=pl.BlockSpec((nblk, KV_WIDTH), lambda i: (i, 0)),
        compiler_params=_cparams(("parallel",)),
        name="compress",
    )(x, pe2, w1bd, w2bd)


def _page_copies_by_feature(pt_ref, n, pool_ref, buf_ref, sem_ref, slot, n_pages):
    return [pltpu.make_async_copy(pool_ref.at[pt_ref[n, p]], buf_ref.at[slot, :, :, p], sem_ref.at[slot])
            for p in range(n_pages)]


def _page_copies_by_page(pt_ref, n, pool_ref, buf_ref, sem_ref, slot, n_pages):
    return [pltpu.make_async_copy(pool_ref.at[pt_ref[n, p]], buf_ref.at[slot, p], sem_ref.at[slot])
            for p in range(n_pages)]


def _paged_prefetch(pt_ref, pools, bufs, sems, n_pages, _page_copies):
    n = pl.program_id(0)
    slot = n % 2

    def start(seq, slot_):
        for pool, buf, sem in zip(pools, bufs, sems):
            for cp in _page_copies(pt_ref, seq, pool, buf, sem, slot_, n_pages):
                cp.start()

    @pl.when(n == 0)
    def _():
        start(0, 0)

    @pl.when(n + 1 < pl.num_programs(0))
    def _():
        start(n + 1, 1 - slot)

    for pool, buf, sem in zip(pools, bufs, sems):
        for cp in _page_copies(pt_ref, n, pool, buf, sem, slot, n_pages):
            cp.wait()
    return slot


def _compress_paged_kernel(pt_ref, pk_ref, pv_ref, pek_ref, w1k_ref, w2k_ref, pev_ref, w1v_ref, w2v_ref,
                           ok_ref, ov_ref, bk_ref, bv_ref, sk_ref, sv_ref, *, n_pages):
    slot = _paged_prefetch(pt_ref, (pk_ref, pv_ref), (bk_ref, bv_ref), (sk_ref, sv_ref), n_pages,
                           _page_copies_by_feature)

    def compress(buf_ref, pe_ref, w1_ref, w2_ref):
        acc = [jnp.zeros((n_pages, PAGE_SIZE), F32) for _ in range(N_KV_NSA)]
        for d in range(HEAD_DIM):
            for g in range(N_KV_NSA):
                rows = buf_ref[slot, g, d] + pe_ref[d:d + 1, :]
                acc[g] = acc[g] + _dot(rows.astype(BF16), w1_ref[d])
        out = [_dot(_gelu_tanh(a).astype(BF16), w2_ref[...]) for a in acc]
        return jnp.concatenate([out[g][:, j * HEAD_DIM:(j + 1) * HEAD_DIM]
                                for j in range(PAGE_SIZE // CMP_BLOCK) for g in range(N_KV_NSA)], axis=1)

    ok_ref[0] = compress(bk_ref, pek_ref, w1k_ref, w2k_ref)
    ov_ref[0] = compress(bv_ref, pev_ref, w1v_ref, w2v_ref)


def _compress_paged(page_table, pool_k, pool_v, cw_k, cw_v):
    n, n_pages = page_table.shape
    past = n_pages * PAGE_SIZE
    nblk = past // CMP_BLOCK
    per_page = PAGE_SIZE // CMP_BLOCK
    assert per_page == 2
    wspec = lambda a: pl.BlockSpec(a.shape, lambda i, pt: (0,) * a.ndim)
    grid_spec = pltpu.PrefetchScalarGridSpec(
        num_scalar_prefetch=1,
        grid=(n,),
        in_specs=[pl.BlockSpec(memory_space=pl.ANY), pl.BlockSpec(memory_space=pl.ANY)]
                 + [wspec(a) for a in cw_k + cw_v],
        out_specs=[pl.BlockSpec((1, n_pages, per_page * KV_WIDTH), lambda i, pt: (i, 0, 0))] * 2,
        scratch_shapes=[pltpu.VMEM((2, N_KV_NSA, HEAD_DIM, n_pages, PAGE_SIZE), F32),
                        pltpu.VMEM((2, N_KV_NSA, HEAD_DIM, n_pages, PAGE_SIZE), F32),
                        pltpu.SemaphoreType.DMA((2,)), pltpu.SemaphoreType.DMA((2,))],
    )
    kcc, vcc = pl.pallas_call(
        functools.partial(_compress_paged_kernel, n_pages=n_pages),
        out_shape=[jax.ShapeDtypeStruct((n, n_pages, per_page * KV_WIDTH), F32)] * 2,
        grid_spec=grid_spec,
        compiler_params=_cparams(("arbitrary",)),
        name="compress_paged",
    )(page_table, pool_k, pool_v, *cw_k, *cw_v)
    return kcc.reshape(n, nblk, KV_WIDTH), vcc.reshape(n, nblk, KV_WIDTH)


def _stack_heads(x, g):
    return jnp.concatenate(
        [x[:, (g * GROUP + r) * HEAD_DIM:(g * GROUP + r + 1) * HEAD_DIM] for r in range(GROUP)], axis=0)


def _stack_cols(x, cols):
    return jnp.concatenate([x[:, c:c + 1] for c in cols], axis=0)


def _softmax_step(carry, s, mask, v):
    m, l, acc = carry
    s = jnp.where(mask, s, NEG_INF)
    m_new = jnp.maximum(m, jnp.max(s, axis=1, keepdims=True))
    p = jnp.where(mask, jnp.exp(s - m_new), 0.0)
    alpha = jnp.exp(m - m_new)
    l = alpha * l + jnp.sum(p, axis=1, keepdims=True)
    pv = v(p.astype(BF16)) if callable(v) else _dot(p.astype(BF16), v)
    return m_new, l, alpha * acc + pv


def _softmax_init(rows, width):
    return (jnp.full((rows, 1), NEG_INF, F32), jnp.zeros((rows, 1), F32), jnp.zeros((rows, width), F32))


def _softmax_done(carry):
    _, l, acc = carry
    return acc / jnp.maximum(l, 1e-30)


def _topk_mask(imp, cur, n_blocks):
    blk = lax.broadcasted_iota(jnp.int32, imp.shape, 1)
    forced = (blk == 0) | (blk == cur) | (blk == cur - 1)
    score = jnp.where(blk <= cur, jnp.where(forced, FORCE_SCORE, imp), -FORCE_SCORE)
    rank = jnp.zeros(imp.shape, jnp.int32)
    for b in range(n_blocks):
        col = score[:, b:b + 1]
        ahead = (col > score) | ((col == score) & (blk > b))
        rank = rank + jnp.where(ahead, 1, 0)
    return (rank < TOP_K_BLOCKS) & (blk <= cur)


def _topk_mask_t(imp, cur, n_blocks):
    blk = lax.broadcasted_iota(jnp.int32, imp.shape, 0)
    forced = (blk == 0) | (blk == cur) | (blk == cur - 1)
    score = jnp.where(blk <= cur, jnp.where(forced, FORCE_SCORE, imp), -FORCE_SCORE)
    rank = jnp.zeros(imp.shape, jnp.int32)
    for b in range(n_blocks):
        row = score[b:b + 1, :]
        rank = rank + jnp.where(row > score, 1, jnp.where(row == score, jnp.where(blk > b, 1, 0), 0))
    return jnp.where(rank < TOP_K_BLOCKS, jnp.where(blk <= cur, 1.0, 0.0), 0.0)


def _nsa_prompt_kernel(q_ref, gate_ref, ks_ref, vs_ref, kw_ref, vw_ref, kcc_ref, vcc_ref, bt_ref, bc_ref,
                       o_ref, sel_ref, m_ref, acc_ref, pv_ref, s_ref):
    qi = pl.program_id(1)
    seq = ks_ref.shape[1]
    nb = seq // SLC_BLOCK
    rows = GROUP * QT
    groups = range(N_KV_NSA)
    q = q_ref[...] * ATTN_SCALE
    gates_t = jax.nn.sigmoid(gate_ref[...]).T
    tpos_row = qi * QT + lax.broadcasted_iota(jnp.int32, (1, QT), 1)
    lane = lax.broadcasted_iota(jnp.int32, (1, KV_WIDTH), 1)
    own = [(lane // HEAD_DIM) == g for g in groups]
    zeros = jnp.zeros((rows, HEAD_DIM), F32)
    qg = []
    for g in groups:
        qs = _stack_heads(q, g)
        qg.append(jnp.concatenate([qs, zeros] if g == 0 else [zeros, qs], axis=1).astype(BF16))

    blk = lax.broadcasted_iota(jnp.int32, (nb, 1), 0)
    tpos_stacked = qi * QT + (lax.broadcasted_iota(jnp.int32, (1, rows), 1) & (QT - 1))
    readable = (blk + 1) * CMP_BLOCK - 1 <= tpos_stacked
    kcc = kcc_ref[0].astype(BF16)
    vcc = vcc_ref[0].astype(BF16)
    o_cmp = []
    for g in groups:
        s = jnp.where(readable, _dot_nt(kcc, qg[g]) + bc_ref[g, 0], NEG_INF)
        e = jnp.where(readable, jnp.exp(s - jnp.max(s, axis=0, keepdims=True)), 0.0)
        p = e / jnp.maximum(jnp.sum(e, axis=0, keepdims=True), 1e-30)
        o_cmp.append(lax.dot_general(vcc, p.astype(BF16), (((0,), (0,)), ((), ())), preferred_element_type=F32))
        imp = p[:, 0:QT]
        for r in range(1, GROUP):
            imp = imp + p[:, r * QT:(r + 1) * QT]
        sel_ref[g] = _topk_mask_t(imp, tpos_row // SLC_BLOCK, nb)

    def branch(lo, hi, k_ref, v_ref, penalty):
        def scores(kt):
            k = k_ref[0, pl.ds(pl.multiple_of(kt * QT, QT), QT), :].astype(BF16)
            return tuple(_dot_nt(k, qg[g]) for g in groups)

        def step(kt, _):
            s_next = scores(jnp.minimum(kt + 1, hi - 1))
            off = pl.multiple_of(kt * QT, QT)
            v = v_ref[0, pl.ds(off, QT), :]
            for g in groups:
                m = m_ref[g]
                s = s_ref[g] + bt_ref[g, jnp.minimum(qi - kt, 2)]
                s = s + jnp.concatenate([penalty(g, off)] * GROUP, axis=1)
                m_new = jnp.maximum(m, jnp.max(s, axis=0, keepdims=True))
                p = jnp.exp(s - m_new).astype(BF16)
                v_ext = jnp.where(own[g], v, 1.0).astype(BF16)
                acc_ref[g] = jnp.exp(m - m_new) * (acc_ref[g] + pv_ref[g])
                pv_ref[g] = lax.dot_general(v_ext, p, (((0,), (0,)), ((), ())), preferred_element_type=F32)
                m_ref[g] = m_new
            for g in groups:
                s_ref[g] = s_next[g]
            return 0

        first = scores(lo)
        for g in groups:
            m_ref[g] = jnp.full((1, rows), M_FLOOR, F32)
            acc_ref[g] = jnp.zeros((KV_WIDTH, rows), F32)
            pv_ref[g] = jnp.zeros((KV_WIDTH, rows), F32)
            s_ref[g] = first[g]
        lax.fori_loop(lo, hi, step, 0)
        out = []
        for g in groups:
            acc = acc_ref[g] + pv_ref[g]
            denom = acc[(1 - g) * HEAD_DIM:(1 - g) * HEAD_DIM + 1, :]
            out.append(acc / jnp.maximum(denom, 1e-30))
        return out

    key_row = lax.broadcasted_iota(jnp.int32, (QT, 1), 0)

    def slc_penalty(g, off):
        causal = jnp.where(tpos_row - (off + key_row) >= 0, 0.0, NEG_INF)
        first = off // SLC_BLOCK
        chosen = jnp.where(key_row < SLC_BLOCK, sel_ref[g, pl.ds(first, 1), :], sel_ref[g, pl.ds(first + 1, 1), :])
        return jnp.where(chosen > 0.5, causal, NEG_INF)

    def win_penalty(g, off):
        dist = tpos_row - (off + key_row)
        return jnp.where(dist >= 0, jnp.where(dist <= WINDOW, 0.0, NEG_INF), NEG_INF)

    slc = branch(0, qi + 1, ks_ref, vs_ref, slc_penalty)
    win = branch(jnp.maximum(qi - WINDOW // QT, 0), qi + 1, kw_ref, vw_ref, win_penalty)

    def gate_row(branch, g):
        first = branch * N_HEADS_NSA + g * GROUP
        return jnp.concatenate([gates_t[first + r:first + r + 1, :] for r in range(GROUP)], axis=1)

    o_t = [gate_row(0, g) * o_cmp[g] + gate_row(1, g) * slc[g] + gate_row(2, g) * win[g] for g in groups]
    feature_row = lax.broadcasted_iota(jnp.int32, (KV_WIDTH, 1), 0)
    o = jnp.where(feature_row < HEAD_DIM, o_t[0], o_t[1]).T
    for g in groups:
        for r in range(GROUP):
            h = g * GROUP + r
            o_ref[:, h * HEAD_DIM:(h + 1) * HEAD_DIM] = o[r * QT:(r + 1) * QT, g * HEAD_DIM:(g + 1) * HEAD_DIM]


def _nsa_prompt(q, gate, ks, vs, kw, vw, kcc, vcc, bias_tiles, bias_cmp, n, seq):
    nq = seq // QT
    nb = seq // SLC_BLOCK
    seq_spec = pl.BlockSpec((1, seq, KV_WIDTH), lambda b, i: (b, 0, 0))
    cmp_spec = pl.BlockSpec((1, nb, KV_WIDTH), lambda b, i: (b, 0, 0))
    return pl.pallas_call(
        _nsa_prompt_kernel,
        out_shape=jax.ShapeDtypeStruct((n * seq, NSA_WIDTH), F32),
        grid=(n, nq),
        in_specs=[
            pl.BlockSpec((QT, NSA_WIDTH), lambda b, i: (b * nq + i, 0)),
            pl.BlockSpec((QT, LANES), lambda b, i: (b * nq + i, 0)),
            seq_spec, seq_spec, seq_spec, seq_spec, cmp_spec, cmp_spec,
            pl.BlockSpec(bias_tiles.shape, lambda b, i: (0, 0, 0, 0)),
            pl.BlockSpec((N_KV_NSA, 1, nb, GROUP * QT), lambda b, i: (0, i, 0, 0)),
        ],
        out_specs=pl.BlockSpec((QT, NSA_WIDTH), lambda b, i: (b * nq + i, 0)),
        scratch_shapes=[pltpu.VMEM((N_KV_NSA, nb, QT), F32),
                        pltpu.VMEM((N_KV_NSA, 1, GROUP * QT), F32),
                        pltpu.VMEM((N_KV_NSA, KV_WIDTH, GROUP * QT), F32),
                        pltpu.VMEM((N_KV_NSA, KV_WIDTH, GROUP * QT), F32),
                        pltpu.VMEM((N_KV_NSA, QT, GROUP * QT), F32)],
        compiler_params=_cparams(("parallel", "arbitrary")),
        name="nsa_prompt",
    )(q, gate, ks.reshape(n, seq, KV_WIDTH), vs.reshape(n, seq, KV_WIDTH), kw.reshape(n, seq, KV_WIDTH),
      vw.reshape(n, seq, KV_WIDTH), kcc.reshape(n, nb, KV_WIDTH), vcc.reshape(n, nb, KV_WIDTH),
      bias_tiles, bias_cmp)


NEW_PAD = SUBLANES


def _row_token(t):
    return jnp.concatenate([lax.broadcasted_iota(jnp.int32, (t, 1), 0)] * GROUP, axis=0)


def _nsa_sample_cmp_kernel(q_ref, kcc_ref, vcc_ref, bc_ref, o_ref, sel_ref, *, past):
    t = q_ref.shape[1]
    nbp = kcc_ref.shape[1]
    width = sel_ref.shape[3]
    q = q_ref[0]
    pos = past + _row_token(t)
    blk_end = (lax.broadcasted_iota(jnp.int32, (1, nbp), 1) + 1) * CMP_BLOCK - 1
    cur = (past + lax.broadcasted_iota(jnp.int32, (t, 1), 0)) // SLC_BLOCK
    for g in range(N_KV_NSA):
        cols = slice(g * HEAD_DIM, (g + 1) * HEAD_DIM)
        qg = _stack_heads(q, g).astype(BF16)
        s = _dot_nt(qg, kcc_ref[0][:, cols].astype(BF16)) * ATTN_SCALE + bc_ref[g]
        mask = blk_end <= pos
        s = jnp.where(mask, s, NEG_INF)
        e = jnp.where(mask, jnp.exp(s - jnp.max(s, axis=1, keepdims=True)), 0.0)
        p = e / jnp.maximum(jnp.sum(e, axis=1, keepdims=True), 1e-30)
        o_ref[0, g] = _dot(p.astype(BF16), vcc_ref[0][:, cols].astype(BF16))
        imp = p[0:t]
        for r in range(1, GROUP):
            imp = imp + p[r * t:(r + 1) * t]
        imp = jnp.concatenate([imp, jnp.zeros((t, width - nbp), F32)], axis=1)
        sel = _topk_mask(imp, cur, nbp + 1)
        sel_ref[0, g] = jnp.where(sel, 1.0, 0.0)


def _nsa_sample_cmp(q3, kcc, vcc, bias_cmp, past):
    n, t, _ = q3.shape
    nbp = kcc.shape[1]
    width = -(-(nbp + 1) // LANES) * LANES
    return pl.pallas_call(
        functools.partial(_nsa_sample_cmp_kernel, past=past),
        out_shape=[jax.ShapeDtypeStruct((n, N_KV_NSA, GROUP * t, HEAD_DIM), F32),
                   jax.ShapeDtypeStruct((n, N_KV_NSA, t, width), F32)],
        grid=(n,),
        in_specs=[
            pl.BlockSpec((1, t, NSA_WIDTH), lambda i: (i, 0, 0)),
            pl.BlockSpec((1, nbp, KV_WIDTH), lambda i: (i, 0, 0)),
            pl.BlockSpec((1, nbp, KV_WIDTH), lambda i: (i, 0, 0)),
            pl.BlockSpec(bias_cmp.shape, lambda i: (0, 0, 0)),
        ],
        out_specs=[pl.BlockSpec((1, N_KV_NSA, GROUP * t, HEAD_DIM), lambda i: (i, 0, 0, 0)),
                   pl.BlockSpec((1, N_KV_NSA, t, width), lambda i: (i, 0, 0, 0))],
        compiler_params=_cparams(("parallel",)),
        name="nsa_sample_cmp",
    )(q3, kcc, vcc, bias_cmp)


KEY_CHUNK = 1024


def _nsa_sample_attn_kernel(pt_ref, q_ref, gate_ref, ocmp_ref, sel_ref, ksn_ref, vsn_ref, kwn_ref, vwn_ref,
                            cwk_ref, cwv_ref, pk_ref, pv_ref, bs_ref, bn_ref, bw_ref, ex_ref,
                            o_ref, bk_ref, bv_ref, sk_ref, sv_ref, selx_ref, *, n_pages, key_chunk):
    slot = _paged_prefetch(pt_ref, (pk_ref, pv_ref), (bk_ref, bv_ref), (sk_ref, sv_ref), n_pages,
                           _page_copies_by_page)
    t = q_ref.shape[1]
    rows = N_KV_NSA * GROUP * t
    past = n_pages * PAGE_SIZE
    nbp = past // SLC_BLOCK
    wb = cwk_ref.shape[3]
    groups = range(N_KV_NSA)
    q = q_ref[0] * ATTN_SCALE
    gates = jax.nn.sigmoid(gate_ref[0])
    zeros = jnp.zeros((GROUP * t, HEAD_DIM), F32)

    def own_lanes(x, g):
        return jnp.concatenate([x, zeros] if g == 0 else [zeros, x], axis=1)

    qa = jnp.concatenate([own_lanes(_stack_heads(q, g), g) for g in groups], axis=0).astype(BF16)
    sel = jnp.concatenate([sel_ref[0, g] for g in groups for _ in range(GROUP)], axis=0)
    selx_ref[...] = _dot(sel[:, :nbp].astype(BF16), ex_ref[...])
    tok = jnp.concatenate([_row_token(t)] * N_KV_NSA, axis=0)
    jn = lax.broadcasted_iota(jnp.int32, (1, NEW_PAD), 1)
    new_mask = jn <= tok
    jw = lax.broadcasted_iota(jnp.int32, (1, wb), 1)
    wdist = wb + tok - jw
    win_mask = (wdist >= 0) & (wdist <= WINDOW)
    bias_new = bn_ref[:, :NEW_PAD]
    pad = jnp.zeros((NEW_PAD - t, KV_WIDTH), F32)

    def new_rows(ref):
        return jnp.concatenate([ref[0], pad], axis=0).astype(BF16)

    pages_per_chunk = key_chunk // PAGE_SIZE
    qg = [_stack_heads(q, g).astype(BF16) for g in groups]

    def scores(k_t):
        return jnp.concatenate([_dot(qg[g], k_t(g)) for g in groups], axis=0)

    def values(v_t):
        return lambda p: jnp.concatenate(
            [own_lanes(_dot_nt(p[g * GROUP * t:(g + 1) * GROUP * t], v_t(g)), g) for g in groups], axis=0)

    def slc_step(c, carry):
        off = pl.multiple_of(c * key_chunk, key_chunk)

        def chunk(buf_ref):
            return lambda g: jnp.concatenate([buf_ref[slot, c * pages_per_chunk + i, g]
                                              for i in range(pages_per_chunk)], axis=1).astype(BF16)

        s = scores(chunk(bk_ref)) + bs_ref[:, pl.ds(off, key_chunk)]
        return _softmax_step(carry, s, selx_ref[:, pl.ds(off, key_chunk)] > 0.5, values(chunk(bv_ref)))

    carry = lax.fori_loop(0, past // key_chunk, slc_step, _softmax_init(rows, KV_WIDTH))
    s = _dot_nt(qa, new_rows(ksn_ref)) + bias_new
    carry = _softmax_step(carry, s, new_mask & (sel[:, nbp:nbp + 1] > 0.5), new_rows(vsn_ref))
    o_slc = _softmax_done(carry)

    s = scores(lambda g: cwk_ref[0, g].astype(BF16)) + bw_ref[...]
    carry = _softmax_step(_softmax_init(rows, KV_WIDTH), s, win_mask, values(lambda g: cwv_ref[0, g].astype(BF16)))
    s = _dot_nt(qa, new_rows(kwn_ref)) + bias_new
    carry = _softmax_step(carry, s, new_mask, new_rows(vwn_ref))
    o_win = _softmax_done(carry)

    def gate_col(branch):
        return _stack_cols(gates, [branch * N_HEADS_NSA + h for h in range(N_HEADS_NSA)])

    o_cmp = jnp.concatenate([own_lanes(ocmp_ref[0, g], g) for g in groups], axis=0)
    o = gate_col(0) * o_cmp + gate_col(1) * o_slc + gate_col(2) * o_win
    for h in range(N_HEADS_NSA):
        g = h // GROUP
        o_ref[0, :, h * HEAD_DIM:(h + 1) * HEAD_DIM] = o[h * t:(h + 1) * t, g * HEAD_DIM:(g + 1) * HEAD_DIM]


def _nsa_sample_attn(page_table, q3, gate3, o_cmp, sel, ksn, vsn, kwn, vwn, cwk, cwv, pool_k, pool_v,
                     bias_slc, bias_new, bias_win, expand):
    n, n_pages = page_table.shape
    t = q3.shape[1]
    past = n_pages * PAGE_SIZE
    wb = cwk.shape[3]
    width = sel.shape[3]
    key_chunk = min(KEY_CHUNK, past)
    assert past % key_chunk == 0 and key_chunk % PAGE_SIZE == 0 and t <= NEW_PAD
    per_seq = lambda shape: pl.BlockSpec((1,) + shape, lambda i, pt: (i,) + (0,) * len(shape))
    full = lambda a: pl.BlockSpec(a.shape, lambda i, pt: (0,) * a.ndim)
    anyspec = pl.BlockSpec(memory_space=pl.ANY)
    grid_spec = pltpu.PrefetchScalarGridSpec(
        num_scalar_prefetch=1,
        grid=(n,),
        in_specs=[per_seq((t, NSA_WIDTH)), per_seq((t, LANES)), per_seq((N_KV_NSA, GROUP * t, HEAD_DIM)),
                  per_seq((N_KV_NSA, t, width)),
                  per_seq((t, KV_WIDTH)), per_seq((t, KV_WIDTH)), per_seq((t, KV_WIDTH)), per_seq((t, KV_WIDTH)),
                  per_seq((N_KV_NSA, HEAD_DIM, wb)), per_seq((N_KV_NSA, HEAD_DIM, wb)), anyspec, anyspec,
                  full(bias_slc), full(bias_new), full(bias_win), full(expand)],
        out_specs=per_seq((t, NSA_WIDTH)),
        scratch_shapes=[pltpu.VMEM((2, n_pages, N_KV_NSA, HEAD_DIM, PAGE_SIZE), F32),
                        pltpu.VMEM((2, n_pages, N_KV_NSA, HEAD_DIM, PAGE_SIZE), F32),
                        pltpu.SemaphoreType.DMA((2,)), pltpu.SemaphoreType.DMA((2,)),
                        pltpu.VMEM((N_KV_NSA * GROUP * t, past), F32)],
    )
    return pl.pallas_call(
        functools.partial(_nsa_sample_attn_kernel, n_pages=n_pages, key_chunk=key_chunk),
        out_shape=jax.ShapeDtypeStruct((n, t, NSA_WIDTH), F32),
        grid_spec=grid_spec,
        compiler_params=_cparams(("arbitrary",)),
        name="nsa_sample_attn",
    )(page_table, q3, gate3, o_cmp, sel, ksn, vsn, kwn, vwn, cwk, cwv, pool_k, pool_v,
      bias_slc, bias_new, bias_win, expand)


def _segsum(x, ones_ref):
    hi = x.astype(BF16)
    lo = (x - hi.astype(F32)).astype(BF16)
    return _dot(hi, ones_ref[...]) + _dot(lo, ones_ref[...])


def _softplus(x):
    return jnp.maximum(x, 0.0) + jnp.log(1.0 + jnp.exp(-jnp.abs(x)))


def _rwkv_pre_kernel(p_ref, prev_ref, mu_ref, w0_ref, w2_ref, a0_ref, a2_ref, g2_ref, kk_ref, ka_ref, ones_ref,
                     r_ref, k_ref, v_ref, kkn_ref, kka_ref, d_ref, g_ref, last_ref, buf_ref):
    ti = pl.program_id(1)
    tt = p_ref.shape[1]
    p = p_ref[0]
    first = jnp.where(ti == 0, prev_ref[0], last_ref[...])
    last_ref[...] = p[tt - 1:tt]
    buf_ref[SUBLANES - 1:SUBLANES, :] = first
    buf_ref[SUBLANES:SUBLANES + tt, :] = p
    prev = buf_ref[SUBLANES - 1:SUBLANES - 1 + tt, :]
    xs = p + (prev - p) * mu_ref[...]
    W = RWKV_WIDTH
    r, k, v = xs[:, 0:W], xs[:, W:2 * W], xs[:, 2 * W:3 * W]
    c = 3 * W
    xw = xs[:, c:c + DECAY_LORA]
    xa = xs[:, c + DECAY_LORA:c + DECAY_LORA + AAA_LORA]
    xg = xs[:, c + DECAY_LORA + AAA_LORA:]
    w = -_softplus(-(w0_ref[...] + _dot(jnp.tanh(xw).astype(BF16), w2_ref[...]))) - 0.5
    a = jax.nn.sigmoid(a0_ref[...] + _dot(xa.astype(BF16), a2_ref[...]))
    kk = k * kk_ref[...]
    kk = kk / jnp.maximum(jnp.sqrt(_segsum(kk * kk, ones_ref)), 1e-12)
    r_ref[0] = r
    k_ref[0] = k * (1.0 + (a - 1.0) * ka_ref[...])
    v_ref[0] = v
    kkn_ref[0] = kk
    kka_ref[0] = kk * a
    d_ref[0] = -jnp.exp(w)
    g_ref[0] = _dot(jax.nn.sigmoid(xg).astype(BF16), g2_ref[...])


def _rwkv_pre(p3, prev, mu, w0, w2, a0, a2, g2, k_k, k_a, ones_bd):
    n, t, c = p3.shape
    tt = min(t, 256)
    assert t % tt == 0
    row = lambda a: a.reshape(1, -1)
    full = lambda a: pl.BlockSpec(a.shape, lambda b, i: (0,) * a.ndim)
    params = [row(mu), row(w0), w2, row(a0), a2, g2, row(k_k), row(k_a), ones_bd]
    out_spec = pl.BlockSpec((1, tt, RWKV_WIDTH), lambda b, i: (b, i, 0))
    return pl.pallas_call(
        _rwkv_pre_kernel,
        out_shape=[jax.ShapeDtypeStruct((n, t, RWKV_WIDTH), F32)] * 7,
        grid=(n, t // tt),
        in_specs=[pl.BlockSpec((1, tt, c), lambda b, i: (b, i, 0)),
                  pl.BlockSpec((1, 1, c), lambda b, i: (b, 0, 0))] + [full(a) for a in params],
        out_specs=[out_spec] * 7,
        scratch_shapes=[pltpu.VMEM((1, c), F32), pltpu.VMEM((SUBLANES + tt, c), F32)],
        compiler_params=_cparams(("parallel", "arbitrary")),
        name="rwkv_pre",
    )(p3, prev.reshape(n, 1, c), *params)


def _rwkv_scan_kernel(r_ref, k_ref, v_ref, kk_ref, kka_ref, ld_ref, s0_ref, y_ref, sout_ref, s_ref, *, C):
    ci = pl.program_id(1)
    t_in = r_ref.shape[1]

    @pl.when(ci == 0)
    def _():
        s_ref[...] = s0_ref[0]

    def rows(ref):
        x = ref[0]
        if t_in < C:
            x = jnp.concatenate([x, jnp.zeros((C - t_in, x.shape[1]), F32)], axis=0)
        return x

    ld = rows(ld_ref)
    tri = jnp.where(lax.broadcasted_iota(jnp.int32, (C, C), 1) <= lax.broadcasted_iota(jnp.int32, (C, C), 0),
                    1.0, 0.0).astype(BF16)
    hi = ld.astype(BF16)
    rem = ld - hi.astype(F32)
    mid = rem.astype(BF16)
    lo = (rem - mid.astype(F32)).astype(BF16)
    cum = _dot(tri, hi) + _dot(tri, mid) + _dot(tri, lo)
    p_inc = jnp.exp(cum)
    p_inv = jnp.exp(-cum)
    a_t = -rows(kk_ref) * jnp.exp(cum - ld)
    b_t = rows(kka_ref) * p_inv
    k_t = rows(k_ref) * p_inv
    r_t = rows(r_ref) * p_inc
    v = rows(v_ref)
    p_end = p_inc[C - 1:C]

    i2 = lax.broadcasted_iota(jnp.int32, (2 * C, 2 * C), 0)
    j2 = lax.broadcasted_iota(jnp.int32, (2 * C, 2 * C), 1)
    src = j2 & (C - 1)
    keep = src < jnp.where(i2 < C, i2, i2 - C + 1)

    heads = range(N_HEADS_RWKV)
    cols = [slice(h * HEAD_DIM, (h + 1) * HEAD_DIM) for h in heads]
    ar = [jnp.concatenate([a_t[:, c], r_t[:, c]], axis=0).astype(BF16) for c in cols]
    bk = [jnp.concatenate([b_t[:, c], k_t[:, c]], axis=0).astype(BF16) for c in cols]
    vh = [v[:, c].astype(BF16) for c in cols]
    s0 = [s_ref[h] for h in heads]
    quad = [jnp.where(keep, _dot_nt(ar[h], bk[h]), 0.0).astype(BF16) for h in heads]
    base = [_dot_nt(ar[h], s0[h].astype(BF16)) for h in heads]
    x = [base[h][:C] + _dot(quad[h][:C, C:], vh[h]) for h in heads]
    power = [quad[h][:C, :C] for h in heads]
    levels = C.bit_length() - 1
    for lvl in range(levels):
        x = [x[h] + _dot(power[h], x[h].astype(BF16)) for h in heads]
        if lvl + 1 < levels:
            power = [_dot(power[h], power[h]).astype(BF16) for h in heads]
    uv = [jnp.concatenate([x[h].astype(BF16), vh[h]], axis=0) for h in heads]
    ys = [base[h][C:] + _dot(quad[h][C:, :], uv[h]) for h in heads]
    for h in heads:
        upd = lax.dot_general(uv[h], bk[h], (((0,), (0,)), ((), ())), preferred_element_type=F32)
        s_ref[h] = (s0[h] + upd) * p_end[:, cols[h]]
    y = jnp.concatenate(ys, axis=1)
    y_ref[0] = y[:t_in]

    @pl.when(ci == pl.num_programs(1) - 1)
    def _():
        sout_ref[0] = s_ref[...]


def _rwkv_scan(r, k, v, kk, kka, d, s0):
    n, t, w = r.shape
    tc = min(t, TC)
    chunk = TC if t >= TC else max(SUBLANES, 1 << (t - 1).bit_length())
    assert t % tc == 0 and chunk & (chunk - 1) == 0
    seq_spec = pl.BlockSpec((1, tc, w), lambda b, i: (b, i, 0))
    st_spec = pl.BlockSpec((1, N_HEADS_RWKV, HEAD_DIM, HEAD_DIM), lambda b, i: (b, 0, 0, 0))
    return pl.pallas_call(
        functools.partial(_rwkv_scan_kernel, C=chunk),
        out_shape=[jax.ShapeDtypeStruct((n, t, w), F32),
                   jax.ShapeDtypeStruct((n, N_HEADS_RWKV, HEAD_DIM, HEAD_DIM), F32)],
        grid=(n, t // tc),
        in_specs=[seq_spec] * 6 + [st_spec],
        out_specs=[seq_spec, st_spec],
        scratch_shapes=[pltpu.VMEM((N_HEADS_RWKV, HEAD_DIM, HEAD_DIM), F32)],
        compiler_params=_cparams(("parallel", "arbitrary")),
        name="rwkv_scan",
    )(r, k, v, kk, kka, d, s0)


def _mix_out_kernel(x_ref, nsa_ref, y_ref, r_ref, k_ref, v_ref, g_ref, rk_ref, lnw_ref, lnb_ref, ones_ref,
                    wa_ref, wb_ref, o_ref):
    y = y_ref[...]
    mean = _segsum(y, ones_ref) * (1.0 / HEAD_DIM)
    yc = y - mean
    var = _segsum(yc * yc, ones_ref) * (1.0 / HEAD_DIM)
    yn = yc * lax.rsqrt(var + GN_EPS) * lnw_ref[...] + lnb_ref[...]
    bonus = _segsum(r_ref[...] * k_ref[...] * rk_ref[...], ones_ref) * v_ref[...]
    rw = ((yn + bonus) * g_ref[...]).astype(BF16)
    o_ref[...] = x_ref[...] + _dot(nsa_ref[...].astype(BF16), wa_ref[...]) + _dot(rw, wb_ref[...])


def _mix_out(x, o_nsa, y, r, k, v, g, r_k, ln_w, ln_b, ones_bd, wa, wb):
    m, d = x.shape
    tm = min(m, 512)
    assert m % tm == 0
    row = lambda a: a.reshape(1, -1)
    tile = lambda w: pl.BlockSpec((tm, w), lambda i: (i, 0))
    full = lambda a: pl.BlockSpec(a.shape, lambda i: (0,) * a.ndim)
    params = [row(r_k), row(ln_w), row(ln_b), ones_bd, wa, wb]
    return pl.pallas_call(
        _mix_out_kernel,
        out_shape=jax.ShapeDtypeStruct((m, d), F32),
        grid=(m // tm,),
        in_specs=[tile(d), tile(NSA_WIDTH)] + [tile(RWKV_WIDTH)] * 5 + [full(a) for a in params],
        out_specs=tile(d),
        compiler_params=_cparams(("parallel",)),
        name="mix_out",
    )(x, o_nsa, y, r, k, v, g, *params)


def _block_diag2(w):
    z = jnp.zeros_like(w)
    return jnp.concatenate([jnp.concatenate([w, z], axis=-1), jnp.concatenate([z, w], axis=-1)], axis=-2)


def _compress_weights_t(pe, w1, w2):
    per_page = PAGE_SIZE // CMP_BLOCK
    return (jnp.concatenate([pe.T] * per_page, axis=1), _block_diag2(jnp.transpose(w1, (1, 0, 2))).astype(BF16),
            _block_diag2(w2).astype(BF16))


def _compress_weights(pe, w1, w2):
    return (jnp.concatenate([pe] * N_KV_NSA, axis=1), _block_diag2(w1).astype(BF16), _block_diag2(w2).astype(BF16))


def kernel(x_prompt, x_sample, cache_cmp_k, cache_cmp_v, cache_slc_k, cache_slc_v, cache_win_k, cache_win_v,
           state_wkv, state_shift, page_table, rel_bias_table, ffn1_norm, ffn1_wg, ffn1_wu, ffn1_wd, mix_norm,
           w_in, cmp_pe_k, cmp_w1_k, cmp_w2_k, cmp_pe_v, cmp_w1_v, cmp_w2_v, shift_mu, decay_w0, decay_w2,
           aaa_a0, aaa_a2, gate_g2, k_k, k_a, r_k, ln_x_w, ln_x_b, w_out, ffn2_norm, ffn2_wg, ffn2_wu, ffn2_wd,
           final_norm):
    bp, seq, d = x_prompt.shape
    bs, ts, _ = x_sample.shape
    depth = w_in.shape[0]
    n_pages = page_table.shape[1]
    past = n_pages * PAGE_SIZE
    n_phys = cache_cmp_k.shape[1]
    wbuf = cache_win_k.shape[2]
    assert seq % QT == 0 and seq >= WINDOW and ts < CMP_BLOCK and wbuf == WINDOW
    nq = seq // QT
    nb = seq // SLC_BLOCK
    nbp = past // SLC_BLOCK
    nsa_main = NSA_WIDTH + 6 * KV_WIDTH
    nsa_cols = nsa_main + N_GATES

    bias_tiles = _bias_tiles(rel_bias_table)
    bias_cmp_p = _bias_rows(rel_bias_table, q0=0, rows=seq, n_keys=nb, key_first=CMP_BLOCK - 1, key_stride=CMP_BLOCK)
    bias_cmp_p = bias_cmp_p.reshape(N_KV_NSA, GROUP, nq, QT, nb).transpose(0, 2, 4, 1, 3).reshape(
        N_KV_NSA, nq, nb, GROUP * QT)
    bias_cmp_s = _bias_rows(rel_bias_table, q0=past, rows=ts, n_keys=nbp, key_first=CMP_BLOCK - 1,
                            key_stride=CMP_BLOCK).reshape(N_KV_NSA, GROUP * ts, nbp)
    stacked = lambda b: b.reshape(N_HEADS_NSA * ts, b.shape[-1])
    bias_slc_s = stacked(_bias_rows(rel_bias_table, q0=past, rows=ts, n_keys=past, key_first=0, key_stride=1))
    bias_new_s = stacked(_bias_rows(rel_bias_table, q0=past, rows=ts, n_keys=LANES, key_first=past, key_stride=1))
    bias_win_s = stacked(_bias_rows(rel_bias_table, q0=past, rows=ts, n_keys=wbuf,
                                    key_first=past - wbuf, key_stride=1))
    expand_s = (jnp.arange(past, dtype=jnp.int32)[None, :] // SLC_BLOCK
                == jnp.arange(nbp, dtype=jnp.int32)[:, None]).astype(BF16)
    head_of = jnp.arange(RWKV_WIDTH, dtype=jnp.int32) // HEAD_DIM
    ones_bd = (head_of[:, None] == head_of[None, :]).astype(BF16)

    xp = x_prompt.reshape(bp * seq, d)
    xs = x_sample.reshape(bs * ts, d)
    states = []
    for l in range(depth):
        bf = lambda a: a[l].astype(BF16)
        w_in_l = w_in[l]
        w_pad = jnp.concatenate(
            [w_in_l[:, :nsa_main], w_in_l[:, nsa_main:nsa_cols], jnp.zeros((d, LANES - N_GATES), F32),
             w_in_l[:, nsa_cols:]], axis=1).astype(BF16)
        cw_k = _compress_weights(cmp_pe_k[l], cmp_w1_k[l], cmp_w2_k[l])
        cw_v = _compress_weights(cmp_pe_v[l], cmp_w1_v[l], cmp_w2_v[l])
        wa, wb = w_out[l][:NSA_WIDTH].astype(BF16), w_out[l][NSA_WIDTH:].astype(BF16)
        rw_pre = (shift_mu[l], decay_w0[l], bf(decay_w2), aaa_a0[l], bf(aaa_a2), bf(gate_g2), k_k[l], k_a[l], ones_bd)
        ffn1 = (ffn1_norm[l], bf(ffn1_wg), bf(ffn1_wu), bf(ffn1_wd), final_norm)
        ffn2 = (ffn2_norm[l], bf(ffn2_wg), bf(ffn2_wu), bf(ffn2_wd), final_norm)
        last = l == depth - 1

        xp = _ffn_half(xp, *ffn1, final=False)
        xs = _ffn_half(xs, *ffn1, final=False)

        q, kc, vc, ks, vs, kw, vw, gate, zr, *state_t = _proj(xp, mix_norm[l], w_pad, seq=seq)
        kcc = _compress(kc, *cw_k)
        vcc = _compress(vc, *cw_v)
        o_nsa = _nsa_prompt(q, gate, ks, vs, kw, vw, kcc, vcc, bias_tiles, bias_cmp_p, bp, seq)
        zr3 = zr.reshape(bp, seq, RWKV_COLS)
        r, k2, v, kk, kka, dec, gg = _rwkv_pre(zr3, jnp.zeros((bp, RWKV_COLS), F32), *rw_pre)
        y, wkv_p = _rwkv_scan(r, k2, v, kk, kka, dec,
                              jnp.zeros((bp, N_HEADS_RWKV, HEAD_DIM, HEAD_DIM), F32))
        flat = lambda a: a.reshape(-1, RWKV_WIDTH)
        xp = _mix_out(xp, o_nsa, flat(y), flat(r), flat(k2), flat(v), flat(gg), r_k[l], ln_x_w[l], ln_x_b[l],
                      ones_bd, wa, wb)
        kv5 = lambda a, n_, t_: a.reshape(n_, t_, N_KV_NSA, HEAD_DIM)
        token_major = lambda a: jnp.transpose(a.reshape(bp, N_KV_NSA, HEAD_DIM, seq), (0, 3, 1, 2))
        p_state = (*(token_major(a) for a in state_t),
                   kv5(kw, bp, seq)[:, seq - WINDOW:], kv5(vw, bp, seq)[:, seq - WINDOW:],
                   wkv_p, zr3[:, seq - 1])

        q, kc, vc, ks, vs, kw, vw, gate, zr = _proj(xs, mix_norm[l], w_pad)
        feature_major = lambda c: jnp.transpose(c[l], (0, 2, 3, 1))
        kcc, vcc = _compress_paged(page_table, feature_major(cache_cmp_k), feature_major(cache_cmp_v),
                                   _compress_weights_t(cmp_pe_k[l], cmp_w1_k[l], cmp_w2_k[l]),
                                   _compress_weights_t(cmp_pe_v[l], cmp_w1_v[l], cmp_w2_v[l]))
        seq3 = lambda a: a.reshape(bs, ts, a.shape[-1])
        o_cmp, sel = _nsa_sample_cmp(seq3(q), kcc, vcc, bias_cmp_s, past)
        cwk = feature_major(cache_win_k)
        cwv = feature_major(cache_win_v)
        o_nsa = _nsa_sample_attn(page_table, seq3(q), seq3(gate), o_cmp, sel, seq3(ks), seq3(vs), seq3(kw),
                                 seq3(vw), cwk, cwv, feature_major(cache_slc_k), feature_major(cache_slc_v),
                                 bias_slc_s, bias_new_s, bias_win_s, expand_s)
        zr3 = seq3(zr)
        r, k2, v, kk, kka, dec, gg = _rwkv_pre(zr3, state_shift[l], *rw_pre)
        y, wkv_s = _rwkv_scan(r, k2, v, kk, kka, dec, state_wkv[l])
        xs = _mix_out(xs, o_nsa.reshape(bs * ts, NSA_WIDTH), flat(y), flat(r), flat(k2), flat(v), flat(gg),
                      r_k[l], ln_x_w[l], ln_x_b[l], ones_bd, wa, wb)
        win = lambda cache, new: jnp.concatenate([cache[l], kv5(new, bs, ts)], axis=1)[:, ts:]
        s_state = (kv5(kc, bs, ts), kv5(vc, bs, ts), kv5(ks, bs, ts), kv5(vs, bs, ts),
                   win(cache_win_k, kw), win(cache_win_v, vw), wkv_s, zr3[:, ts - 1])

        xp = _ffn_half(xp, *ffn2, final=last)
        xs = _ffn_half(xs, *ffn2, final=last)
        states.append(p_state + s_state)

    stacked_states = [jnp.stack(z) for z in zip(*states)]
    return (xp.reshape(bp, seq, d), xs.reshape(bs, ts, d), *stacked_states)
```

```python
import functools
import math

import jax
import jax.numpy as jnp
import numpy as np
from jax import lax
from jax.experimental import pallas as pl
from jax.experimental.pallas import tpu as pltpu

F32 = jnp.float32
BF16 = jnp.bfloat16

HEAD_DIM = 64
N_HEADS_NSA = 8
N_KV_NSA = 2
GROUP = N_HEADS_NSA // N_KV_NSA
N_HEADS_RWKV = 8
NSA_WIDTH = N_HEADS_NSA * HEAD_DIM
RWKV_WIDTH = N_HEADS_RWKV * HEAD_DIM
KV_WIDTH = N_KV_NSA * HEAD_DIM
CMP_BLOCK = 64
SLC_BLOCK = 64
TOP_K_BLOCKS = 16
WINDOW = 512
PAGE_SIZE = 128
N_BUCKETS = 32
MAX_DISTANCE = 128
DECAY_LORA = 64
AAA_LORA = 64
GATE_LORA = 128
NORM_EPS = 1e-6
GN_EPS = 64e-5
ATTN_SCALE = HEAD_DIM ** -0.5
FORCE_SCORE = 1e4
NEG_INF = -1e30
M_FLOOR = -1e29
RWKV_COLS = 3 * RWKV_WIDTH + DECAY_LORA + AAA_LORA + GATE_LORA
N_GATES = 3 * N_HEADS_NSA

LANES = 128
SUBLANES = 8
VMEM_LIMIT = 56 * 1024 * 1024

QT = 128
TC = 64


def _bucket_thresholds():
    max_exact = N_BUCKETS // 2
    thr = list(range(max_exact + 1))
    for b in range(max_exact + 1, N_BUCKETS):
        x = max_exact * (MAX_DISTANCE / max_exact) ** ((b - max_exact) / (N_BUCKETS - max_exact))
        thr.append(int(math.ceil(x - 1e-9)))
    return thr


BUCKET_THR = _bucket_thresholds()


def _cparams(sem):
    return pltpu.CompilerParams(dimension_semantics=sem, vmem_limit_bytes=VMEM_LIMIT)


def _dot(a, b):
    return jnp.dot(a, b, preferred_element_type=F32)


def _dot_nt(a, b):
    return lax.dot_general(a, b, (((1,), (1,)), ((), ())), preferred_element_type=F32)


def _rms(x, g):
    return x * lax.rsqrt(jnp.mean(x * x, axis=-1, keepdims=True) + NORM_EPS) * g


def _ffn_kernel(x_ref, g_ref, wg_ref, wu_ref, wd_ref, gf_ref, o_ref, h_ref, acc_ref, *, final):
    j = pl.program_id(1)

    @pl.when(j == 0)
    def _():
        h_ref[...] = _rms(x_ref[...], g_ref[...]).astype(BF16)
        acc_ref[...] = jnp.zeros_like(acc_ref)

    h = h_ref[...]
    a = _dot(h, wg_ref[...])
    b = _dot(h, wu_ref[...])
    act = (a * jax.nn.sigmoid(a) * b).astype(BF16)
    acc_ref[...] += _dot(act, wd_ref[...])

    @pl.when(j == pl.num_programs(1) - 1)
    def _():
        y = x_ref[...] + 0.5 * acc_ref[...]
        if final:
            y = _rms(y, gf_ref[...])
        o_ref[...] = y


def _ffn_half(x, g, wg, wu, wd, gf, *, final):
    m, d = x.shape
    ff = wg.shape[1]
    tm = min(m, 1024)
    tf = 256 if ff % 256 == 0 else ff
    assert m % tm == 0 and ff % tf == 0
    return pl.pallas_call(
        functools.partial(_ffn_kernel, final=final),
        out_shape=jax.ShapeDtypeStruct((m, d), F32),
        grid=(m // tm, ff // tf),
        in_specs=[
            pl.BlockSpec((tm, d), lambda i, j: (i, 0)),
            pl.BlockSpec((1, d), lambda i, j: (0, 0)),
            pl.BlockSpec((d, tf), lambda i, j: (0, j)),
            pl.BlockSpec((d, tf), lambda i, j: (0, j)),
            pl.BlockSpec((tf, d), lambda i, j: (j, 0)),
            pl.BlockSpec((1, d), lambda i, j: (0, 0)),
        ],
        out_specs=pl.BlockSpec((tm, d), lambda i, j: (i, 0)),
        scratch_shapes=[pltpu.VMEM((tm, d), BF16), pltpu.VMEM((tm, d), F32)],
        compiler_params=_cparams(("parallel", "arbitrary")),
        name="ffn_half",
    )(x, g.reshape(1, d), wg, wu, wd, gf.reshape(1, d))


N_STATE_ROWS = 4


def _proj_kernel(x_ref, g_ref, w_ref, q_ref, kc_ref, vc_ref, ks_ref, vs_ref, kw_ref, vw_ref, gate_ref, zr_ref,
                 *state_t_refs):
    h = _rms(x_ref[...], g_ref[...]).astype(BF16)
    z = _dot(h, w_ref[...])
    c = 0
    for ref in (q_ref, kc_ref, vc_ref, ks_ref, vs_ref, kw_ref, vw_ref, gate_ref, zr_ref):
        w = ref.shape[1]
        ref[...] = z[:, c:c + w]
        c += w
    for i, ref in enumerate(state_t_refs):
        c = NSA_WIDTH + i * KV_WIDTH
        ref[0] = z[:, c:c + KV_WIDTH].T


def _proj(x, g, w_pad, seq=None):
    m, d = x.shape
    tm = min(m, 512)
    widths = (NSA_WIDTH,) + (KV_WIDTH,) * 6 + (LANES, RWKV_COLS)
    assert sum(widths) == w_pad.shape[1] and m % tm == 0
    out_shape = [jax.ShapeDtypeStruct((m, w), F32) for w in widths]
    out_specs = [pl.BlockSpec((tm, w), lambda i: (i, 0)) for w in widths]
    if seq is not None:
        assert seq % tm == 0
        per_seq = seq // tm
        out_shape += [jax.ShapeDtypeStruct((m // seq, KV_WIDTH, seq), F32)] * N_STATE_ROWS
        out_specs += [pl.BlockSpec((1, KV_WIDTH, tm), lambda i: (i // per_seq, 0, i % per_seq))] * N_STATE_ROWS
    return pl.pallas_call(
        _proj_kernel,
        out_shape=out_shape,
        grid=(m // tm,),
        in_specs=[
            pl.BlockSpec((tm, d), lambda i: (i, 0)),
            pl.BlockSpec((1, d), lambda i: (0, 0)),
            pl.BlockSpec(w_pad.shape, lambda i: (0, 0)),
        ],
        out_specs=out_specs,
        compiler_params=_cparams(("parallel",)),
        name="in_proj",
    )(x, g.reshape(1, d), w_pad)


def _out_kernel(x_ref, a_ref, b_ref, wa_ref, wb_ref, o_ref):
    o_ref[...] = (x_ref[...] + _dot(a_ref[...].astype(BF16), wa_ref[...])
                  + _dot(b_ref[...].astype(BF16), wb_ref[...]))


def _out_proj(x, a, b, wa, wb):
    m, d = x.shape
    tm = min(m, 512)
    return pl.pallas_call(
        _out_kernel,
        out_shape=jax.ShapeDtypeStruct((m, d), F32),
        grid=(m // tm,),
        in_specs=[
            pl.BlockSpec((tm, d), lambda i: (i, 0)),
            pl.BlockSpec((tm, a.shape[1]), lambda i: (i, 0)),
            pl.BlockSpec((tm, b.shape[1]), lambda i: (i, 0)),
            pl.BlockSpec(wa.shape, lambda i: (0, 0)),
            pl.BlockSpec(wb.shape, lambda i: (0, 0)),
        ],
        out_specs=pl.BlockSpec((tm, d), lambda i: (i, 0)),
        compiler_params=_cparams(("parallel",)),
        name="out_proj",
    )(x, a, b, wa, wb)


def _bias_lookup(dist, table_ref, heads):
    vals = [jnp.full(dist.shape, table_ref[0, h], F32) for h in heads]
    for b in range(1, N_BUCKETS):
        ge = dist >= BUCKET_THR[b]
        vals = [jnp.where(ge, table_ref[b, h], v) for h, v in zip(heads, vals)]
    return vals


def _bias_tiles_kernel(table_ref, o_ref):
    j = lax.broadcasted_iota(jnp.int32, (QT, QT), 0)
    i = lax.broadcasted_iota(jnp.int32, (QT, QT), 1)
    for off in range(3):
        vals = _bias_lookup(off * QT + i - j, table_ref, range(N_HEADS_NSA))
        for h, v in enumerate(vals):
            g, r = divmod(h, GROUP)
            o_ref[g, off, :, r * QT:(r + 1) * QT] = v


def _bias_tiles(table):
    return pl.pallas_call(
        _bias_tiles_kernel,
        out_shape=jax.ShapeDtypeStruct((N_KV_NSA, 3, QT, GROUP * QT), F32),
        in_specs=[pl.BlockSpec(memory_space=pltpu.SMEM)],
        name="bias_tiles",
    )(table)


def _bias_rows_kernel(table_ref, o_ref, *, q0, key_first, key_stride):
    _, _, rc, L = o_ref.shape
    i = lax.broadcasted_iota(jnp.int32, (rc, L), 0) + pl.program_id(0) * rc
    j = lax.broadcasted_iota(jnp.int32, (rc, L), 1)
    dist = (q0 + i) - (key_first + key_stride * j)
    vals = _bias_lookup(dist, table_ref, range(N_HEADS_NSA))
    for h, v in enumerate(vals):
        g, r = divmod(h, GROUP)
        o_ref[g, r] = v


def _bias_rows(table, *, q0, rows, n_keys, key_first, key_stride):
    rc = min(rows, 256)
    assert rows % rc == 0
    return pl.pallas_call(
        functools.partial(_bias_rows_kernel, q0=q0, key_first=key_first, key_stride=key_stride),
        out_shape=jax.ShapeDtypeStruct((N_KV_NSA, GROUP, rows, n_keys), F32),
        grid=(rows // rc,),
        in_specs=[pl.BlockSpec(memory_space=pltpu.SMEM)],
        out_specs=pl.BlockSpec((N_KV_NSA, GROUP, rc, n_keys), lambda i: (0, 0, i, 0)),
        compiler_params=_cparams(("parallel",)),
        name="bias_rows",
    )(table)


def _gelu_tanh(x):
    return 0.5 * x * (1.0 + jnp.tanh(math.sqrt(2.0 / math.pi) * (x + 0.044715 * (x * x * x))))


def _compress_blocks(load_rows, nblk, pe_ref, w1_ref, w2_ref):
    acc = jnp.zeros((nblk, KV_WIDTH), F32)
    for c in range(CMP_BLOCK):
        rows = load_rows(c) + pe_ref[c:c + 1, :]
        acc = acc + _dot(rows.astype(BF16), w1_ref[c])
    return _dot(_gelu_tanh(acc).astype(BF16), w2_ref[...])


def _compress_kernel(x_ref, pe_ref, w1_ref, w2_ref, o_ref):
    nblk = o_ref.shape[0]
    o_ref[...] = _compress_blocks(lambda c: x_ref[pl.ds(c, nblk, stride=CMP_BLOCK), :],
                                  nblk, pe_ref, w1_ref, w2_ref)


def _compress(x, pe2, w1bd, w2bd):
    rows = x.shape[0]
    nb = rows // CMP_BLOCK
    nblk = min(nb, 128)
    assert nb % nblk == 0
    return pl.pallas_call(
        _compress_kernel,
        out_shape=jax.ShapeDtypeStruct((nb, KV_WIDTH), F32),
        grid=(nb // nblk,),
        in_specs=[
            pl.BlockSpec((nblk * CMP_BLOCK, KV_WIDTH), lambda i: (i, 0)),
            pl.BlockSpec(pe2.shape, lambda i: (0, 0)),
            pl.BlockSpec(w1bd.shape, lambda i: (0, 0, 0)),
            pl.BlockSpec(w2bd.shape, lambda i: (0, 0)),
        ],
        out_specs=pl.BlockSpec((nblk, KV_WIDTH), lambda i: (i, 0)),
        compiler_params=_cparams(("parallel",)),
        name="compress",
    )(x, pe2, w1bd, w2bd)


def _page_copies_by_feature(pt_ref, n, pool_ref, buf_ref, sem_ref, slot, n_pages):
    return [pltpu.make_async_copy(pool_ref.at[pt_ref[n, p]], buf_ref.at[slot, :, :, p], sem_ref.at[slot])
            for p in range(n_pages)]


def _page_copies_by_page(pt_ref, n, pool_ref, buf_ref, sem_ref, slot, n_pages):
    return [pltpu.make_async_copy(pool_ref.at[pt_ref[n, p]], buf_ref.at[slot, p], sem_ref.at[slot])
            for p in range(n_pages)]


def _paged_prefetch(pt_ref, pools, bufs, sems, n_pages, _page_copies):
    n = pl.program_id(0)
    slot = n % 2

    def start(seq, slot_):
        for pool, buf, sem in zip(pools, bufs, sems):
            for cp in _page_copies(pt_ref, seq, pool, buf, sem, slot_, n_pages):
                cp.start()

    @pl.when(n == 0)
    def _():
        start(0, 0)

    @pl.when(n + 1 < pl.num_programs(0))
    def _():
        start(n + 1, 1 - slot)

    for pool, buf, sem in zip(pools, bufs, sems):
        for cp in _page_copies(pt_ref, n, pool, buf, sem, slot, n_pages):
            cp.wait()
    return slot


def _compress_paged_kernel(pt_ref, pk_ref, pv_ref, pek_ref, w1k_ref, w2k_ref, pev_ref, w1v_ref, w2v_ref,
                           ok_ref, ov_ref, bk_ref, bv_ref, sk_ref, sv_ref, *, n_pages):
    slot = _paged_prefetch(pt_ref, (pk_ref, pv_ref), (bk_ref, bv_ref), (sk_ref, sv_ref), n_pages,
                           _page_copies_by_feature)

    def compress(buf_ref, pe_ref, w1_ref, w2_ref):
        acc = [jnp.zeros((n_pages, PAGE_SIZE), F32) for _ in range(N_KV_NSA)]
        for d in range(HEAD_DIM):
            for g in range(N_KV_NSA):
                rows = buf_ref[slot, g, d] + pe_ref[d:d + 1, :]
                acc[g] = acc[g] + _dot(rows.astype(BF16), w1_ref[d])
        out = [_dot(_gelu_tanh(a).astype(BF16), w2_ref[...]) for a in acc]
        return jnp.concatenate([out[g][:, j * HEAD_DIM:(j + 1) * HEAD_DIM]
                                for j in range(PAGE_SIZE // CMP_BLOCK) for g in range(N_KV_NSA)], axis=1)

    ok_ref[0] = compress(bk_ref, pek_ref, w1k_ref, w2k_ref)
    ov_ref[0] = compress(bv_ref, pev_ref, w1v_ref, w2v_ref)


def _compress_paged(page_table, pool_k, pool_v, cw_k, cw_v):
    n, n_pages = page_table.shape
    past = n_pages * PAGE_SIZE
    nblk = past // CMP_BLOCK
    per_page = PAGE_SIZE // CMP_BLOCK
    assert per_page == 2
    wspec = lambda a: pl.BlockSpec(a.shape, lambda i, pt: (0,) * a.ndim)
    grid_spec = pltpu.PrefetchScalarGridSpec(
        num_scalar_prefetch=1,
        grid=(n,),
        in_specs=[pl.BlockSpec(memory_space=pl.ANY), pl.BlockSpec(memory_space=pl.ANY)]
                 + [wspec(a) for a in cw_k + cw_v],
        out_specs=[pl.BlockSpec((1, n_pages, per_page * KV_WIDTH), lambda i, pt: (i, 0, 0))] * 2,
        scratch_shapes=[pltpu.VMEM((2, N_KV_NSA, HEAD_DIM, n_pages, PAGE_SIZE), F32),
                        pltpu.VMEM((2, N_KV_NSA, HEAD_DIM, n_pages, PAGE_SIZE), F32),
                        pltpu.SemaphoreType.DMA((2,)), pltpu.SemaphoreType.DMA((2,))],
    )
    kcc, vcc = pl.pallas_call(
        functools.partial(_compress_paged_kernel, n_pages=n_pages),
        out_shape=[jax.ShapeDtypeStruct((n, n_pages, per_page * KV_WIDTH), F32)] * 2,
        grid_spec=grid_spec,
        compiler_params=_cparams(("arbitrary",)),
        name="compress_paged",
    )(page_table, pool_k, pool_v, *cw_k, *cw_v)
    return kcc.reshape(n, nblk, KV_WIDTH), vcc.reshape(n, nblk, KV_WIDTH)


def _stack_heads(x, g):
    return jnp.concatenate(
        [x[:, (g * GROUP + r) * HEAD_DIM:(g * GROUP + r + 1) * HEAD_DIM] for r in range(GROUP)], axis=0)


def _stack_cols(x, cols):
    return jnp.concatenate([x[:, c:c + 1] for c in cols], axis=0)


def _softmax_step(carry, s, mask, v):
    m, l, acc = carry
    s = jnp.where(mask, s, NEG_INF)
    m_new = jnp.maximum(m, jnp.max(s, axis=1, keepdims=True))
    p = jnp.where(mask, jnp.exp(s - m_new), 0.0)
    alpha = jnp.exp(m - m_new)
    l = alpha * l + jnp.sum(p, axis=1, keepdims=True)
    pv = v(p.astype(BF16)) if callable(v) else _dot(p.astype(BF16), v)
    return m_new, l, alpha * acc + pv


def _softmax_init(rows, width):
    return (jnp.full((rows, 1), NEG_INF, F32), jnp.zeros((rows, 1), F32), jnp.zeros((rows, width), F32))


def _softmax_done(carry):
    _, l, acc = carry
    return acc / jnp.maximum(l, 1e-30)


def _topk_mask(imp, cur, n_blocks):
    blk = lax.broadcasted_iota(jnp.int32, imp.shape, 1)
    forced = (blk == 0) | (blk == cur) | (blk == cur - 1)
    score = jnp.where(blk <= cur, jnp.where(forced, FORCE_SCORE, imp), -FORCE_SCORE)
    rank = jnp.zeros(imp.shape, jnp.int32)
    for b in range(n_blocks):
        col = score[:, b:b + 1]
        ahead = (col > score) | ((col == score) & (blk > b))
        rank = rank + jnp.where(ahead, 1, 0)
    return (rank < TOP_K_BLOCKS) & (blk <= cur)


def _topk_mask_t(imp, cur, n_blocks):
    blk = lax.broadcasted_iota(jnp.int32, imp.shape, 0)
    forced = (blk == 0) | (blk == cur) | (blk == cur - 1)
    score = jnp.where(blk <= cur, jnp.where(forced, FORCE_SCORE, imp), -FORCE_SCORE)
    rank = jnp.zeros(imp.shape, jnp.int32)
    for b in range(n_blocks):
        row = score[b:b + 1, :]
        rank = rank + jnp.where(row > score, 1, jnp.where(row == score, jnp.where(blk > b, 1, 0), 0))
    return jnp.where(rank < TOP_K_BLOCKS, jnp.where(blk <= cur, 1.0, 0.0), 0.0)


def _nsa_prompt_kernel(q_ref, gate_ref, ks_ref, vs_ref, kw_ref, vw_ref, kcc_ref, vcc_ref, bt_ref, bc_ref,
                       o_ref, sel_ref, m_ref, acc_ref, pv_ref, s_ref):
    qi = pl.program_id(1)
    seq = ks_ref.shape[1]
    nb = seq // SLC_BLOCK
    rows = GROUP * QT
    groups = range(N_KV_NSA)
    q = q_ref[...] * ATTN_SCALE
    gates_t = jax.nn.sigmoid(gate_ref[...]).T
    tpos_row = qi * QT + lax.broadcasted_iota(jnp.int32, (1, QT), 1)
    lane = lax.broadcasted_iota(jnp.int32, (1, KV_WIDTH), 1)
    own = [(lane // HEAD_DIM) == g for g in groups]
    zeros = jnp.zeros((rows, HEAD_DIM), F32)
    qg = []
    for g in groups:
        qs = _stack_heads(q, g)
        qg.append(jnp.concatenate([qs, zeros] if g == 0 else [zeros, qs], axis=1).astype(BF16))

    blk = lax.broadcasted_iota(jnp.int32, (nb, 1), 0)
    tpos_stacked = qi * QT + (lax.broadcasted_iota(jnp.int32, (1, rows), 1) & (QT - 1))
    readable = (blk + 1) * CMP_BLOCK - 1 <= tpos_stacked
    kcc = kcc_ref[0].astype(BF16)
    vcc = vcc_ref[0].astype(BF16)
    o_cmp = []
    for g in groups:
        s = jnp.where(readable, _dot_nt(kcc, qg[g]) + bc_ref[g, 0], NEG_INF)
        e = jnp.where(readable, jnp.exp(s - jnp.max(s, axis=0, keepdims=True)), 0.0)
        p = e / jnp.maximum(jnp.sum(e, axis=0, keepdims=True), 1e-30)
        o_cmp.append(lax.dot_general(vcc, p.astype(BF16), (((0,), (0,)), ((), ())), preferred_element_type=F32))
        imp = p[:, 0:QT]
        for r in range(1, GROUP):
            imp = imp + p[:, r * QT:(r + 1) * QT]
        sel_ref[g] = _topk_mask_t(imp, tpos_row // SLC_BLOCK, nb)

    def branch(lo, hi, k_ref, v_ref, penalty):
        def scores(kt):
            k = k_ref[0, pl.ds(pl.multiple_of(kt * QT, QT), QT), :].astype(BF16)
            return tuple(_dot_nt(k, qg[g]) for g in groups)

        def step(kt, _):
            s_next = scores(jnp.minimum(kt + 1, hi - 1))
            off = pl.multiple_of(kt * QT, QT)
            v = v_ref[0, pl.ds(off, QT), :]
            for g in groups:
                m = m_ref[g]
                s = s_ref[g] + bt_ref[g, jnp.minimum(qi - kt, 2)]
                s = s + jnp.concatenate([penalty(g, off)] * GROUP, axis=1)
                m_new = jnp.maximum(m, jnp.max(s, axis=0, keepdims=True))
                p = jnp.exp(s - m_new).astype(BF16)
                v_ext = jnp.where(own[g], v, 1.0).astype(BF16)
                acc_ref[g] = jnp.exp(m - m_new) * (acc_ref[g] + pv_ref[g])
                pv_ref[g] = lax.dot_general(v_ext, p, (((0,), (0,)), ((), ())), preferred_element_type=F32)
                m_ref[g] = m_new
            for g in groups:
                s_ref[g] = s_next[g]
            return 0

        first = scores(lo)
        for g in groups:
            m_ref[g] = jnp.full((1, rows), M_FLOOR, F32)
            acc_ref[g] = jnp.zeros((KV_WIDTH, rows), F32)
            pv_ref[g] = jnp.zeros((KV_WIDTH, rows), F32)
            s_ref[g] = first[g]
        lax.fori_loop(lo, hi, step, 0)
        out = []
        for g in groups:
            acc = acc_ref[g] + pv_ref[g]
            denom = acc[(1 - g) * HEAD_DIM:(1 - g) * HEAD_DIM + 1, :]
            out.append(acc / jnp.maximum(denom, 1e-30))
        return out

    key_row = lax.broadcasted_iota(jnp.int32, (QT, 1), 0)

    def slc_penalty(g, off):
        causal = jnp.where(tpos_row - (off + key_row) >= 0, 0.0, NEG_INF)
        first = off // SLC_BLOCK
        chosen = jnp.where(key_row < SLC_BLOCK, sel_ref[g, pl.ds(first, 1), :], sel_ref[g, pl.ds(first + 1, 1), :])
        return jnp.where(chosen > 0.5, causal, NEG_INF)

    def win_penalty(g, off):
        dist = tpos_row - (off + key_row)
        return jnp.where(dist >= 0, jnp.where(dist <= WINDOW, 0.0, NEG_INF), NEG_INF)

    slc = branch(0, qi + 1, ks_ref, vs_ref, slc_penalty)
    win = branch(jnp.maximum(qi - WINDOW // QT, 0), qi + 1, kw_ref, vw_ref, win_penalty)

    def gate_row(branch, g):
        first = branch * N_HEADS_NSA + g * GROUP
        return jnp.concatenate([gates_t[first + r:first + r + 1, :] for r in range(GROUP)], axis=1)

    o_t = [gate_row(0, g) * o_cmp[g] + gate_row(1, g) * slc[g] + gate_row(2, g) * win[g] for g in groups]
    feature_row = lax.broadcasted_iota(jnp.int32, (KV_WIDTH, 1), 0)
    o = jnp.where(feature_row < HEAD_DIM, o_t[0], o_t[1]).T
    for g in groups:
        for r in range(GROUP):
            h = g * GROUP + r
            o_ref[:, h * HEAD_DIM:(h + 1) * HEAD_DIM] = o[r * QT:(r + 1) * QT, g * HEAD_DIM:(g + 1) * HEAD_DIM]


def _nsa_prompt(q, gate, ks, vs, kw, vw, kcc, vcc, bias_tiles, bias_cmp, n, seq):
    nq = seq // QT
    nb = seq // SLC_BLOCK
    seq_spec = pl.BlockSpec((1, seq, KV_WIDTH), lambda b, i: (b, 0, 0))
    cmp_spec = pl.BlockSpec((1, nb, KV_WIDTH), lambda b, i: (b, 0, 0))
    return pl.pallas_call(
        _nsa_prompt_kernel,
        out_shape=jax.ShapeDtypeStruct((n * seq, NSA_WIDTH), F32),
        grid=(n, nq),
        in_specs=[
            pl.BlockSpec((QT, NSA_WIDTH), lambda b, i: (b * nq + i, 0)),
            pl.BlockSpec((QT, LANES), lambda b, i: (b * nq + i, 0)),
            seq_spec, seq_spec, seq_spec, seq_spec, cmp_spec, cmp_spec,
            pl.BlockSpec(bias_tiles.shape, lambda b, i: (0, 0, 0, 0)),
            pl.BlockSpec((N_KV_NSA, 1, nb, GROUP * QT), lambda b, i: (0, i, 0, 0)),
        ],
        out_specs=pl.BlockSpec((QT, NSA_WIDTH), lambda b, i: (b * nq + i, 0)),
        scratch_shapes=[pltpu.VMEM((N_KV_NSA, nb, QT), F32),
                        pltpu.VMEM((N_KV_NSA, 1, GROUP * QT), F32),
                        pltpu.VMEM((N_KV_NSA, KV_WIDTH, GROUP * QT), F32),
                        pltpu.VMEM((N_KV_NSA, KV_WIDTH, GROUP * QT), F32),
                        pltpu.VMEM((N_KV_NSA, QT, GROUP * QT), F32)],
        compiler_params=_cparams(("parallel", "arbitrary")),
        name="nsa_prompt",
    )(q, gate, ks.reshape(n, seq, KV_WIDTH), vs.reshape(n, seq, KV_WIDTH), kw.reshape(n, seq, KV_WIDTH),
      vw.reshape(n, seq, KV_WIDTH), kcc.reshape(n, nb, KV_WIDTH), vcc.reshape(n, nb, KV_WIDTH),
      bias_tiles, bias_cmp)


NEW_PAD = SUBLANES


def _row_token(t):
    return jnp.concatenate([lax.broadcasted_iota(jnp.int32, (t, 1), 0)] * GROUP, axis=0)


def _nsa_sample_cmp_kernel(q_ref, kcc_ref, vcc_ref, bc_ref, o_ref, sel_ref, *, past):
    t = q_ref.shape[1]
    nbp = kcc_ref.shape[1]
    width = sel_ref.shape[3]
    q = q_ref[0]
    pos = past + _row_token(t)
    blk_end = (lax.broadcasted_iota(jnp.int32, (1, nbp), 1) + 1) * CMP_BLOCK - 1
    cur = (past + lax.broadcasted_iota(jnp.int32, (t, 1), 0)) // SLC_BLOCK
    for g in range(N_KV_NSA):
        cols = slice(g * HEAD_DIM, (g + 1) * HEAD_DIM)
        qg = _stack_heads(q, g).astype(BF16)
        s = _dot_nt(qg, kcc_ref[0][:, cols].astype(BF16)) * ATTN_SCALE + bc_ref[g]
        mask = blk_end <= pos
        s = jnp.where(mask, s, NEG_INF)
        e = jnp.where(mask, jnp.exp(s - jnp.max(s, axis=1, keepdims=True)), 0.0)
        p = e / jnp.maximum(jnp.sum(e, axis=1, keepdims=True), 1e-30)
        o_ref[0, g] = _dot(p.astype(BF16), vcc_ref[0][:, cols].astype(BF16))
        imp = p[0:t]
        for r in range(1, GROUP):
            imp = imp + p[r * t:(r + 1) * t]
        imp = jnp.concatenate([imp, jnp.zeros((t, width - nbp), F32)], axis=1)
        sel = _topk_mask(imp, cur, nbp + 1)
        sel_ref[0, g] = jnp.where(sel, 1.0, 0.0)


def _nsa_sample_cmp(q3, kcc, vcc, bias_cmp, past):
    n, t, _ = q3.shape
    nbp = kcc.shape[1]
    width = -(-(nbp + 1) // LANES) * LANES
    return pl.pallas_call(
        functools.partial(_nsa_sample_cmp_kernel, past=past),
        out_shape=[jax.ShapeDtypeStruct((n, N_KV_NSA, GROUP * t, HEAD_DIM), F32),
                   jax.ShapeDtypeStruct((n, N_KV_NSA, t, width), F32)],
        grid=(n,),
        in_specs=[
            pl.BlockSpec((1, t, NSA_WIDTH), lambda i: (i, 0, 0)),
            pl.BlockSpec((1, nbp, KV_WIDTH), lambda i: (i, 0, 0)),
            pl.BlockSpec((1, nbp, KV_WIDTH), lambda i: (i, 0, 0)),
            pl.BlockSpec(bias_cmp.shape, lambda i: (0, 0, 0)),
        ],
        out_specs=[pl.BlockSpec((1, N_KV_NSA, GROUP * t, HEAD_DIM), lambda i: (i, 0, 0, 0)),
                   pl.BlockSpec((1, N_KV_NSA, t, width), lambda i: (i, 0, 0, 0))],
        compiler_params=_cparams(("parallel",)),
        name="nsa_sample_cmp",
    )(q3, kcc, vcc, bias_cmp)


KEY_CHUNK = 1024


def _nsa_sample_attn_kernel(pt_ref, q_ref, gate_ref, ocmp_ref, sel_ref, ksn_ref, vsn_ref, kwn_ref, vwn_ref,
                            cwk_ref, cwv_ref, pk_ref, pv_ref, bs_ref, bn_ref, bw_ref, ex_ref,
                            o_ref, bk_ref, bv_ref, sk_ref, sv_ref, selx_ref, *, n_pages, key_chunk):
    slot = _paged_prefetch(pt_ref, (pk_ref, pv_ref), (bk_ref, bv_ref), (sk_ref, sv_ref), n_pages,
                           _page_copies_by_page)
    t = q_ref.shape[1]
    rows = N_KV_NSA * GROUP * t
    past = n_pages * PAGE_SIZE
    nbp = past // SLC_BLOCK
    wb = cwk_ref.shape[3]
    groups = range(N_KV_NSA)
    q = q_ref[0] * ATTN_SCALE
    gates = jax.nn.sigmoid(gate_ref[0])
    zeros = jnp.zeros((GROUP * t, HEAD_DIM), F32)

    def own_lanes(x, g):
        return jnp.concatenate([x, zeros] if g == 0 else [zeros, x], axis=1)

    qa = jnp.concatenate([own_lanes(_stack_heads(q, g), g) for g in groups], axis=0).astype(BF16)
    sel = jnp.concatenate([sel_ref[0, g] for g in groups for _ in range(GROUP)], axis=0)
    selx_ref[...] = _dot(sel[:, :nbp].astype(BF16), ex_ref[...])
    tok = jnp.concatenate([_row_token(t)] * N_KV_NSA, axis=0)
    jn = lax.broadcasted_iota(jnp.int32, (1, NEW_PAD), 1)
    new_mask = jn <= tok
    jw = lax.broadcasted_iota(jnp.int32, (1, wb), 1)
    wdist = wb + tok - jw
    win_mask = (wdist >= 0) & (wdist <= WINDOW)
    bias_new = bn_ref[:, :NEW_PAD]
    pad = jnp.zeros((NEW_PAD - t, KV_WIDTH), F32)

    def new_rows(ref):
        return jnp.concatenate([ref[0], pad], axis=0).astype(BF16)

    pages_per_chunk = key_chunk // PAGE_SIZE
    qg = [_stack_heads(q, g).astype(BF16) for g in groups]

    def scores(k_t):
        return jnp.concatenate([_dot(qg[g], k_t(g)) for g in groups], axis=0)

    def values(v_t):
        return lambda p: jnp.concatenate(
            [own_lanes(_dot_nt(p[g * GROUP * t:(g + 1) * GROUP * t], v_t(g)), g) for g in groups], axis=0)

    def slc_step(c, carry):
        off = pl.multiple_of(c * key_chunk, key_chunk)

        def chunk(buf_ref):
            return lambda g: jnp.concatenate([buf_ref[slot, c * pages_per_chunk + i, g]
                                              for i in range(pages_per_chunk)], axis=1).astype(BF16)

        s = scores(chunk(bk_ref)) + bs_ref[:, pl.ds(off, key_chunk)]
        return _softmax_step(carry, s, selx_ref[:, pl.ds(off, key_chunk)] > 0.5, values(chunk(bv_ref)))

    carry = lax.fori_loop(0, past // key_chunk, slc_step, _softmax_init(rows, KV_WIDTH))
    s = _dot_nt(qa, new_rows(ksn_ref)) + bias_new
    carry = _softmax_step(carry, s, new_mask & (sel[:, nbp:nbp + 1] > 0.5), new_rows(vsn_ref))
    o_slc = _softmax_done(carry)

    s = scores(lambda g: cwk_ref[0, g].astype(BF16)) + bw_ref[...]
    carry = _softmax_step(_softmax_init(rows, KV_WIDTH), s, win_mask, values(lambda g: cwv_ref[0, g].astype(BF16)))
    s = _dot_nt(qa, new_rows(kwn_ref)) + bias_new
    carry = _softmax_step(carry, s, new_mask, new_rows(vwn_ref))
    o_win = _softmax_done(carry)

    def gate_col(branch):
        return _stack_cols(gates, [branch * N_HEADS_NSA + h for h in range(N_HEADS_NSA)])

    o_cmp = jnp.concatenate([own_lanes(ocmp_ref[0, g], g) for g in groups], axis=0)
    o = gate_col(0) * o_cmp + gate_col(1) * o_slc + gate_col(2) * o_win
    for h in range(N_HEADS_NSA):
        g = h // GROUP
        o_ref[0, :, h * HEAD_DIM:(h + 1) * HEAD_DIM] = o[h * t:(h + 1) * t, g * HEAD_DIM:(g + 1) * HEAD_DIM]


def _nsa_sample_attn(page_table, q3, gate3, o_cmp, sel, ksn, vsn, kwn, vwn, cwk, cwv, pool_k, pool_v,
                     bias_slc, bias_new, bias_win, expand):
    n, n_pages = page_table.shape
    t = q3.shape[1]
    past = n_pages * PAGE_SIZE
    wb = cwk.shape[3]
    width = sel.shape[3]
    key_chunk = min(KEY_CHUNK, past)
    assert past % key_chunk == 0 and key_chunk % PAGE_SIZE == 0 and t <= NEW_PAD
    per_seq = lambda shape: pl.BlockSpec((1,) + shape, lambda i, pt: (i,) + (0,) * len(shape))
    full = lambda a: pl.BlockSpec(a.shape, lambda i, pt: (0,) * a.ndim)
    anyspec = pl.BlockSpec(memory_space=pl.ANY)
    grid_spec = pltpu.PrefetchScalarGridSpec(
        num_scalar_prefetch=1,
        grid=(n,),
        in_specs=[per_seq((t, NSA_WIDTH)), per_seq((t, LANES)), per_seq((N_KV_NSA, GROUP * t, HEAD_DIM)),
                  per_seq((N_KV_NSA, t, width)),
                  per_seq((t, KV_WIDTH)), per_seq((t, KV_WIDTH)), per_seq((t, KV_WIDTH)), per_seq((t, KV_WIDTH)),
                  per_seq((N_KV_NSA, HEAD_DIM, wb)), per_seq((N_KV_NSA, HEAD_DIM, wb)), anyspec, anyspec,
                  full(bias_slc), full(bias_new), full(bias_win), full(expand)],
        out_specs=per_seq((t, NSA_WIDTH)),
        scratch_shapes=[pltpu.VMEM((2, n_pages, N_KV_NSA, HEAD_DIM, PAGE_SIZE), F32),
                        pltpu.VMEM((2, n_pages, N_KV_NSA, HEAD_DIM, PAGE_SIZE), F32),
                        pltpu.SemaphoreType.DMA((2,)), pltpu.SemaphoreType.DMA((2,)),
                        pltpu.VMEM((N_KV_NSA * GROUP * t, past), F32)],
    )
    return pl.pallas_call(
        functools.partial(_nsa_sample_attn_kernel, n_pages=n_pages, key_chunk=key_chunk),
        out_shape=jax.ShapeDtypeStruct((n, t, NSA_WIDTH), F32),
        grid_spec=grid_spec,
        compiler_params=_cparams(("arbitrary",)),
        name="nsa_sample_attn",
    )(page_table, q3, gate3, o_cmp, sel, ksn, vsn, kwn, vwn, cwk, cwv, pool_k, pool_v,
      bias_slc, bias_new, bias_win, expand)


def _segsum(x, ones_ref):
    hi = x.astype(BF16)
    lo = (x - hi.astype(F32)).astype(BF16)
    return _dot(hi, ones_ref[...]) + _dot(lo, ones_ref[...])


def _softplus(x):
    return jnp.maximum(x, 0.0) + jnp.log(1.0 + jnp.exp(-jnp.abs(x)))


def _rwkv_pre_kernel(p_ref, prev_ref, mu_ref, w0_ref, w2_ref, a0_ref, a2_ref, g2_ref, kk_ref, ka_ref, ones_ref,
                     r_ref, k_ref, v_ref, kkn_ref, kka_ref, d_ref, g_ref, last_ref, buf_ref):
    ti = pl.program_id(1)
    tt = p_ref.shape[1]
    p = p_ref[0]
    first = jnp.where(ti == 0, prev_ref[0], last_ref[...])
    last_ref[...] = p[tt - 1:tt]
    buf_ref[SUBLANES - 1:SUBLANES, :] = first
    buf_ref[SUBLANES:SUBLANES + tt, :] = p
    prev = buf_ref[SUBLANES - 1:SUBLANES - 1 + tt, :]
    xs = p + (prev - p) * mu_ref[...]
    W = RWKV_WIDTH
    r, k, v = xs[:, 0:W], xs[:, W:2 * W], xs[:, 2 * W:3 * W]
    c = 3 * W
    xw = xs[:, c:c + DECAY_LORA]
    xa = xs[:, c + DECAY_LORA:c + DECAY_LORA + AAA_LORA]
    xg = xs[:, c + DECAY_LORA + AAA_LORA:]
    w = -_softplus(-(w0_ref[...] + _dot(jnp.tanh(xw).astype(BF16), w2_ref[...]))) - 0.5
    a = jax.nn.sigmoid(a0_ref[...] + _dot(xa.astype(BF16), a2_ref[...]))
    kk = k * kk_ref[...]
    kk = kk / jnp.maximum(jnp.sqrt(_segsum(kk * kk, ones_ref)), 1e-12)
    r_ref[0] = r
    k_ref[0] = k * (1.0 + (a - 1.0) * ka_ref[...])
    v_ref[0] = v
    kkn_ref[0] = kk
    kka_ref[0] = kk * a
    d_ref[0] = -jnp.exp(w)
    g_ref[0] = _dot(jax.nn.sigmoid(xg).astype(BF16), g2_ref[...])


def _rwkv_pre(p3, prev, mu, w0, w2, a0, a2, g2, k_k, k_a, ones_bd):
    n, t, c = p3.shape
    tt = min(t, 256)
    assert t % tt == 0
    row = lambda a: a.reshape(1, -1)
    full = lambda a: pl.BlockSpec(a.shape, lambda b, i: (0,) * a.ndim)
    params = [row(mu), row(w0), w2, row(a0), a2, g2, row(k_k), row(k_a), ones_bd]
    out_spec = pl.BlockSpec((1, tt, RWKV_WIDTH), lambda b, i: (b, i, 0))
    return pl.pallas_call(
        _rwkv_pre_kernel,
        out_shape=[jax.ShapeDtypeStruct((n, t, RWKV_WIDTH), F32)] * 7,
        grid=(n, t // tt),
        in_specs=[pl.BlockSpec((1, tt, c), lambda b, i: (b, i, 0)),
                  pl.BlockSpec((1, 1, c), lambda b, i: (b, 0, 0))] + [full(a) for a in params],
        out_specs=[out_spec] * 7,
        scratch_shapes=[pltpu.VMEM((1, c), F32), pltpu.VMEM((SUBLANES + tt, c), F32)],
        compiler_params=_cparams(("parallel", "arbitrary")),
        name="rwkv_pre",
    )(p3, prev.reshape(n, 1, c), *params)


def _rwkv_scan_kernel(r_ref, k_ref, v_ref, kk_ref, kka_ref, ld_ref, s0_ref, y_ref, sout_ref, s_ref, *, C):
    ci = pl.program_id(1)
    nseq, t_in = r_ref.shape[:2]

    @pl.when(ci == 0)
    def _():
        s_ref[...] = s0_ref[...]

    def rows(ref, b):
        x = ref[b]
        if t_in < C:
            x = jnp.concatenate([x, jnp.zeros((C - t_in, x.shape[1]), F32)], axis=0)
        return x

    tri = jnp.where(lax.broadcasted_iota(jnp.int32, (C, C), 1) <= lax.broadcasted_iota(jnp.int32, (C, C), 0),
                    1.0, 0.0).astype(BF16)
    i2 = lax.broadcasted_iota(jnp.int32, (2 * C, 2 * C), 0)
    j2 = lax.broadcasted_iota(jnp.int32, (2 * C, 2 * C), 1)
    src = j2 & (C - 1)
    keep = src < jnp.where(i2 < C, i2, i2 - C + 1)

    heads = range(nseq * N_HEADS_RWKV)
    ar, bk, vh, s0, p_end = [], [], [], [], []
    for b in range(nseq):
        ld = rows(ld_ref, b)
        hi = ld.astype(BF16)
        rem = ld - hi.astype(F32)
        mid = rem.astype(BF16)
        lo = (rem - mid.astype(F32)).astype(BF16)
        cum = _dot(tri, hi) + _dot(tri, mid) + _dot(tri, lo)
        p_inc = jnp.exp(cum)
        p_inv = jnp.exp(-cum)
        a_t = -rows(kk_ref, b) * jnp.exp(cum - ld)
        b_t = rows(kka_ref, b) * p_inv
        k_t = rows(k_ref, b) * p_inv
        r_t = rows(r_ref, b) * p_inc
        v = rows(v_ref, b)
        for h in range(N_HEADS_RWKV):
            c = slice(h * HEAD_DIM, (h + 1) * HEAD_DIM)
            ar.append(jnp.concatenate([a_t[:, c], r_t[:, c]], axis=0).astype(BF16))
            bk.append(jnp.concatenate([b_t[:, c], k_t[:, c]], axis=0).astype(BF16))
            vh.append(v[:, c].astype(BF16))
            s0.append(s_ref[b, h])
            p_end.append(p_inc[C - 1:C, c])
    quad = [jnp.where(keep, _dot_nt(ar[h], bk[h]), 0.0).astype(BF16) for h in heads]
    base = [_dot_nt(ar[h], s0[h].astype(BF16)) for h in heads]
    x = [base[h][:C] + _dot(quad[h][:C, C:], vh[h]) for h in heads]
    power = [quad[h][:C, :C] for h in heads]
    levels = C.bit_length() - 1
    for lvl in range(levels):
        x = [x[h] + _dot(power[h], x[h].astype(BF16)) for h in heads]
        if lvl + 1 < levels:
            power = [_dot(power[h], power[h]).astype(BF16) for h in heads]
    uv = [jnp.concatenate([x[h].astype(BF16), vh[h]], axis=0) for h in heads]
    ys = [base[h][C:] + _dot(quad[h][C:, :], uv[h]) for h in heads]
    for h in heads:
        upd = lax.dot_general(uv[h], bk[h], (((0,), (0,)), ((), ())), preferred_element_type=F32)
        s_ref[h // N_HEADS_RWKV, h % N_HEADS_RWKV] = (s0[h] + upd) * p_end[h]
    for b in range(nseq):
        y = jnp.concatenate(ys[b * N_HEADS_RWKV:(b + 1) * N_HEADS_RWKV], axis=1)
        y_ref[b] = y[:t_in]

    @pl.when(ci == pl.num_programs(1) - 1)
    def _():
        sout_ref[...] = s_ref[...]


SCAN_SEQS = 4


def _rwkv_scan(r, k, v, kk, kka, d, s0):
    n, t, w = r.shape
    tc = min(t, TC)
    chunk = TC if t >= TC else max(SUBLANES, 1 << (t - 1).bit_length())
    nseq = SCAN_SEQS if n % SCAN_SEQS == 0 else 1
    assert t % tc == 0 and chunk & (chunk - 1) == 0
    seq_spec = pl.BlockSpec((nseq, tc, w), lambda b, i: (b, i, 0))
    st_spec = pl.BlockSpec((nseq, N_HEADS_RWKV, HEAD_DIM, HEAD_DIM), lambda b, i: (b, 0, 0, 0))
    return pl.pallas_call(
        functools.partial(_rwkv_scan_kernel, C=chunk),
        out_shape=[jax.ShapeDtypeStruct((n, t, w), F32),
                   jax.ShapeDtypeStruct((n, N_HEADS_RWKV, HEAD_DIM, HEAD_DIM), F32)],
        grid=(n // nseq, t // tc),
        in_specs=[seq_spec] * 6 + [st_spec],
        out_specs=[seq_spec, st_spec],
        scratch_shapes=[pltpu.VMEM((nseq, N_HEADS_RWKV, HEAD_DIM, HEAD_DIM), F32)],
        compiler_params=_cparams(("parallel", "arbitrary")),
        name="rwkv_scan",
    )(r, k, v, kk, kka, d, s0)


def _mix_out_kernel(x_ref, nsa_ref, y_ref, r_ref, k_ref, v_ref, g_ref, rk_ref, lnw_ref, lnb_ref, ones_ref,
                    wa_ref, wb_ref, o_ref):
    y = y_ref[...]
    mean = _segsum(y, ones_ref) * (1.0 / HEAD_DIM)
    yc = y - mean
    var = _segsum(yc * yc, ones_ref) * (1.0 / HEAD_DIM)
    yn = yc * lax.rsqrt(var + GN_EPS) * lnw_ref[...] + lnb_ref[...]
    bonus = _segsum(r_ref[...] * k_ref[...] * rk_ref[...], ones_ref) * v_ref[...]
    rw = ((yn + bonus) * g_ref[...]).astype(BF16)
    o_ref[...] = x_ref[...] + _dot(nsa_ref[...].astype(BF16), wa_ref[...]) + _dot(rw, wb_ref[...])


def _mix_out(x, o_nsa, y, r, k, v, g, r_k, ln_w, ln_b, ones_bd, wa, wb):
    m, d = x.shape
    tm = min(m, 512)
    assert m % tm == 0
    row = lambda a: a.reshape(1, -1)
    tile = lambda w: pl.BlockSpec((tm, w), lambda i: (i, 0))
    full = lambda a: pl.BlockSpec(a.shape, lambda i: (0,) * a.ndim)
    params = [row(r_k), row(ln_w), row(ln_b), ones_bd, wa, wb]
    return pl.pallas_call(
        _mix_out_kernel,
        out_shape=jax.ShapeDtypeStruct((m, d), F32),
        grid=(m // tm,),
        in_specs=[tile(d), tile(NSA_WIDTH)] + [tile(RWKV_WIDTH)] * 5 + [full(a) for a in params],
        out_specs=tile(d),
        compiler_params=_cparams(("parallel",)),
        name="mix_out",
    )(x, o_nsa, y, r, k, v, g, *params)


def _block_diag2(w):
    z = jnp.zeros_like(w)
    return jnp.concatenate([jnp.concatenate([w, z], axis=-1), jnp.concatenate([z, w], axis=-1)], axis=-2)


def _compress_weights_t(pe, w1, w2):
    per_page = PAGE_SIZE // CMP_BLOCK
    return (jnp.concatenate([pe.T] * per_page, axis=1), _block_diag2(jnp.transpose(w1, (1, 0, 2))).astype(BF16),
            _block_diag2(w2).astype(BF16))


def _compress_weights(pe, w1, w2):
    return (jnp.concatenate([pe] * N_KV_NSA, axis=1), _block_diag2(w1).astype(BF16), _block_diag2(w2).astype(BF16))


def kernel(x_prompt, x_sample, cache_cmp_k, cache_cmp_v, cache_slc_k, cache_slc_v, cache_win_k, cache_win_v,
           state_wkv, state_shift, page_table, rel_bias_table, ffn1_norm, ffn1_wg, ffn1_wu, ffn1_wd, mix_norm,
           w_in, cmp_pe_k, cmp_w1_k, cmp_w2_k, cmp_pe_v, cmp_w1_v, cmp_w2_v, shift_mu, decay_w0, decay_w2,
           aaa_a0, aaa_a2, gate_g2, k_k, k_a, r_k, ln_x_w, ln_x_b, w_out, ffn2_norm, ffn2_wg, ffn2_wu, ffn2_wd,
           final_norm):
    bp, seq, d = x_prompt.shape
    bs, ts, _ = x_sample.shape
    depth = w_in.shape[0]
    n_pages = page_table.shape[1]
    past = n_pages * PAGE_SIZE
    n_phys = cache_cmp_k.shape[1]
    wbuf = cache_win_k.shape[2]
    assert seq % QT == 0 and seq >= WINDOW and ts < CMP_BLOCK and wbuf == WINDOW
    nq = seq // QT
    nb = seq // SLC_BLOCK
    nbp = past // SLC_BLOCK
    nsa_main = NSA_WIDTH + 6 * KV_WIDTH
    nsa_cols = nsa_main + N_GATES

    bias_tiles = _bias_tiles(rel_bias_table)
    bias_cmp_p = _bias_rows(rel_bias_table, q0=0, rows=seq, n_keys=nb, key_first=CMP_BLOCK - 1, key_stride=CMP_BLOCK)
    bias_cmp_p = bias_cmp_p.reshape(N_KV_NSA, GROUP, nq, QT, nb).transpose(0, 2, 4, 1, 3).reshape(
        N_KV_NSA, nq, nb, GROUP * QT)
    bias_cmp_s = _bias_rows(rel_bias_table, q0=past, rows=ts, n_keys=nbp, key_first=CMP_BLOCK - 1,
                            key_stride=CMP_BLOCK).reshape(N_KV_NSA, GROUP * ts, nbp)
    stacked = lambda b: b.reshape(N_HEADS_NSA * ts, b.shape[-1])
    bias_slc_s = stacked(_bias_rows(rel_bias_table, q0=past, rows=ts, n_keys=past, key_first=0, key_stride=1))
    bias_new_s = stacked(_bias_rows(rel_bias_table, q0=past, rows=ts, n_keys=LANES, key_first=past, key_stride=1))
    bias_win_s = stacked(_bias_rows(rel_bias_table, q0=past, rows=ts, n_keys=wbuf,
                                    key_first=past - wbuf, key_stride=1))
    expand_s = (jnp.arange(past, dtype=jnp.int32)[None, :] // SLC_BLOCK
                == jnp.arange(nbp, dtype=jnp.int32)[:, None]).astype(BF16)
    head_of = jnp.arange(RWKV_WIDTH, dtype=jnp.int32) // HEAD_DIM
    ones_bd = (head_of[:, None] == head_of[None, :]).astype(BF16)

    xp = x_prompt.reshape(bp * seq, d)
    xs = x_sample.reshape(bs * ts, d)
    states = []
    for l in range(depth):
        bf = lambda a: a[l].astype(BF16)
        w_in_l = w_in[l]
        w_pad = jnp.concatenate(
            [w_in_l[:, :nsa_main], w_in_l[:, nsa_main:nsa_cols], jnp.zeros((d, LANES - N_GATES), F32),
             w_in_l[:, nsa_cols:]], axis=1).astype(BF16)
        cw_k = _compress_weights(cmp_pe_k[l], cmp_w1_k[l], cmp_w2_k[l])
        cw_v = _compress_weights(cmp_pe_v[l], cmp_w1_v[l], cmp_w2_v[l])
        wa, wb = w_out[l][:NSA_WIDTH].astype(BF16), w_out[l][NSA_WIDTH:].astype(BF16)
        rw_pre = (shift_mu[l], decay_w0[l], bf(decay_w2), aaa_a0[l], bf(aaa_a2), bf(gate_g2), k_k[l], k_a[l], ones_bd)
        ffn1 = (ffn1_norm[l], bf(ffn1_wg), bf(ffn1_wu), bf(ffn1_wd), final_norm)
        ffn2 = (ffn2_norm[l], bf(ffn2_wg), bf(ffn2_wu), bf(ffn2_wd), final_norm)
        last = l == depth - 1

        xp = _ffn_half(xp, *ffn1, final=False)
        xs = _ffn_half(xs, *ffn1, final=False)

        q, kc, vc, ks, vs, kw, vw, gate, zr, *state_t = _proj(xp, mix_norm[l], w_pad, seq=seq)
        kcc = _compress(kc, *cw_k)
        vcc = _compress(vc, *cw_v)
        o_nsa = _nsa_prompt(q, gate, ks, vs, kw, vw, kcc, vcc, bias_tiles, bias_cmp_p, bp, seq)
        zr3 = zr.reshape(bp, seq, RWKV_COLS)
        r, k2, v, kk, kka, dec, gg = _rwkv_pre(zr3, jnp.zeros((bp, RWKV_COLS), F32), *rw_pre)
        y, wkv_p = _rwkv_scan(r, k2, v, kk, kka, dec,
                              jnp.zeros((bp, N_HEADS_RWKV, HEAD_DIM, HEAD_DIM), F32))
        flat = lambda a: a.reshape(-1, RWKV_WIDTH)
        xp = _mix_out(xp, o_nsa, flat(y), flat(r), flat(k2), flat(v), flat(gg), r_k[l], ln_x_w[l], ln_x_b[l],
                      ones_bd, wa, wb)
        kv5 = lambda a, n_, t_: a.reshape(n_, t_, N_KV_NSA, HEAD_DIM)
        token_major = lambda a: jnp.transpose(a.reshape(bp, N_KV_NSA, HEAD_DIM, seq), (0, 3, 1, 2))
        p_state = (*(token_major(a) for a in state_t),
                   kv5(kw, bp, seq)[:, seq - WINDOW:], kv5(vw, bp, seq)[:, seq - WINDOW:],
                   wkv_p, zr3[:, seq - 1])

        q, kc, vc, ks, vs, kw, vw, gate, zr = _proj(xs, mix_norm[l], w_pad)
        feature_major = lambda c: jnp.transpose(c[l], (0, 2, 3, 1))
        kcc, vcc = _compress_paged(page_table, feature_major(cache_cmp_k), feature_major(cache_cmp_v),
                                   _compress_weights_t(cmp_pe_k[l], cmp_w1_k[l], cmp_w2_k[l]),
                                   _compress_weights_t(cmp_pe_v[l], cmp_w1_v[l], cmp_w2_v[l]))
        seq3 = lambda a: a.reshape(bs, ts, a.shape[-1])
        o_cmp, sel = _nsa_sample_cmp(seq3(q), kcc, vcc, bias_cmp_s, past)
        cwk = feature_major(cache_win_k)
        cwv = feature_major(cache_win_v)
        o_nsa = _nsa_sample_attn(page_table, seq3(q), seq3(gate), o_cmp, sel, seq3(ks), seq3(vs), seq3(kw),
                                 seq3(vw), cwk, cwv, feature_major(cache_slc_k), feature_major(cache_slc_v),
                                 bias_slc_s, bias_new_s, bias_win_s, expand_s)
        zr3 = seq3(zr)
        r, k2, v, kk, kka, dec, gg = _rwkv_pre(zr3, state_shift[l], *rw_pre)
        y, wkv_s = _rwkv_scan(r, k2, v, kk, kka, dec, state_wkv[l])
        xs = _mix_out(xs, o_nsa.reshape(bs * ts, NSA_WIDTH), flat(y), flat(r), flat(k2), flat(v), flat(gg),
                      r_k[l], ln_x_w[l], ln_x_b[l], ones_bd, wa, wb)
        win = lambda cache, new: jnp.concatenate([cache[l], kv5(new, bs, ts)], axis=1)[:, ts:]
        s_state = (kv5(kc, bs, ts), kv5(vc, bs, ts), kv5(ks, bs, ts), kv5(vs, bs, ts),
                   win(cache_win_k, kw), win(cache_win_v, vw), wkv_s, zr3[:, ts - 1])

        xp = _ffn_half(xp, *ffn2, final=last)
        xs = _ffn_half(xs, *ffn2, final=last)
        states.append(p_state + s_state)

    stacked_states = [jnp.stack(z) for z in zip(*states)]
    return (xp.reshape(bp, seq, d), xs.reshape(bs, ts, d), *stacked_states)
```

```python
import functools
import math

import jax
import jax.numpy as jnp
import numpy as np
from jax import lax
from jax.experimental import pallas as pl
from jax.experimental.pallas import tpu as pltpu

F32 = jnp.float32
BF16 = jnp.bfloat16

HEAD_DIM = 64
N_HEADS_NSA = 8
N_KV_NSA = 2
GROUP = N_HEADS_NSA // N_KV_NSA
N_HEADS_RWKV = 8
NSA_WIDTH = N_HEADS_NSA * HEAD_DIM
RWKV_WIDTH = N_HEADS_RWKV * HEAD_DIM
KV_WIDTH = N_KV_NSA * HEAD_DIM
CMP_BLOCK = 64
SLC_BLOCK = 64
TOP_K_BLOCKS = 16
WINDOW = 512
PAGE_SIZE = 128
N_BUCKETS = 32
MAX_DISTANCE = 128
DECAY_LORA = 64
AAA_LORA = 64
GATE_LORA = 128
NORM_EPS = 1e-6
GN_EPS = 64e-5
ATTN_SCALE = HEAD_DIM ** -0.5
FORCE_SCORE = 1e4
NEG_INF = -1e30
M_FLOOR = -1e29
RWKV_COLS = 3 * RWKV_WIDTH + DECAY_LORA + AAA_LORA + GATE_LORA
N_GATES = 3 * N_HEADS_NSA

LANES = 128
SUBLANES = 8
VMEM_LIMIT = 56 * 1024 * 1024

QT = 128
TC = 64


def _bucket_thresholds():
    max_exact = N_BUCKETS // 2
    thr = list(range(max_exact + 1))
    for b in range(max_exact + 1, N_BUCKETS):
        x = max_exact * (MAX_DISTANCE / max_exact) ** ((b - max_exact) / (N_BUCKETS - max_exact))
        thr.append(int(math.ceil(x - 1e-9)))
    return thr


BUCKET_THR = _bucket_thresholds()


def _cparams(sem):
    return pltpu.CompilerParams(dimension_semantics=sem, vmem_limit_bytes=VMEM_LIMIT)


def _dot(a, b):
    return jnp.dot(a, b, preferred_element_type=F32)


def _dot_nt(a, b):
    return lax.dot_general(a, b, (((1,), (1,)), ((), ())), preferred_element_type=F32)


def _rms(x, g):
    return x * lax.rsqrt(jnp.mean(x * x, axis=-1, keepdims=True) + NORM_EPS) * g


def _ffn_kernel(x_ref, g_ref, wg_ref, wu_ref, wd_ref, gf_ref, o_ref, h_ref, acc_ref, *, final):
    j = pl.program_id(1)

    @pl.when(j == 0)
    def _():
        h_ref[...] = _rms(x_ref[...], g_ref[...]).astype(BF16)
        acc_ref[...] = jnp.zeros_like(acc_ref)

    h = h_ref[...]
    a = _dot(h, wg_ref[...])
    b = _dot(h, wu_ref[...])
    act = (a * jax.nn.sigmoid(a) * b).astype(BF16)
    acc_ref[...] += _dot(act, wd_ref[...])

    @pl.when(j == pl.num_programs(1) - 1)
    def _():
        y = x_ref[...] + 0.5 * acc_ref[...]
        if final:
            y = _rms(y, gf_ref[...])
        o_ref[...] = y


def _ffn_half(x, g, wg, wu, wd, gf, *, final):
    m, d = x.shape
    ff = wg.shape[1]
    tm = min(m, 1024)
    tf = 256 if ff % 256 == 0 else ff
    assert m % tm == 0 and ff % tf == 0
    return pl.pallas_call(
        functools.partial(_ffn_kernel, final=final),
        out_shape=jax.ShapeDtypeStruct((m, d), F32),
        grid=(m // tm, ff // tf),
        in_specs=[
            pl.BlockSpec((tm, d), lambda i, j: (i, 0)),
            pl.BlockSpec((1, d), lambda i, j: (0, 0)),
            pl.BlockSpec((d, tf), lambda i, j: (0, j)),
            pl.BlockSpec((d, tf), lambda i, j: (0, j)),
            pl.BlockSpec((tf, d), lambda i, j: (j, 0)),
            pl.BlockSpec((1, d), lambda i, j: (0, 0)),
        ],
        out_specs=pl.BlockSpec((tm, d), lambda i, j: (i, 0)),
        scratch_shapes=[pltpu.VMEM((tm, d), BF16), pltpu.VMEM((tm, d), F32)],
        compiler_params=_cparams(("parallel", "arbitrary")),
        name="ffn_half",
    )(x, g.reshape(1, d), wg, wu, wd, gf.reshape(1, d))


N_STATE_ROWS = 4


def _proj_kernel(x_ref, g_ref, w_ref, q_ref, kc_ref, vc_ref, ks_ref, vs_ref, kw_ref, vw_ref, gate_ref, zr_ref,
                 *state_t_refs):
    h = _rms(x_ref[...], g_ref[...]).astype(BF16)
    z = _dot(h, w_ref[...])
    c = 0
    for ref in (q_ref, kc_ref, vc_ref, ks_ref, vs_ref, kw_ref, vw_ref, gate_ref, zr_ref):
        w = ref.shape[1]
        ref[...] = z[:, c:c + w]
        c += w
    for i, ref in enumerate(state_t_refs):
        c = NSA_WIDTH + i * KV_WIDTH
        ref[0] = z[:, c:c + KV_WIDTH].T


def _proj(x, g, w_pad, seq=None):
    m, d = x.shape
    tm = min(m, 512)
    widths = (NSA_WIDTH,) + (KV_WIDTH,) * 6 + (LANES, RWKV_COLS)
    assert sum(widths) == w_pad.shape[1] and m % tm == 0
    out_shape = [jax.ShapeDtypeStruct((m, w), F32) for w in widths]
    out_specs = [pl.BlockSpec((tm, w), lambda i: (i, 0)) for w in widths]
    if seq is not None:
        assert seq % tm == 0
        per_seq = seq // tm
        out_shape += [jax.ShapeDtypeStruct((m // seq, KV_WIDTH, seq), F32)] * N_STATE_ROWS
        out_specs += [pl.BlockSpec((1, KV_WIDTH, tm), lambda i: (i // per_seq, 0, i % per_seq))] * N_STATE_ROWS
    return pl.pallas_call(
        _proj_kernel,
        out_shape=out_shape,
        grid=(m // tm,),
        in_specs=[
            pl.BlockSpec((tm, d), lambda i: (i, 0)),
            pl.BlockSpec((1, d), lambda i: (0, 0)),
            pl.BlockSpec(w_pad.shape, lambda i: (0, 0)),
        ],
        out_specs=out_specs,
        compiler_params=_cparams(("parallel",)),
        name="in_proj",
    )(x, g.reshape(1, d), w_pad)


def _out_kernel(x_ref, a_ref, b_ref, wa_ref, wb_ref, o_ref):
    o_ref[...] = (x_ref[...] + _dot(a_ref[...].astype(BF16), wa_ref[...])
                  + _dot(b_ref[...].astype(BF16), wb_ref[...]))


def _out_proj(x, a, b, wa, wb):
    m, d = x.shape
    tm = min(m, 512)
    return pl.pallas_call(
        _out_kernel,
        out_shape=jax.ShapeDtypeStruct((m, d), F32),
        grid=(m // tm,),
        in_specs=[
            pl.BlockSpec((tm, d), lambda i: (i, 0)),
            pl.BlockSpec((tm, a.shape[1]), lambda i: (i, 0)),
            pl.BlockSpec((tm, b.shape[1]), lambda i: (i, 0)),
            pl.BlockSpec(wa.shape, lambda i: (0, 0)),
            pl.BlockSpec(wb.shape, lambda i: (0, 0)),
        ],
        out_specs=pl.BlockSpec((tm, d), lambda i: (i, 0)),
        compiler_params=_cparams(("parallel",)),
        name="out_proj",
    )(x, a, b, wa, wb)


def _bias_lookup(dist, table_ref, heads):
    vals = [jnp.full(dist.shape, table_ref[0, h], F32) for h in heads]
    for b in range(1, N_BUCKETS):
        ge = dist >= BUCKET_THR[b]
        vals = [jnp.where(ge, table_ref[b, h], v) for h, v in zip(heads, vals)]
    return vals


def _bias_tiles_kernel(table_ref, o_ref):
    j = lax.broadcasted_iota(jnp.int32, (QT, QT), 0)
    i = lax.broadcasted_iota(jnp.int32, (QT, QT), 1)
    for off in range(3):
        vals = _bias_lookup(off * QT + i - j, table_ref, range(N_HEADS_NSA))
        for h, v in enumerate(vals):
            g, r = divmod(h, GROUP)
            o_ref[g, off, :, r * QT:(r + 1) * QT] = v


def _bias_tiles(table):
    return pl.pallas_call(
        _bias_tiles_kernel,
        out_shape=jax.ShapeDtypeStruct((N_KV_NSA, 3, QT, GROUP * QT), F32),
        in_specs=[pl.BlockSpec(memory_space=pltpu.SMEM)],
        name="bias_tiles",
    )(table)


def _bias_rows_kernel(table_ref, o_ref, *, q0, key_first, key_stride):
    _, _, rc, L = o_ref.shape
    i = lax.broadcasted_iota(jnp.int32, (rc, L), 0) + pl.program_id(0) * rc
    j = lax.broadcasted_iota(jnp.int32, (rc, L), 1)
    dist = (q0 + i) - (key_first + key_stride * j)
    vals = _bias_lookup(dist, table_ref, range(N_HEADS_NSA))
    for h, v in enumerate(vals):
        g, r = divmod(h, GROUP)
        o_ref[g, r] = v


def _bias_rows(table, *, q0, rows, n_keys, key_first, key_stride):
    rc = min(rows, 256)
    assert rows % rc == 0
    return pl.pallas_call(
        functools.partial(_bias_rows_kernel, q0=q0, key_first=key_first, key_stride=key_stride),
        out_shape=jax.ShapeDtypeStruct((N_KV_NSA, GROUP, rows, n_keys), F32),
        grid=(rows // rc,),
        in_specs=[pl.BlockSpec(memory_space=pltpu.SMEM)],
        out_specs=pl.BlockSpec((N_KV_NSA, GROUP, rc, n_keys), lambda i: (0, 0, i, 0)),
        compiler_params=_cparams(("parallel",)),
        name="bias_rows",
    )(table)


def _gelu_tanh(x):
    return 0.5 * x * (1.0 + jnp.tanh(math.sqrt(2.0 / math.pi) * (x + 0.044715 * (x * x * x))))


def _compress_blocks(load_rows, nblk, pe_ref, w1_ref, w2_ref):
    acc = jnp.zeros((nblk, KV_WIDTH), F32)
    for c in range(CMP_BLOCK):
        rows = load_rows(c) + pe_ref[c:c + 1, :]
        acc = acc + _dot(rows.astype(BF16), w1_ref[c])
    return _dot(_gelu_tanh(acc).astype(BF16), w2_ref[...])


def _compress_kernel(x_ref, pe_ref, w1_ref, w2_ref, o_ref):
    nblk = o_ref.shape[0]
    o_ref[...] = _compress_blocks(lambda c: x_ref[pl.ds(c, nblk, stride=CMP_BLOCK), :],
                                  nblk, pe_ref, w1_ref, w2_ref)


def _compress(x, pe2, w1bd, w2bd):
    rows = x.shape[0]
    nb = rows // CMP_BLOCK
    nblk = min(nb, 128)
    assert nb % nblk == 0
    return pl.pallas_call(
        _compress_kernel,
        out_shape=jax.ShapeDtypeStruct((nb, KV_WIDTH), F32),
        grid=(nb // nblk,),
        in_specs=[
            pl.BlockSpec((nblk * CMP_BLOCK, KV_WIDTH), lambda i: (i, 0)),
            pl.BlockSpec(pe2.shape, lambda i: (0, 0)),
            pl.BlockSpec(w1bd.shape, lambda i: (0, 0, 0)),
            pl.BlockSpec(w2bd.shape, lambda i: (0, 0)),
        ],
        out_specs=pl.BlockSpec((nblk, KV_WIDTH), lambda i: (i, 0)),
        compiler_params=_cparams(("parallel",)),
        name="compress",
    )(x, pe2, w1bd, w2bd)


def _page_copies_by_feature(pt_ref, n, pool_ref, buf_ref, sem_ref, slot, n_pages):
    return [pltpu.make_async_copy(pool_ref.at[pt_ref[n, p]], buf_ref.at[slot, :, :, p], sem_ref.at[slot])
            for p in range(n_pages)]


def _page_copies_by_page(pt_ref, n, pool_ref, buf_ref, sem_ref, slot, n_pages):
    return [pltpu.make_async_copy(pool_ref.at[pt_ref[n, p]], buf_ref.at[slot, p], sem_ref.at[slot])
            for p in range(n_pages)]


def _paged_prefetch(pt_ref, pools, bufs, sems, n_pages, _page_copies):
    n = pl.program_id(0)
    slot = n % 2

    def start(seq, slot_):
        for pool, buf, sem in zip(pools, bufs, sems):
            for cp in _page_copies(pt_ref, seq, pool, buf, sem, slot_, n_pages):
                cp.start()

    @pl.when(n == 0)
    def _():
        start(0, 0)

    @pl.when(n + 1 < pl.num_programs(0))
    def _():
        start(n + 1, 1 - slot)

    for pool, buf, sem in zip(pools, bufs, sems):
        for cp in _page_copies(pt_ref, n, pool, buf, sem, slot, n_pages):
            cp.wait()
    return slot


def _compress_paged_kernel(pt_ref, pk_ref, pv_ref, pek_ref, w1k_ref, w2k_ref, pev_ref, w1v_ref, w2v_ref,
                           ok_ref, ov_ref, bk_ref, bv_ref, sk_ref, sv_ref, *, n_pages):
    slot = _paged_prefetch(pt_ref, (pk_ref, pv_ref), (bk_ref, bv_ref), (sk_ref, sv_ref), n_pages,
                           _page_copies_by_feature)

    def compress(buf_ref, pe_ref, w1_ref, w2_ref):
        acc = [jnp.zeros((n_pages, PAGE_SIZE), F32) for _ in range(N_KV_NSA)]
        for d in range(HEAD_DIM):
            for g in range(N_KV_NSA):
                rows = buf_ref[slot, g, d] + pe_ref[d:d + 1, :]
                acc[g] = acc[g] + _dot(rows.astype(BF16), w1_ref[d])
        out = [_dot(_gelu_tanh(a).astype(BF16), w2_ref[...]) for a in acc]
        return jnp.concatenate([out[g][:, j * HEAD_DIM:(j + 1) * HEAD_DIM]
                                for j in range(PAGE_SIZE // CMP_BLOCK) for g in range(N_KV_NSA)], axis=1)

    ok_ref[0] = compress(bk_ref, pek_ref, w1k_ref, w2k_ref)
    ov_ref[0] = compress(bv_ref, pev_ref, w1v_ref, w2v_ref)


def _compress_paged(page_table, pool_k, pool_v, cw_k, cw_v):
    n, n_pages = page_table.shape
    past = n_pages * PAGE_SIZE
    nblk = past // CMP_BLOCK
    per_page = PAGE_SIZE // CMP_BLOCK
    assert per_page == 2
    wspec = lambda a: pl.BlockSpec(a.shape, lambda i, pt: (0,) * a.ndim)
    grid_spec = pltpu.PrefetchScalarGridSpec(
        num_scalar_prefetch=1,
        grid=(n,),
        in_specs=[pl.BlockSpec(memory_space=pl.ANY), pl.BlockSpec(memory_space=pl.ANY)]
                 + [wspec(a) for a in cw_k + cw_v],
        out_specs=[pl.BlockSpec((1, n_pages, per_page * KV_WIDTH), lambda i, pt: (i, 0, 0))] * 2,
        scratch_shapes=[pltpu.VMEM((2, N_KV_NSA, HEAD_DIM, n_pages, PAGE_SIZE), F32),
                        pltpu.VMEM((2, N_KV_NSA, HEAD_DIM, n_pages, PAGE_SIZE), F32),
                        pltpu.SemaphoreType.DMA((2,)), pltpu.SemaphoreType.DMA((2,))],
    )
    kcc, vcc = pl.pallas_call(
        functools.partial(_compress_paged_kernel, n_pages=n_pages),
        out_shape=[jax.ShapeDtypeStruct((n, n_pages, per_page * KV_WIDTH), F32)] * 2,
        grid_spec=grid_spec,
        compiler_params=_cparams(("arbitrary",)),
        name="compress_paged",
    )(page_table, pool_k, pool_v, *cw_k, *cw_v)
    return kcc.reshape(n, nblk, KV_WIDTH), vcc.reshape(n, nblk, KV_WIDTH)


def _stack_heads(x, g):
    return jnp.concatenate(
        [x[:, (g * GROUP + r) * HEAD_DIM:(g * GROUP + r + 1) * HEAD_DIM] for r in range(GROUP)], axis=0)


def _stack_cols(x, cols):
    return jnp.concatenate([x[:, c:c + 1] for c in cols], axis=0)


def _softmax_step(carry, s, mask, v):
    m, l, acc = carry
    s = jnp.where(mask, s, NEG_INF)
    m_new = jnp.maximum(m, jnp.max(s, axis=1, keepdims=True))
    p = jnp.where(mask, jnp.exp(s - m_new), 0.0)
    alpha = jnp.exp(m - m_new)
    l = alpha * l + jnp.sum(p, axis=1, keepdims=True)
    pv = v(p.astype(BF16)) if callable(v) else _dot(p.astype(BF16), v)
    return m_new, l, alpha * acc + pv


def _softmax_init(rows, width):
    return (jnp.full((rows, 1), NEG_INF, F32), jnp.zeros((rows, 1), F32), jnp.zeros((rows, width), F32))


def _softmax_done(carry):
    _, l, acc = carry
    return acc / jnp.maximum(l, 1e-30)


def _topk_mask(imp, cur, n_blocks):
    blk = lax.broadcasted_iota(jnp.int32, imp.shape, 1)
    forced = (blk == 0) | (blk == cur) | (blk == cur - 1)
    score = jnp.where(blk <= cur, jnp.where(forced, FORCE_SCORE, imp), -FORCE_SCORE)
    rank = jnp.zeros(imp.shape, jnp.int32)
    for b in range(n_blocks):
        col = score[:, b:b + 1]
        ahead = (col > score) | ((col == score) & (blk > b))
        rank = rank + jnp.where(ahead, 1, 0)
    return (rank < TOP_K_BLOCKS) & (blk <= cur)


def _topk_mask_t(imp, cur, n_blocks):
    blk = lax.broadcasted_iota(jnp.int32, imp.shape, 0)
    forced = (blk == 0) | (blk == cur) | (blk == cur - 1)
    score = jnp.where(blk <= cur, jnp.where(forced, FORCE_SCORE, imp), -FORCE_SCORE)
    rank = jnp.zeros(imp.shape, jnp.int32)
    for b in range(n_blocks):
        row = score[b:b + 1, :]
        rank = rank + jnp.where(row > score, 1, jnp.where(row == score, jnp.where(blk > b, 1, 0), 0))
    return jnp.where(rank < TOP_K_BLOCKS, jnp.where(blk <= cur, 1.0, 0.0), 0.0)


def _nsa_prompt_kernel(q_ref, gate_ref, ks_ref, vs_ref, kw_ref, vw_ref, kcc_ref, vcc_ref, bt_ref, bc_ref,
                       o_ref, sel_ref, m_ref, acc_ref, pv_ref, s_ref):
    qi = pl.program_id(1)
    seq = ks_ref.shape[1]
    nb = seq // SLC_BLOCK
    rows = GROUP * QT
    groups = range(N_KV_NSA)
    q = q_ref[...] * ATTN_SCALE
    gates_t = jax.nn.sigmoid(gate_ref[...]).T
    tpos_row = qi * QT + lax.broadcasted_iota(jnp.int32, (1, QT), 1)
    lane = lax.broadcasted_iota(jnp.int32, (1, KV_WIDTH), 1)
    own = [(lane // HEAD_DIM) == g for g in groups]
    zeros = jnp.zeros((rows, HEAD_DIM), F32)
    qg = []
    for g in groups:
        qs = _stack_heads(q, g)
        qg.append(jnp.concatenate([qs, zeros] if g == 0 else [zeros, qs], axis=1).astype(BF16))

    blk = lax.broadcasted_iota(jnp.int32, (nb, 1), 0)
    tpos_stacked = qi * QT + (lax.broadcasted_iota(jnp.int32, (1, rows), 1) & (QT - 1))
    readable = (blk + 1) * CMP_BLOCK - 1 <= tpos_stacked
    kcc = kcc_ref[0].astype(BF16)
    vcc = vcc_ref[0].astype(BF16)
    o_cmp = []
    for g in groups:
        s = jnp.where(readable, _dot_nt(kcc, qg[g]) + bc_ref[g, 0], NEG_INF)
        e = jnp.where(readable, jnp.exp(s - jnp.max(s, axis=0, keepdims=True)), 0.0)
        p = e / jnp.maximum(jnp.sum(e, axis=0, keepdims=True), 1e-30)
        o_cmp.append(lax.dot_general(vcc, p.astype(BF16), (((0,), (0,)), ((), ())), preferred_element_type=F32))
        imp = p[:, 0:QT]
        for r in range(1, GROUP):
            imp = imp + p[:, r * QT:(r + 1) * QT]
        sel_ref[g] = _topk_mask_t(imp, tpos_row // SLC_BLOCK, nb)

    def branch(lo, hi, k_ref, v_ref, penalty):
        def scores(kt):
            k = k_ref[0, pl.ds(pl.multiple_of(kt * QT, QT), QT), :].astype(BF16)
            return tuple(_dot_nt(k, qg[g]) for g in groups)

        def step(kt, _):
            s_next = scores(jnp.minimum(kt + 1, hi - 1))
            off = pl.multiple_of(kt * QT, QT)
            v = v_ref[0, pl.ds(off, QT), :]
            for g in groups:
                m = m_ref[g]
                s = s_ref[g] + bt_ref[g, jnp.minimum(qi - kt, 2)]
                s = s + jnp.concatenate([penalty(g, off)] * GROUP, axis=1)
                m_new = jnp.maximum(m, jnp.max(s, axis=0, keepdims=True))
                p = jnp.exp(s - m_new).astype(BF16)
                v_ext = jnp.where(own[g], v, 1.0).astype(BF16)
                acc_ref[g] = jnp.exp(m - m_new) * (acc_ref[g] + pv_ref[g])
                pv_ref[g] = lax.dot_general(v_ext, p, (((0,), (0,)), ((), ())), preferred_element_type=F32)
                m_ref[g] = m_new
            for g in groups:
                s_ref[g] = s_next[g]
            return 0

        first = scores(lo)
        for g in groups:
            m_ref[g] = jnp.full((1, rows), M_FLOOR, F32)
            acc_ref[g] = jnp.zeros((KV_WIDTH, rows), F32)
            pv_ref[g] = jnp.zeros((KV_WIDTH, rows), F32)
            s_ref[g] = first[g]
        lax.fori_loop(lo, hi, step, 0)
        out = []
        for g in groups:
            acc = acc_ref[g] + pv_ref[g]
            denom = acc[(1 - g) * HEAD_DIM:(1 - g) * HEAD_DIM + 1, :]
            out.append(acc / jnp.maximum(denom, 1e-30))
        return out

    key_row = lax.broadcasted_iota(jnp.int32, (QT, 1), 0)

    def slc_penalty(g, off):
        causal = jnp.where(tpos_row - (off + key_row) >= 0, 0.0, NEG_INF)
        first = off // SLC_BLOCK
        chosen = jnp.where(key_row < SLC_BLOCK, sel_ref[g, pl.ds(first, 1), :], sel_ref[g, pl.ds(first + 1, 1), :])
        return jnp.where(chosen > 0.5, causal, NEG_INF)

    def win_penalty(g, off):
        dist = tpos_row - (off + key_row)
        return jnp.where(dist >= 0, jnp.where(dist <= WINDOW, 0.0, NEG_INF), NEG_INF)

    slc = branch(0, qi + 1, ks_ref, vs_ref, slc_penalty)
    win = branch(jnp.maximum(qi - WINDOW // QT, 0), qi + 1, kw_ref, vw_ref, win_penalty)

    def gate_row(branch, g):
        first = branch * N_HEADS_NSA + g * GROUP
        return jnp.concatenate([gates_t[first + r:first + r + 1, :] for r in range(GROUP)], axis=1)

    o_t = [gate_row(0, g) * o_cmp[g] + gate_row(1, g) * slc[g] + gate_row(2, g) * win[g] for g in groups]
    feature_row = lax.broadcasted_iota(jnp.int32, (KV_WIDTH, 1), 0)
    o = jnp.where(feature_row < HEAD_DIM, o_t[0], o_t[1]).T
    for g in groups:
        for r in range(GROUP):
            h = g * GROUP + r
            o_ref[:, h * HEAD_DIM:(h + 1) * HEAD_DIM] = o[r * QT:(r + 1) * QT, g * HEAD_DIM:(g + 1) * HEAD_DIM]


def _nsa_prompt(q, gate, ks, vs, kw, vw, kcc, vcc, bias_tiles, bias_cmp, n, seq):
    nq = seq // QT
    nb = seq // SLC_BLOCK
    seq_spec = pl.BlockSpec((1, seq, KV_WIDTH), lambda b, i: (b, 0, 0))
    cmp_spec = pl.BlockSpec((1, nb, KV_WIDTH), lambda b, i: (b, 0, 0))
    return pl.pallas_call(
        _nsa_prompt_kernel,
        out_shape=jax.ShapeDtypeStruct((n * seq, NSA_WIDTH), F32),
        grid=(n, nq),
        in_specs=[
            pl.BlockSpec((QT, NSA_WIDTH), lambda b, i: (b * nq + i, 0)),
            pl.BlockSpec((QT, LANES), lambda b, i: (b * nq + i, 0)),
            seq_spec, seq_spec, seq_spec, seq_spec, cmp_spec, cmp_spec,
            pl.BlockSpec(bias_tiles.shape, lambda b, i: (0, 0, 0, 0)),
            pl.BlockSpec((N_KV_NSA, 1, nb, GROUP * QT), lambda b, i: (0, i, 0, 0)),
        ],
        out_specs=pl.BlockSpec((QT, NSA_WIDTH), lambda b, i: (b * nq + i, 0)),
        scratch_shapes=[pltpu.VMEM((N_KV_NSA, nb, QT), F32),
                        pltpu.VMEM((N_KV_NSA, 1, GROUP * QT), F32),
                        pltpu.VMEM((N_KV_NSA, KV_WIDTH, GROUP * QT), F32),
                        pltpu.VMEM((N_KV_NSA, KV_WIDTH, GROUP * QT), F32),
                        pltpu.VMEM((N_KV_NSA, QT, GROUP * QT), F32)],
        compiler_params=_cparams(("parallel", "arbitrary")),
        name="nsa_prompt",
    )(q, gate, ks.reshape(n, seq, KV_WIDTH), vs.reshape(n, seq, KV_WIDTH), kw.reshape(n, seq, KV_WIDTH),
      vw.reshape(n, seq, KV_WIDTH), kcc.reshape(n, nb, KV_WIDTH), vcc.reshape(n, nb, KV_WIDTH),
      bias_tiles, bias_cmp)


NEW_PAD = SUBLANES


def _row_token(t):
    return jnp.concatenate([lax.broadcasted_iota(jnp.int32, (t, 1), 0)] * GROUP, axis=0)


def _nsa_sample_cmp_kernel(q_ref, kcc_ref, vcc_ref, bc_ref, o_ref, sel_ref, *, past):
    nseq, t = q_ref.shape[:2]
    nbp = kcc_ref.shape[1]
    width = sel_ref.shape[3]
    pos = past + _row_token(t)
    blk_end = (lax.broadcasted_iota(jnp.int32, (1, nbp), 1) + 1) * CMP_BLOCK - 1
    cur = (past + lax.broadcasted_iota(jnp.int32, (t, 1), 0)) // SLC_BLOCK
    mask = blk_end <= pos
    for g in range(N_KV_NSA):
        cols = slice(g * HEAD_DIM, (g + 1) * HEAD_DIM)
        imps = []
        for b in range(nseq):
            qg = _stack_heads(q_ref[b], g).astype(BF16)
            s = _dot_nt(qg, kcc_ref[b][:, cols].astype(BF16)) * ATTN_SCALE + bc_ref[g]
            s = jnp.where(mask, s, NEG_INF)
            e = jnp.where(mask, jnp.exp(s - jnp.max(s, axis=1, keepdims=True)), 0.0)
            p = e / jnp.maximum(jnp.sum(e, axis=1, keepdims=True), 1e-30)
            o_ref[b, g] = _dot(p.astype(BF16), vcc_ref[b][:, cols].astype(BF16))
            imp = p[0:t]
            for r in range(1, GROUP):
                imp = imp + p[r * t:(r + 1) * t]
            imps.append(jnp.concatenate([imp, jnp.zeros((t, width - nbp), F32)], axis=1))
        sel = _topk_mask(jnp.concatenate(imps, axis=0), jnp.concatenate([cur] * nseq, axis=0), nbp + 1)
        sel = jnp.where(sel, 1.0, 0.0)
        for b in range(nseq):
            sel_ref[b, g] = sel[b * t:(b + 1) * t]


CMP_SEQS = 8


def _nsa_sample_cmp(q3, kcc, vcc, bias_cmp, past):
    n, t, _ = q3.shape
    nbp = kcc.shape[1]
    width = -(-(nbp + 1) // LANES) * LANES
    nseq = CMP_SEQS if n % CMP_SEQS == 0 else 1
    return pl.pallas_call(
        functools.partial(_nsa_sample_cmp_kernel, past=past),
        out_shape=[jax.ShapeDtypeStruct((n, N_KV_NSA, GROUP * t, HEAD_DIM), F32),
                   jax.ShapeDtypeStruct((n, N_KV_NSA, t, width), F32)],
        grid=(n // nseq,),
        in_specs=[
            pl.BlockSpec((nseq, t, NSA_WIDTH), lambda i: (i, 0, 0)),
            pl.BlockSpec((nseq, nbp, KV_WIDTH), lambda i: (i, 0, 0)),
            pl.BlockSpec((nseq, nbp, KV_WIDTH), lambda i: (i, 0, 0)),
            pl.BlockSpec(bias_cmp.shape, lambda i: (0, 0, 0)),
        ],
        out_specs=[pl.BlockSpec((nseq, N_KV_NSA, GROUP * t, HEAD_DIM), lambda i: (i, 0, 0, 0)),
                   pl.BlockSpec((nseq, N_KV_NSA, t, width), lambda i: (i, 0, 0, 0))],
        compiler_params=_cparams(("parallel",)),
        name="nsa_sample_cmp",
    )(q3, kcc, vcc, bias_cmp)


KEY_CHUNK = 2048


def _nsa_sample_attn_kernel(pt_ref, q_ref, gate_ref, ocmp_ref, sel_ref, ksn_ref, vsn_ref, kwn_ref, vwn_ref,
                            cwk_ref, cwv_ref, pk_ref, pv_ref, bs_ref, bn_ref, bw_ref, ex_ref,
                            o_ref, bk_ref, bv_ref, sk_ref, sv_ref, selx_ref, *, n_pages, key_chunk):
    slot = _paged_prefetch(pt_ref, (pk_ref, pv_ref), (bk_ref, bv_ref), (sk_ref, sv_ref), n_pages,
                           _page_copies_by_page)
    t = q_ref.shape[1]
    rows = N_KV_NSA * GROUP * t
    past = n_pages * PAGE_SIZE
    nbp = past // SLC_BLOCK
    wb = cwk_ref.shape[3]
    groups = range(N_KV_NSA)
    q = q_ref[0] * ATTN_SCALE
    gates = jax.nn.sigmoid(gate_ref[0])
    zeros = jnp.zeros((GROUP * t, HEAD_DIM), F32)

    def own_lanes(x, g):
        return jnp.concatenate([x, zeros] if g == 0 else [zeros, x], axis=1)

    qa = jnp.concatenate([own_lanes(_stack_heads(q, g), g) for g in groups], axis=0).astype(BF16)
    sel = jnp.concatenate([sel_ref[0, g] for g in groups for _ in range(GROUP)], axis=0)
    selx_ref[...] = _dot(sel[:, :nbp].astype(BF16), ex_ref[...])
    tok = jnp.concatenate([_row_token(t)] * N_KV_NSA, axis=0)
    jn = lax.broadcasted_iota(jnp.int32, (1, NEW_PAD), 1)
    new_mask = jn <= tok
    jw = lax.broadcasted_iota(jnp.int32, (1, wb), 1)
    wdist = wb + tok - jw
    win_mask = (wdist >= 0) & (wdist <= WINDOW)
    bias_new = bn_ref[:, :NEW_PAD]
    pad = jnp.zeros((NEW_PAD - t, KV_WIDTH), F32)

    def new_rows(ref):
        return jnp.concatenate([ref[0], pad], axis=0).astype(BF16)

    pages_per_chunk = key_chunk // PAGE_SIZE
    qg = [_stack_heads(q, g).astype(BF16) for g in groups]

    def scores(k_t):
        return jnp.concatenate([_dot(qg[g], k_t(g)) for g in groups], axis=0)

    def values(v_t):
        return lambda p: jnp.concatenate(
            [own_lanes(_dot_nt(p[g * GROUP * t:(g + 1) * GROUP * t], v_t(g)), g) for g in groups], axis=0)

    def slc_step(c, carry):
        off = pl.multiple_of(c * key_chunk, key_chunk)

        def chunk(buf_ref):
            return lambda g: jnp.concatenate([buf_ref[slot, c * pages_per_chunk + i, g]
                                              for i in range(pages_per_chunk)], axis=1).astype(BF16)

        s = scores(chunk(bk_ref)) + bs_ref[:, pl.ds(off, key_chunk)]
        return _softmax_step(carry, s, selx_ref[:, pl.ds(off, key_chunk)] > 0.5, values(chunk(bv_ref)))

    carry = lax.fori_loop(0, past // key_chunk, slc_step, _softmax_init(rows, KV_WIDTH))
    s = _dot_nt(qa, new_rows(ksn_ref)) + bias_new
    carry = _softmax_step(carry, s, new_mask & (sel[:, nbp:nbp + 1] > 0.5), new_rows(vsn_ref))
    o_slc = _softmax_done(carry)

    s = scores(lambda g: cwk_ref[0, g].astype(BF16)) + bw_ref[...]
    carry = _softmax_step(_softmax_init(rows, KV_WIDTH), s, win_mask, values(lambda g: cwv_ref[0, g].astype(BF16)))
    s = _dot_nt(qa, new_rows(kwn_ref)) + bias_new
    carry = _softmax_step(carry, s, new_mask, new_rows(vwn_ref))
    o_win = _softmax_done(carry)

    def gate_col(branch):
        return _stack_cols(gates, [branch * N_HEADS_NSA + h for h in range(N_HEADS_NSA)])

    o_cmp = jnp.concatenate([own_lanes(ocmp_ref[0, g], g) for g in groups], axis=0)
    o = gate_col(0) * o_cmp + gate_col(1) * o_slc + gate_col(2) * o_win
    for h in range(N_HEADS_NSA):
        g = h // GROUP
        o_ref[0, :, h * HEAD_DIM:(h + 1) * HEAD_DIM] = o[h * t:(h + 1) * t, g * HEAD_DIM:(g + 1) * HEAD_DIM]


def _nsa_sample_attn(page_table, q3, gate3, o_cmp, sel, ksn, vsn, kwn, vwn, cwk, cwv, pool_k, pool_v,
                     bias_slc, bias_new, bias_win, expand):
    n, n_pages = page_table.shape
    t = q3.shape[1]
    past = n_pages * PAGE_SIZE
    wb = cwk.shape[3]
    width = sel.shape[3]
    key_chunk = min(KEY_CHUNK, past)
    assert past % key_chunk == 0 and key_chunk % PAGE_SIZE == 0 and t <= NEW_PAD
    per_seq = lambda shape: pl.BlockSpec((1,) + shape, lambda i, pt: (i,) + (0,) * len(shape))
    full = lambda a: pl.BlockSpec(a.shape, lambda i, pt: (0,) * a.ndim)
    anyspec = pl.BlockSpec(memory_space=pl.ANY)
    grid_spec = pltpu.PrefetchScalarGridSpec(
        num_scalar_prefetch=1,
        grid=(n,),
        in_specs=[per_seq((t, NSA_WIDTH)), per_seq((t, LANES)), per_seq((N_KV_NSA, GROUP * t, HEAD_DIM)),
                  per_seq((N_KV_NSA, t, width)),
                  per_seq((t, KV_WIDTH)), per_seq((t, KV_WIDTH)), per_seq((t, KV_WIDTH)), per_seq((t, KV_WIDTH)),
                  per_seq((N_KV_NSA, HEAD_DIM, wb)), per_seq((N_KV_NSA, HEAD_DIM, wb)), anyspec, anyspec,
                  full(bias_slc), full(bias_new), full(bias_win), full(expand)],
        out_specs=per_seq((t, NSA_WIDTH)),
        scratch_shapes=[pltpu.VMEM((2, n_pages, N_KV_NSA, HEAD_DIM, PAGE_SIZE), F32),
                        pltpu.VMEM((2, n_pages, N_KV_NSA, HEAD_DIM, PAGE_SIZE), F32),
                        pltpu.SemaphoreType.DMA((2,)), pltpu.SemaphoreType.DMA((2,)),
                        pltpu.VMEM((N_KV_NSA * GROUP * t, past), F32)],
    )
    return pl.pallas_call(
        functools.partial(_nsa_sample_attn_kernel, n_pages=n_pages, key_chunk=key_chunk),
        out_shape=jax.ShapeDtypeStruct((n, t, NSA_WIDTH), F32),
        grid_spec=grid_spec,
        compiler_params=_cparams(("arbitrary",)),
        name="nsa_sample_attn",
    )(page_table, q3, gate3, o_cmp, sel, ksn, vsn, kwn, vwn, cwk, cwv, pool_k, pool_v,
      bias_slc, bias_new, bias_win, expand)


def _segsum(x, ones_ref):
    hi = x.astype(BF16)
    lo = (x - hi.astype(F32)).astype(BF16)
    return _dot(hi, ones_ref[...]) + _dot(lo, ones_ref[...])


def _softplus(x):
    return jnp.maximum(x, 0.0) + jnp.log(1.0 + jnp.exp(-jnp.abs(x)))


def _rwkv_pre_kernel(p_ref, prev_ref, mu_ref, w0_ref, w2_ref, a0_ref, a2_ref, g2_ref, kk_ref, ka_ref, ones_ref,
                     r_ref, k_ref, v_ref, kkn_ref, kka_ref, d_ref, g_ref, last_ref, buf_ref):
    ti = pl.program_id(1)
    tt = p_ref.shape[1]
    p = p_ref[0]
    first = jnp.where(ti == 0, prev_ref[0], last_ref[...])
    last_ref[...] = p[tt - 1:tt]
    buf_ref[SUBLANES - 1:SUBLANES, :] = first
    buf_ref[SUBLANES:SUBLANES + tt, :] = p
    prev = buf_ref[SUBLANES - 1:SUBLANES - 1 + tt, :]
    xs = p + (prev - p) * mu_ref[...]
    W = RWKV_WIDTH
    r, k, v = xs[:, 0:W], xs[:, W:2 * W], xs[:, 2 * W:3 * W]
    c = 3 * W
    xw = xs[:, c:c + DECAY_LORA]
    xa = xs[:, c + DECAY_LORA:c + DECAY_LORA + AAA_LORA]
    xg = xs[:, c + DECAY_LORA + AAA_LORA:]
    w = -_softplus(-(w0_ref[...] + _dot(jnp.tanh(xw).astype(BF16), w2_ref[...]))) - 0.5
    a = jax.nn.sigmoid(a0_ref[...] + _dot(xa.astype(BF16), a2_ref[...]))
    kk = k * kk_ref[...]
    kk = kk / jnp.maximum(jnp.sqrt(_segsum(kk * kk, ones_ref)), 1e-12)
    r_ref[0] = r
    k_ref[0] = k * (1.0 + (a - 1.0) * ka_ref[...])
    v_ref[0] = v
    kkn_ref[0] = kk
    kka_ref[0] = kk * a
    d_ref[0] = -jnp.exp(w)
    g_ref[0] = _dot(jax.nn.sigmoid(xg).astype(BF16), g2_ref[...])


def _rwkv_pre(p3, prev, mu, w0, w2, a0, a2, g2, k_k, k_a, ones_bd):
    n, t, c = p3.shape
    tt = min(t, 256)
    assert t % tt == 0
    row = lambda a: a.reshape(1, -1)
    full = lambda a: pl.BlockSpec(a.shape, lambda b, i: (0,) * a.ndim)
    params = [row(mu), row(w0), w2, row(a0), a2, g2, row(k_k), row(k_a), ones_bd]
    out_spec = pl.BlockSpec((1, tt, RWKV_WIDTH), lambda b, i: (b, i, 0))
    return pl.pallas_call(
        _rwkv_pre_kernel,
        out_shape=[jax.ShapeDtypeStruct((n, t, RWKV_WIDTH), F32)] * 7,
        grid=(n, t // tt),
        in_specs=[pl.BlockSpec((1, tt, c), lambda b, i: (b, i, 0)),
                  pl.BlockSpec((1, 1, c), lambda b, i: (b, 0, 0))] + [full(a) for a in params],
        out_specs=[out_spec] * 7,
        scratch_shapes=[pltpu.VMEM((1, c), F32), pltpu.VMEM((SUBLANES + tt, c), F32)],
        compiler_params=_cparams(("parallel", "arbitrary")),
        name="rwkv_pre",
    )(p3, prev.reshape(n, 1, c), *params)


def _rwkv_scan_kernel(r_ref, k_ref, v_ref, kk_ref, kka_ref, ld_ref, s0_ref, y_ref, sout_ref, s_ref, *, C):
    ci = pl.program_id(1)
    nseq, t_in = r_ref.shape[:2]

    @pl.when(ci == 0)
    def _():
        s_ref[...] = s0_ref[...]

    def rows(ref, b):
        x = ref[b]
        if t_in < C:
            x = jnp.concatenate([x, jnp.zeros((C - t_in, x.shape[1]), F32)], axis=0)
        return x

    tri = jnp.where(lax.broadcasted_iota(jnp.int32, (C, C), 1) <= lax.broadcasted_iota(jnp.int32, (C, C), 0),
                    1.0, 0.0).astype(BF16)
    i2 = lax.broadcasted_iota(jnp.int32, (2 * C, 2 * C), 0)
    j2 = lax.broadcasted_iota(jnp.int32, (2 * C, 2 * C), 1)
    src = j2 & (C - 1)
    keep = src < jnp.where(i2 < C, i2, i2 - C + 1)

    heads = range(nseq * N_HEADS_RWKV)
    ar, bk, vh, s0, p_end = [], [], [], [], []
    for b in range(nseq):
        ld = rows(ld_ref, b)
        hi = ld.astype(BF16)
        rem = ld - hi.astype(F32)
        mid = rem.astype(BF16)
        lo = (rem - mid.astype(F32)).astype(BF16)
        cum = _dot(tri, hi) + _dot(tri, mid) + _dot(tri, lo)
        p_inc = jnp.exp(cum)
        p_inv = jnp.exp(-cum)
        a_t = -rows(kk_ref, b) * jnp.exp(cum - ld)
        b_t = rows(kka_ref, b) * p_inv
        k_t = rows(k_ref, b) * p_inv
        r_t = rows(r_ref, b) * p_inc
        v = rows(v_ref, b)
        for h in range(N_HEADS_RWKV):
            c = slice(h * HEAD_DIM, (h + 1) * HEAD_DIM)
            ar.append(jnp.concatenate([a_t[:, c], r_t[:, c]], axis=0).astype(BF16))
            bk.append(jnp.concatenate([b_t[:, c], k_t[:, c]], axis=0).astype(BF16))
            vh.append(v[:, c].astype(BF16))
            s0.append(s_ref[b, h])
            p_end.append(p_inc[C - 1:C, c])
    quad = [jnp.where(keep, _dot_nt(ar[h], bk[h]), 0.0).astype(BF16) for h in heads]
    base = [_dot_nt(ar[h], s0[h].astype(BF16)) for h in heads]
    x = [base[h][:C] + _dot(quad[h][:C, C:], vh[h]) for h in heads]
    power = [quad[h][:C, :C] for h in heads]
    levels = C.bit_length() - 1
    for lvl in range(levels):
        x = [x[h] + _dot(power[h], x[h].astype(BF16)) for h in heads]
        if lvl + 1 < levels:
            power = [_dot(power[h], power[h]).astype(BF16) for h in heads]
    uv = [jnp.concatenate([x[h].astype(BF16), vh[h]], axis=0) for h in heads]
    ys = [base[h][C:] + _dot(quad[h][C:, :], uv[h]) for h in heads]
    for h in heads:
        upd = lax.dot_general(uv[h], bk[h], (((0,), (0,)), ((), ())), preferred_element_type=F32)
        s_ref[h // N_HEADS_RWKV, h % N_HEADS_RWKV] = (s0[h] + upd) * p_end[h]
    for b in range(nseq):
        y = jnp.concatenate(ys[b * N_HEADS_RWKV:(b + 1) * N_HEADS_RWKV], axis=1)
        y_ref[b] = y[:t_in]

    @pl.when(ci == pl.num_programs(1) - 1)
    def _():
        sout_ref[...] = s_ref[...]


SCAN_SEQS = 4


def _rwkv_scan(r, k, v, kk, kka, d, s0):
    n, t, w = r.shape
    tc = min(t, TC)
    chunk = TC if t >= TC else max(SUBLANES, 1 << (t - 1).bit_length())
    nseq = SCAN_SEQS if n % SCAN_SEQS == 0 else 1
    assert t % tc == 0 and chunk & (chunk - 1) == 0
    seq_spec = pl.BlockSpec((nseq, tc, w), lambda b, i: (b, i, 0))
    st_spec = pl.BlockSpec((nseq, N_HEADS_RWKV, HEAD_DIM, HEAD_DIM), lambda b, i: (b, 0, 0, 0))
    return pl.pallas_call(
        functools.partial(_rwkv_scan_kernel, C=chunk),
        out_shape=[jax.ShapeDtypeStruct((n, t, w), F32),
                   jax.ShapeDtypeStruct((n, N_HEADS_RWKV, HEAD_DIM, HEAD_DIM), F32)],
        grid=(n // nseq, t // tc),
        in_specs=[seq_spec] * 6 + [st_spec],
        out_specs=[seq_spec, st_spec],
        scratch_shapes=[pltpu.VMEM((nseq, N_HEADS_RWKV, HEAD_DIM, HEAD_DIM), F32)],
        compiler_params=_cparams(("parallel", "arbitrary")),
        name="rwkv_scan",
    )(r, k, v, kk, kka, d, s0)


def _mix_out_kernel(x_ref, nsa_ref, y_ref, r_ref, k_ref, v_ref, g_ref, rk_ref, lnw_ref, lnb_ref, ones_ref,
                    wa_ref, wb_ref, o_ref):
    y = y_ref[...]
    mean = _segsum(y, ones_ref) * (1.0 / HEAD_DIM)
    yc = y - mean
    var = _segsum(yc * yc, ones_ref) * (1.0 / HEAD_DIM)
    yn = yc * lax.rsqrt(var + GN_EPS) * lnw_ref[...] + lnb_ref[...]
    bonus = _segsum(r_ref[...] * k_ref[...] * rk_ref[...], ones_ref) * v_ref[...]
    rw = ((yn + bonus) * g_ref[...]).astype(BF16)
    o_ref[...] = x_ref[...] + _dot(nsa_ref[...].astype(BF16), wa_ref[...]) + _dot(rw, wb_ref[...])


def _mix_out(x, o_nsa, y, r, k, v, g, r_k, ln_w, ln_b, ones_bd, wa, wb):
    m, d = x.shape
    tm = min(m, 512)
    assert m % tm == 0
    row = lambda a: a.reshape(1, -1)
    tile = lambda w: pl.BlockSpec((tm, w), lambda i: (i, 0))
    full = lambda a: pl.BlockSpec(a.shape, lambda i: (0,) * a.ndim)
    params = [row(r_k), row(ln_w), row(ln_b), ones_bd, wa, wb]
    return pl.pallas_call(
        _mix_out_kernel,
        out_shape=jax.ShapeDtypeStruct((m, d), F32),
        grid=(m // tm,),
        in_specs=[tile(d), tile(NSA_WIDTH)] + [tile(RWKV_WIDTH)] * 5 + [full(a) for a in params],
        out_specs=tile(d),
        compiler_params=_cparams(("parallel",)),
        name="mix_out",
    )(x, o_nsa, y, r, k, v, g, *params)


def _block_diag2(w):
    z = jnp.zeros_like(w)
    return jnp.concatenate([jnp.concatenate([w, z], axis=-1), jnp.concatenate([z, w], axis=-1)], axis=-2)


def _compress_weights_t(pe, w1, w2):
    per_page = PAGE_SIZE // CMP_BLOCK
    return (jnp.concatenate([pe.T] * per_page, axis=1), _block_diag2(jnp.transpose(w1, (1, 0, 2))).astype(BF16),
            _block_diag2(w2).astype(BF16))


def _compress_weights(pe, w1, w2):
    return (jnp.concatenate([pe] * N_KV_NSA, axis=1), _block_diag2(w1).astype(BF16), _block_diag2(w2).astype(BF16))


def kernel(x_prompt, x_sample, cache_cmp_k, cache_cmp_v, cache_slc_k, cache_slc_v, cache_win_k, cache_win_v,
           state_wkv, state_shift, page_table, rel_bias_table, ffn1_norm, ffn1_wg, ffn1_wu, ffn1_wd, mix_norm,
           w_in, cmp_pe_k, cmp_w1_k, cmp_w2_k, cmp_pe_v, cmp_w1_v, cmp_w2_v, shift_mu, decay_w0, decay_w2,
           aaa_a0, aaa_a2, gate_g2, k_k, k_a, r_k, ln_x_w, ln_x_b, w_out, ffn2_norm, ffn2_wg, ffn2_wu, ffn2_wd,
           final_norm):
    bp, seq, d = x_prompt.shape
    bs, ts, _ = x_sample.shape
    depth = w_in.shape[0]
    n_pages = page_table.shape[1]
    past = n_pages * PAGE_SIZE
    n_phys = cache_cmp_k.shape[1]
    wbuf = cache_win_k.shape[2]
    assert seq % QT == 0 and seq >= WINDOW and ts < CMP_BLOCK and wbuf == WINDOW
    nq = seq // QT
    nb = seq // SLC_BLOCK
    nbp = past // SLC_BLOCK
    nsa_main = NSA_WIDTH + 6 * KV_WIDTH
    nsa_cols = nsa_main + N_GATES

    bias_tiles = _bias_tiles(rel_bias_table)
    bias_cmp_p = _bias_rows(rel_bias_table, q0=0, rows=seq, n_keys=nb, key_first=CMP_BLOCK - 1, key_stride=CMP_BLOCK)
    bias_cmp_p = bias_cmp_p.reshape(N_KV_NSA, GROUP, nq, QT, nb).transpose(0, 2, 4, 1, 3).reshape(
        N_KV_NSA, nq, nb, GROUP * QT)
    bias_cmp_s = _bias_rows(rel_bias_table, q0=past, rows=ts, n_keys=nbp, key_first=CMP_BLOCK - 1,
                            key_stride=CMP_BLOCK).reshape(N_KV_NSA, GROUP * ts, nbp)
    stacked = lambda b: b.reshape(N_HEADS_NSA * ts, b.shape[-1])
    bias_slc_s = stacked(_bias_rows(rel_bias_table, q0=past, rows=ts, n_keys=past, key_first=0, key_stride=1))
    bias_new_s = stacked(_bias_rows(rel_bias_table, q0=past, rows=ts, n_keys=LANES, key_first=past, key_stride=1))
    bias_win_s = stacked(_bias_rows(rel_bias_table, q0=past, rows=ts, n_keys=wbuf,
                                    key_first=past - wbuf, key_stride=1))
    expand_s = (jnp.arange(past, dtype=jnp.int32)[None, :] // SLC_BLOCK
                == jnp.arange(nbp, dtype=jnp.int32)[:, None]).astype(BF16)
    head_of = jnp.arange(RWKV_WIDTH, dtype=jnp.int32) // HEAD_DIM
    ones_bd = (head_of[:, None] == head_of[None, :]).astype(BF16)

    xp = x_prompt.reshape(bp * seq, d)
    xs = x_sample.reshape(bs * ts, d)
    states = []
    for l in range(depth):
        bf = lambda a: a[l].astype(BF16)
        w_in_l = w_in[l]
        w_pad = jnp.concatenate(
            [w_in_l[:, :nsa_main], w_in_l[:, nsa_main:nsa_cols], jnp.zeros((d, LANES - N_GATES), F32),
             w_in_l[:, nsa_cols:]], axis=1).astype(BF16)
        cw_k = _compress_weights(cmp_pe_k[l], cmp_w1_k[l], cmp_w2_k[l])
        cw_v = _compress_weights(cmp_pe_v[l], cmp_w1_v[l], cmp_w2_v[l])
        wa, wb = w_out[l][:NSA_WIDTH].astype(BF16), w_out[l][NSA_WIDTH:].astype(BF16)
        rw_pre = (shift_mu[l], decay_w0[l], bf(decay_w2), aaa_a0[l], bf(aaa_a2), bf(gate_g2), k_k[l], k_a[l], ones_bd)
        ffn1 = (ffn1_norm[l], bf(ffn1_wg), bf(ffn1_wu), bf(ffn1_wd), final_norm)
        ffn2 = (ffn2_norm[l], bf(ffn2_wg), bf(ffn2_wu), bf(ffn2_wd), final_norm)
        last = l == depth - 1

        xp = _ffn_half(xp, *ffn1, final=False)
        xs = _ffn_half(xs, *ffn1, final=False)

        q, kc, vc, ks, vs, kw, vw, gate, zr, *state_t = _proj(xp, mix_norm[l], w_pad, seq=seq)
        kcc = _compress(kc, *cw_k)
        vcc = _compress(vc, *cw_v)
        o_nsa = _nsa_prompt(q, gate, ks, vs, kw, vw, kcc, vcc, bias_tiles, bias_cmp_p, bp, seq)
        zr3 = zr.reshape(bp, seq, RWKV_COLS)
        r, k2, v, kk, kka, dec, gg = _rwkv_pre(zr3, jnp.zeros((bp, RWKV_COLS), F32), *rw_pre)
        y, wkv_p = _rwkv_scan(r, k2, v, kk, kka, dec,
                              jnp.zeros((bp, N_HEADS_RWKV, HEAD_DIM, HEAD_DIM), F32))
        flat = lambda a: a.reshape(-1, RWKV_WIDTH)
        xp = _mix_out(xp, o_nsa, flat(y), flat(r), flat(k2), flat(v), flat(gg), r_k[l], ln_x_w[l], ln_x_b[l],
                      ones_bd, wa, wb)
        kv5 = lambda a, n_, t_: a.reshape(n_, t_, N_KV_NSA, HEAD_DIM)
        token_major = lambda a: jnp.transpose(a.reshape(bp, N_KV_NSA, HEAD_DIM, seq), (0, 3, 1, 2))
        p_state = (*(token_major(a) for a in state_t),
                   kv5(kw, bp, seq)[:, seq - WINDOW:], kv5(vw, bp, seq)[:, seq - WINDOW:],
                   wkv_p, zr3[:, seq - 1])

        q, kc, vc, ks, vs, kw, vw, gate, zr = _proj(xs, mix_norm[l], w_pad)
        feature_major = lambda c: jnp.transpose(c[l], (0, 2, 3, 1))
        kcc, vcc = _compress_paged(page_table, feature_major(cache_cmp_k), feature_major(cache_cmp_v),
                                   _compress_weights_t(cmp_pe_k[l], cmp_w1_k[l], cmp_w2_k[l]),
                                   _compress_weights_t(cmp_pe_v[l], cmp_w1_v[l], cmp_w2_v[l]))
        seq3 = lambda a: a.reshape(bs, ts, a.shape[-1])
        o_cmp, sel = _nsa_sample_cmp(seq3(q), kcc, vcc, bias_cmp_s, past)
        cwk = feature_major(cache_win_k)
        cwv = feature_major(cache_win_v)
        o_nsa = _nsa_sample_attn(page_table, seq3(q), seq3(gate), o_cmp, sel, seq3(ks), seq3(vs), seq3(kw),
                                 seq3(vw), cwk, cwv, feature_major(cache_slc_k), feature_major(cache_slc_v),
                                 bias_slc_s, bias_new_s, bias_win_s, expand_s)
        zr3 = seq3(zr)
        r, k2, v, kk, kka, dec, gg = _rwkv_pre(zr3, state_shift[l], *rw_pre)
        y, wkv_s = _rwkv_scan(r, k2, v, kk, kka, dec, state_wkv[l])
        xs = _mix_out(xs, o_nsa.reshape(bs * ts, NSA_WIDTH), flat(y), flat(r), flat(k2), flat(v), flat(gg),
                      r_k[l], ln_x_w[l], ln_x_b[l], ones_bd, wa, wb)
        win = lambda cache, new: jnp.concatenate([cache[l], kv5(new, bs, ts)], axis=1)[:, ts:]
        s_state = (kv5(kc, bs, ts), kv5(vc, bs, ts), kv5(ks, bs, ts), kv5(vs, bs, ts),
                   win(cache_win_k, kw), win(cache_win_v, vw), wkv_s, zr3[:, ts - 1])

        xp = _ffn_half(xp, *ffn2, final=last)
        xs = _ffn_half(xs, *ffn2, final=last)
        states.append(p_state + s_state)

    stacked_states = [jnp.stack(z) for z in zip(*states)]
    return (xp.reshape(bp, seq, d), xs.reshape(bs, ts, d), *stacked_states)
```
